```python
import math
import jax, jax.numpy as jnp
from jax import lax
import numpy as np

D_MODEL = 1024
BATCH = 8
SEQ = 2048
DEPTH = 4

CONV_DIM = 512
CONV_WIDTH = 31
N_GROUPS = 3
HEADS_PER_GROUP = 8
HEAD_DIM = 64
N_HEADS = N_GROUPS * HEADS_PER_GROUP
ATTN_DIM = N_HEADS * HEAD_DIM
ATTN_OUT_DIM = HEADS_PER_GROUP * HEAD_DIM
WINDOWS = (128, 512, 2048)
DILATIONS = (1, 4, 16)
SUB_WINDOW = 128
BLOCK = 128
NUM_BUCKETS = 32
MAX_REL_DISTANCE = 2048
D_FF = 4 * D_MODEL
EPS = 1e-6
NEG_INF = -1e30
IN_COLS = 2 * CONV_DIM + 3 * ATTN_DIM + 2 * D_MODEL

kernel_name = "hybrid_conformer_conv_dilated_attn_gated"


def rms_norm(x, g):
    xf = x.astype(jnp.float32)
    y = xf * lax.rsqrt(jnp.mean(xf * xf, axis=-1, keepdims=True) + EPS)
    return (y * g.astype(jnp.float32)).astype(x.dtype)


def layer_norm(x, g, b):
    xf = x.astype(jnp.float32)
    mu = jnp.mean(xf, axis=-1, keepdims=True)
    xc = xf - mu
    y = xc * lax.rsqrt(jnp.mean(xc * xc, axis=-1, keepdims=True) + EPS)
    return (y * g.astype(jnp.float32) + b.astype(jnp.float32)).astype(x.dtype)


def t5_bucket(dist):
    max_exact = NUM_BUCKETS // 2
    nf = jnp.maximum(dist, 1).astype(jnp.float32)
    large = max_exact + (jnp.log(nf / max_exact) / math.log(MAX_REL_DISTANCE / max_exact)
                         * (NUM_BUCKETS - max_exact)).astype(jnp.int32)
    large = jnp.minimum(large, NUM_BUCKETS - 1)
    return jnp.where(dist < max_exact, dist, large)


def conformer_conv(u, dw_w, dw_b, ln_g, ln_b, w_pw):
    a, gt = jnp.split(u, 2, axis=-1)
    z = a * jax.nn.sigmoid(gt)
    z = lax.conv_general_dilated(
        z, dw_w[:, None, :].astype(z.dtype), window_strides=(1,),
        padding=((CONV_WIDTH - 1, 0),),
        dimension_numbers=('NWC', 'WIO', 'NWC'),
        feature_group_count=CONV_DIM) + dw_b
    z = jax.nn.silu(layer_norm(z, ln_g, ln_b))
    return z @ w_pw


def dilated_group(q, k, v, bias_g, d):
    B, S, H, Dh = q.shape
    L = S // d
    nb = -(-L // BLOCK)
    Lp = nb * BLOCK

    def to_blocks(t):
        t = t.reshape(B, L, d, H, Dh)
        t = jnp.pad(t, ((0, 0), (0, Lp - L), (0, 0), (0, 0), (0, 0)))
        return t.reshape(B, nb, BLOCK, d, H, Dh)

    def band_keys(t):
        prev = jnp.pad(t, ((0, 0), (1, 0), (0, 0), (0, 0), (0, 0), (0, 0)))[:, :-1]
        return jnp.concatenate([prev, t], axis=2)

    qb = to_blocks(q)
    kw = band_keys(to_blocks(k))
    vw = band_keys(to_blocks(v))
    s = jnp.einsum('bnqrhe,bnkrhe->bnrhqk', qb, kw)

    qi = jnp.arange(BLOCK)[:, None]
    kj = jnp.arange(2 * BLOCK)[None, :]
    off = qi + BLOCK - kj
    band = (off >= 0) & (off <= SUB_WINDOW)
    blk = jnp.arange(nb)[:, None, None]
    valid = band[None] & (blk * BLOCK + kj[None] - BLOCK >= 0)
    bucket = t5_bucket(jnp.clip(off, 0, SUB_WINDOW) * d)
    bias = jnp.transpose(bias_g.astype(jnp.float32)[bucket], (2, 0, 1))

    s = jnp.where(valid[None, :, None, None], s + bias, NEG_INF)
    lse = jax.nn.logsumexp(s, axis=-1)
    p = jnp.exp(s - lse[..., None])
    o = jnp.einsum('bnrhqk,bnkrhe->bnqrhe', p, vw)
    o = o.reshape(B, Lp, d, H, Dh)[:, :L].reshape(B, S, H, Dh)
    lse = jnp.transpose(lse, (0, 1, 4, 2, 3)).reshape(B, Lp, d, H)[:, :L].reshape(B, S, H)
    return o, lse


def dilated_attention(qkv, q_g, k_g, rel_bias):
    B, S, _ = qkv.shape
    q, k, v = jnp.split(qkv.astype(jnp.float32), 3, axis=-1)
    shp = (B, S, N_GROUPS, HEADS_PER_GROUP, HEAD_DIM)
    q = rms_norm(q.reshape(shp), q_g) * (HEAD_DIM ** -0.5)
    k = rms_norm(k.reshape(shp), k_g)
    v = v.reshape(shp)
    outs, lses = [], []
    for g in range(N_GROUPS):
        o, l = dilated_group(q[:, :, g], k[:, :, g], v[:, :, g],
                             rel_bias[:, g * HEADS_PER_GROUP:(g + 1) * HEADS_PER_GROUP], DILATIONS[g])
        outs.append(o)
        lses.append(l)
    w = jax.nn.softmax(jnp.stack(lses, axis=0), axis=0)
    o = jnp.sum(w[..., None] * jnp.stack(outs, axis=0), axis=0)
    return o.reshape(B, S, ATTN_OUT_DIM)


def setup_inputs(seed: int = 0) -> dict:
    key = jax.random.key(seed)
    ks = jax.random.split(key, 20)
    f32 = jnp.float32

    def nrm(k, shape, scale):
        return jax.random.normal(k, shape, f32) * scale

    res_scale = (2 * DEPTH) ** -0.5
    return {
        "x": nrm(ks[0], (BATCH, SEQ, D_MODEL), 1.0),
        "rel_bias": nrm(ks[1], (NUM_BUCKETS, N_HEADS), 0.5),
        "norm1_g": 1.0 + nrm(ks[2], (DEPTH, D_MODEL), 0.02),
        "w_in": nrm(ks[3], (DEPTH, D_MODEL, IN_COLS), D_MODEL ** -0.5),
        "q_norm_g": 1.0 + nrm(ks[4], (DEPTH, HEAD_DIM), 0.02),
        "k_norm_g": 1.0 + nrm(ks[5], (DEPTH, HEAD_DIM), 0.02),
        "conv_dw_w": nrm(ks[6], (DEPTH, CONV_WIDTH, CONV_DIM), CONV_WIDTH ** -0.5),
        "conv_dw_b": nrm(ks[7], (DEPTH, CONV_DIM), 0.02),
        "conv_ln_g": 1.0 + nrm(ks[8], (DEPTH, CONV_DIM), 0.02),
        "conv_ln_b": nrm(ks[9], (DEPTH, CONV_DIM), 0.02),
        "w_conv_out": nrm(ks[10], (DEPTH, CONV_DIM, D_MODEL), CONV_DIM ** -0.5),
        "w_attn_out": nrm(ks[11], (DEPTH, ATTN_OUT_DIM, D_MODEL), ATTN_OUT_DIM ** -0.5),
        "w_out": nrm(ks[12], (DEPTH, D_MODEL, D_MODEL), D_MODEL ** -0.5 * res_scale),
        "norm2_g": 1.0 + nrm(ks[13], (DEPTH, D_MODEL), 0.02),
        "w_ff1": nrm(ks[14], (DEPTH, D_MODEL, D_FF), D_MODEL ** -0.5),
        "w_ff2": nrm(ks[15], (DEPTH, D_FF, D_MODEL), D_FF ** -0.5 * res_scale),
    }


def reference(x, rel_bias, norm1_g, w_in, q_norm_g, k_norm_g, conv_dw_w, conv_dw_b,
              conv_ln_g, conv_ln_b, w_conv_out, w_attn_out, w_out, norm2_g, w_ff1, w_ff2):
    c_conv = 2 * CONV_DIM
    c_attn = c_conv + 3 * ATTN_DIM
    for l in range(DEPTH):
        h = rms_norm(x, norm1_g[l])
        u = h @ w_in[l]
        y_conv = conformer_conv(u[..., :c_conv], conv_dw_w[l], conv_dw_b[l],
                                conv_ln_g[l], conv_ln_b[l], w_conv_out[l])
        y_attn = dilated_attention(u[..., c_conv:c_attn], q_norm_g[l], k_norm_g[l],
                                   rel_bias).astype(x.dtype) @ w_attn_out[l]
        g_conv, g_attn = jnp.split(jax.nn.sigmoid(u[..., c_attn:]), 2, axis=-1)
        x = x + (g_conv * y_conv + g_attn * y_attn) @ w_out[l]
        h = rms_norm(x, norm2_g[l])
        x = x + jnp.square(jax.nn.relu(h @ w_ff1[l])) @ w_ff2[l]
    return x
```

```python
import functools
import math

import jax
import jax.numpy as jnp
from jax import lax
from jax.experimental import pallas as pl
from jax.experimental.pallas import tpu as pltpu

F32 = jnp.float32
BF16 = jnp.bfloat16

D_MODEL = 1024
CONV_DIM = 512
CONV_WIDTH = 31
N_GROUPS = 3
HEADS_PER_GROUP = 8
HEAD_DIM = 64
GROUP_DIM = HEADS_PER_GROUP * HEAD_DIM
ATTN_DIM = N_GROUPS * GROUP_DIM
DILATIONS = (1, 4, 16)
SUB_WINDOW = 128
BLOCK = 128
NUM_BUCKETS = 32
MAX_REL_DISTANCE = 2048
D_FF = 4 * D_MODEL
EPS = 1e-6
NEG_INF = -1e30
IN_COLS = 2 * CONV_DIM + 3 * ATTN_DIM + 2 * D_MODEL

LANES = 128
COL_TILE = 512
N_COL_TILES = IN_COLS // COL_TILE
N_SLABS = D_MODEL // LANES
CONV_HALO = 32
VMEM_LIMIT = 56 * 1024 * 1024


def _cparams(sem):
    return pltpu.CompilerParams(dimension_semantics=sem, vmem_limit_bytes=VMEM_LIMIT)


def _in_proj_kernel(*refs, seq):
    x_refs = refs[:N_SLABS]
    g_ref, w_ref, u_ref, h_ref = refs[N_SLABS:]
    j = pl.program_id(1)
    rc = 128

    def fill(variant, d):
        sub_len = seq // d

        def body(ci, carry):
            p0 = pl.multiple_of(ci * rc, rc)
            if d == 1:
                xs = [x_refs[c][pl.ds(p0, rc), :] for c in range(N_SLABS)]
            else:
                r = p0 // sub_len
                src = (p0 - r * sub_len) * d + r
                xs = [x_refs[c][pl.ds(src, rc, stride=d), :] for c in range(N_SLABS)]
            ss = xs[0] * xs[0]
            for c in range(1, N_SLABS):
                ss = ss + xs[c] * xs[c]
            rn = lax.rsqrt(jnp.sum(ss, axis=-1, keepdims=True) * (1.0 / D_MODEL) + EPS)
            for c in range(N_SLABS):
                cs = slice(c * LANES, (c + 1) * LANES)
                h_ref[variant, pl.ds(p0, rc), cs] = (xs[c] * rn * g_ref[:, cs]).astype(BF16)
            return carry

        lax.fori_loop(0, seq // rc, body, 0)

    @pl.when(j == 0)
    def _():
        for variant, d in enumerate(DILATIONS):
            fill(variant, d)

    jj = jnp.clip(j - 2, 0, 8)
    sel = jnp.where((j >= 2) & (j <= 10), lax.rem(jj, 3), 0)
    u_ref[...] = jnp.dot(h_ref[sel], w_ref[...], preferred_element_type=F32).astype(BF16)


def _in_proj(x2d, g, w, layer, batch, seq):
    x_specs = [pl.BlockSpec((seq, LANES), functools.partial(lambda b, j, c: (b, c), c=c))
               for c in range(N_SLABS)]
    return pl.pallas_call(
        functools.partial(_in_proj_kernel, seq=seq),
        grid=(batch, N_COL_TILES),
        in_specs=x_specs + [
            pl.BlockSpec((None, 1, D_MODEL), lambda b, j: (layer, 0, 0)),
            pl.BlockSpec((None, D_MODEL, COL_TILE), lambda b, j: (layer, 0, j)),
        ],
        out_specs=pl.BlockSpec((seq, COL_TILE), lambda b, j: (b, j)),
        out_shape=jax.ShapeDtypeStruct((batch * seq, IN_COLS), BF16),
        scratch_shapes=[pltpu.VMEM((N_GROUPS, seq, D_MODEL), BF16)],
        compiler_params=_cparams(("arbitrary", "arbitrary")),
        name="in_proj",
    )(*([x2d] * N_SLABS), g, w)


def _conv_kernel(u_ref, w_ref, b_ref, lg_ref, lb_ref, c_ref, z_ref, *, tile, chunk):
    t = pl.program_id(1)

    @pl.when(t == 0)
    def _():
        z_ref[0:CONV_HALO, :] = jnp.zeros((CONV_HALO, CONV_DIM), F32)

    @pl.when(t != 0)
    def _():
        z_ref[0:CONV_HALO, :] = z_ref[tile:tile + CONV_HALO, :]

    a = u_ref[0, :, 0:CONV_DIM].astype(F32)
    gt = u_ref[0, :, CONV_DIM:2 * CONV_DIM].astype(F32)
    z_ref[CONV_HALO:CONV_HALO + tile, :] = a * jax.nn.sigmoid(gt)

    first_tap = CONV_HALO - (CONV_WIDTH - 1)
    for c0 in range(0, tile, chunk):
        acc = w_ref[0:1, :] * z_ref[c0 + first_tap:c0 + first_tap + chunk, :]
        for k in range(1, CONV_WIDTH):
            acc = acc + w_ref[k:k + 1, :] * z_ref[c0 + first_tap + k:c0 + first_tap + k + chunk, :]
        acc = acc + b_ref[...]
        mu = jnp.mean(acc, axis=-1, keepdims=True)
        xc = acc - mu
        y = xc * lax.rsqrt(jnp.mean(xc * xc, axis=-1, keepdims=True) + EPS)
        y = y * lg_ref[...] + lb_ref[...]
        c_ref[0, c0:c0 + chunk, :] = (y * jax.nn.sigmoid(y)).astype(BF16)


def _conv_branch(u3, dw_w, dw_b, ln_g, ln_b, layer, batch, seq):
    tile, chunk = 256, 64
    vec = lambda: pl.BlockSpec((None, 1, CONV_DIM), lambda b, t: (layer, 0, 0))
    return pl.pallas_call(
        functools.partial(_conv_kernel, tile=tile, chunk=chunk),
        grid=(batch, seq // tile),
        in_specs=[
            pl.BlockSpec((1, tile, 2 * CONV_DIM), lambda b, t: (b, t, 0)),
            pl.BlockSpec((None, CONV_WIDTH, CONV_DIM), lambda b, t: (layer, 0, 0)),
            vec(), vec(), vec(),
        ],
        out_specs=pl.BlockSpec((1, tile, CONV_DIM), lambda b, t: (b, t, 0)),
        out_shape=jax.ShapeDtypeStruct((batch, seq, CONV_DIM), BF16),
        scratch_shapes=[pltpu.VMEM((tile + CONV_HALO, CONV_DIM), F32)],
        compiler_params=_cparams(("arbitrary", "arbitrary")),
        name="conv_branch",
    )(u3, dw_w, dw_b, ln_g, ln_b)


def _attn_kernel(bkt_ref, rb_ref, bsum_ref, qg_ref, kg_ref, q_ref, k_ref, v_ref,
                 o_ref, lse_ref,
                 bm_ref, qe_ref, qo_ref, kn_ref, va_ref, *stage, seq, d):
    sub_len = seq // d
    nb = sub_len // BLOCK
    n_pairs = HEADS_PER_GROUP // 2

    @pl.when(pl.program_id(0) == 0)
    def _():
        bk = bkt_ref[...]
        for h in range(HEADS_PER_GROUP):
            acc = jnp.full((BLOCK, 2 * BLOCK), NEG_INF, F32)
            for b in range(NUM_BUCKETS):
                acc = jnp.where(bk == b, rb_ref[b * HEADS_PER_GROUP + h], acc)
            bm_ref[h] = acc
        va_ref[:, :, LANES:2 * LANES] = jnp.ones((n_pairs, seq, LANES), BF16)

    rc = 256
    even = (lax.broadcasted_iota(jnp.int32, (rc, GROUP_DIM), 1) % LANES) < HEAD_DIM

    def norm_body(ci, carry):
        r0 = pl.multiple_of(ci * rc, rc)
        rows = pl.ds(r0, rc)
        q = q_ref[0, rows, :].astype(F32)
        ssq = jnp.dot((q * q).astype(BF16), bsum_ref[...], preferred_element_type=F32)
        qn = q * lax.rsqrt(ssq * (1.0 / HEAD_DIM) + EPS) * qg_ref[...] * (HEAD_DIM ** -0.5)
        qe_ref[rows, :] = jnp.where(even, qn, 0.0).astype(BF16)
        qo_ref[rows, :] = jnp.where(even, 0.0, qn).astype(BF16)
        k = k_ref[0, rows, :].astype(F32)
        ssk = jnp.dot((k * k).astype(BF16), bsum_ref[...], preferred_element_type=F32)
        kn_ref[rows, :] = (k * lax.rsqrt(ssk * (1.0 / HEAD_DIM) + EPS) * kg_ref[...]).astype(BF16)
        for p in range(n_pairs):
            va_ref[p, rows, 0:LANES] = v_ref[0, rows, p * LANES:(p + 1) * LANES]
        return carry

    lax.fori_loop(0, seq // rc, norm_body, 0)

    lt64 = lax.broadcasted_iota(jnp.int32, (BLOCK, LANES), 1) < HEAD_DIM

    def do_block(row0, t0, first):
        qrows = pl.ds(row0, BLOCK)
        krows = qrows if first else pl.ds(row0 - BLOCK, 2 * BLOCK)
        for p in range(n_pairs):
            cs = slice(p * LANES, (p + 1) * LANES)
            kw = kn_ref[krows, cs]
            va = va_ref[p, krows, :]
            res = []
            for par in range(2):
                h = 2 * p + par
                qh = (qe_ref if par == 0 else qo_ref)[qrows, cs]
                s = lax.dot_general(qh, kw, (((1,), (1,)), ((), ())), preferred_element_type=F32)
                s = s + (bm_ref[h, :, BLOCK:2 * BLOCK] if first else bm_ref[h])
                m = jnp.max(s, axis=-1, keepdims=True)
                pr = jnp.exp(s - m).astype(BF16)
                res.append((jnp.dot(pr, va, preferred_element_type=F32), m))
            (re, me), (ro, mo) = res
            le = re[:, LANES:]
            lo = ro[:, LANES:]
            o_pair = jnp.where(lt64, re[:, :LANES] / le, ro[:, :LANES] / lo)
            lse_pair = jnp.where(lt64, me + jnp.log(le), mo + jnp.log(lo))
            if d == 1:
                o_ref[0, qrows, cs] = o_pair.astype(BF16)
                lse_ref[0, qrows, cs] = lse_pair
            else:
                nat = pl.ds(t0, BLOCK, stride=d)
                stage[0][p, nat, :] = o_pair
                stage[1][p, nat, :] = lse_pair

    def residue_body(r, carry):
        base = r * sub_len
        do_block(base, r, True)
        if nb > 1:
            def blk_body(n, c):
                do_block(pl.multiple_of(base + n * BLOCK, BLOCK), n * (BLOCK * d) + r, False)
                return c
            lax.fori_loop(1, nb, blk_body, 0)
        return carry

    if d == 1:
        residue_body(0, 0)
    else:
        lax.fori_loop(0, d, residue_body, 0)

        def copy_body(ci, carry):
            rows = pl.ds(pl.multiple_of(ci * rc, rc), rc)
            for p in range(n_pairs):
                cs = slice(p * LANES, (p + 1) * LANES)
                o_ref[0, rows, cs] = stage[0][p, rows, :].astype(BF16)
                lse_ref[0, rows, cs] = stage[1][p, rows, :]
            return carry

        lax.fori_loop(0, seq // rc, copy_body, 0)


def _attn_group(u3, bkt, rb, bsum, qg, kg, group, batch, seq):
    d = DILATIONS[group]
    n_pairs = HEADS_PER_GROUP // 2
    const = lambda shape: pl.BlockSpec(shape, lambda b: (0,) * len(shape))
    col = lambda first_tile: pl.BlockSpec((1, seq, GROUP_DIM), lambda b: (b, 0, first_tile + group))
    scratch = [
        pltpu.VMEM((HEADS_PER_GROUP, BLOCK, 2 * BLOCK), F32),
        pltpu.VMEM((seq, GROUP_DIM), BF16),
        pltpu.VMEM((seq, GROUP_DIM), BF16),
        pltpu.VMEM((seq, GROUP_DIM), BF16),
        pltpu.VMEM((n_pairs, seq, 2 * LANES), BF16),
    ]
    if d > 1:
        scratch += [pltpu.VMEM((n_pairs, seq, LANES), F32), pltpu.VMEM((n_pairs, seq, LANES), F32)]
    return pl.pallas_call(
        functools.partial(_attn_kernel, seq=seq, d=d),
        grid=(batch,),
        in_specs=[
            const((BLOCK, 2 * BLOCK)),
            pl.BlockSpec(memory_space=pltpu.SMEM),
            const((GROUP_DIM, GROUP_DIM)),
            const((1, GROUP_DIM)), const((1, GROUP_DIM)),
            col(2), col(5), col(8),
        ],
        out_specs=[pl.BlockSpec((1, seq, GROUP_DIM), lambda b: (b, 0, 0))] * 2,
        out_shape=[jax.ShapeDtypeStruct((batch, seq, GROUP_DIM), BF16),
                   jax.ShapeDtypeStruct((batch, seq, GROUP_DIM), F32)],
        scratch_shapes=scratch,
        compiler_params=_cparams(("arbitrary",)),
        name=f"attn_d{d}",
    )(bkt, rb, bsum, qg, kg, u3, u3, u3)


def _merge_kernel(x_ref, c_ref, o0_ref, o1_ref, o2_ref, l0_ref, l1_ref, l2_ref,
                  gc0_ref, gc1_ref, ga0_ref, ga1_ref, wc_ref, wa_ref, wo_ref, out_ref):
    l0, l1, l2 = l0_ref[...], l1_ref[...], l2_ref[...]
    m = jnp.maximum(jnp.maximum(l0, l1), l2)
    e0, e1, e2 = jnp.exp(l0 - m), jnp.exp(l1 - m), jnp.exp(l2 - m)
    o = (e0 * o0_ref[...].astype(F32) + e1 * o1_ref[...].astype(F32)
         + e2 * o2_ref[...].astype(F32)) / (e0 + e1 + e2)
    y_attn = jnp.dot(o.astype(BF16), wa_ref[...], preferred_element_type=F32)
    y_conv = jnp.dot(c_ref[...], wc_ref[...], preferred_element_type=F32)
    half = D_MODEL // 2
    mix = []
    for hs, gc_ref, ga_ref in ((slice(0, half), gc0_ref, ga0_ref), (slice(half, D_MODEL), gc1_ref, ga1_ref)):
        gc = jax.nn.sigmoid(gc_ref[...].astype(F32))
        ga = jax.nn.sigmoid(ga_ref[...].astype(F32))
        mix.append((gc * y_conv[:, hs] + ga * y_attn[:, hs]).astype(BF16))
    y = (jnp.dot(mix[0], wo_ref[0:half, :], preferred_element_type=F32)
         + jnp.dot(mix[1], wo_ref[half:D_MODEL, :], preferred_element_type=F32))
    out_ref[...] = x_ref[...] + y


def _merge(x2d, c2d, outs, lses, u2d, wc, wa, wo, layer, m_rows):
    tm = 512
    row = lambda width: pl.BlockSpec((tm, width), lambda i: (i, 0))
    gate = lambda tile: pl.BlockSpec((tm, COL_TILE), lambda i: (i, tile))
    wspec = lambda k: pl.BlockSpec((None, k, D_MODEL), lambda i: (layer, 0, 0))
    gate0 = (2 * CONV_DIM + 3 * ATTN_DIM) // COL_TILE
    return pl.pallas_call(
        _merge_kernel,
        grid=(m_rows // tm,),
        in_specs=[row(D_MODEL), row(CONV_DIM)] + [row(GROUP_DIM)] * 6
                 + [gate(gate0), gate(gate0 + 1), gate(gate0 + 2), gate(gate0 + 3)]
                 + [wspec(CONV_DIM), wspec(GROUP_DIM), wspec(D_MODEL)],
        out_specs=row(D_MODEL),
        out_shape=jax.ShapeDtypeStruct((m_rows, D_MODEL), F32),
        compiler_params=_cparams(("arbitrary",)),
        name="merge",
    )(x2d, c2d, *outs, *lses, u2d, u2d, u2d, u2d, wc, wa, wo)


def _ffn_kernel(x_ref, g_ref, w1_ref, w2_ref, out_ref, h_ref):
    j = pl.program_id(1)

    @pl.when(j == 0)
    def _():
        x = x_ref[...]
        rn = lax.rsqrt(jnp.mean(x * x, axis=-1, keepdims=True) + EPS)
        h_ref[...] = (x * rn * g_ref[...]).astype(BF16)

    a = jnp.dot(h_ref[...], w1_ref[...], preferred_element_type=F32)
    a = jnp.square(jnp.maximum(a, 0.0)).astype(BF16)
    y = jnp.dot(a, w2_ref[...], preferred_element_type=F32)

    @pl.when(j == 0)
    def _():
        out_ref[...] = x_ref[...] + y

    @pl.when(j != 0)
    def _():
        out_ref[...] += y


def _ffn(x2d, g, w1, w2, layer, m_rows):
    tm, tf = 1024, 512
    return pl.pallas_call(
        _ffn_kernel,
        grid=(m_rows // tm, D_FF // tf),
        in_specs=[
            pl.BlockSpec((tm, D_MODEL), lambda i, j: (i, 0)),
            pl.BlockSpec((None, 1, D_MODEL), lambda i, j: (layer, 0, 0)),
            pl.BlockSpec((None, D_MODEL, tf), lambda i, j: (layer, 0, j)),
            pl.BlockSpec((None, tf, D_MODEL), lambda i, j: (layer, j, 0)),
        ],
        out_specs=pl.BlockSpec((tm, D_MODEL), lambda i, j: (i, 0)),
        out_shape=jax.ShapeDtypeStruct((m_rows, D_MODEL), F32),
        scratch_shapes=[pltpu.VMEM((tm, D_MODEL), BF16)],
        compiler_params=_cparams(("arbitrary", "arbitrary")),
        name="ffn",
    )(x2d, g, w1, w2)


def _t5_bucket(dist):
    max_exact = NUM_BUCKETS // 2
    nf = jnp.maximum(dist, 1).astype(jnp.float32)
    large = max_exact + (jnp.log(nf / max_exact) / math.log(MAX_REL_DISTANCE / max_exact)
                         * (NUM_BUCKETS - max_exact)).astype(jnp.int32)
    large = jnp.minimum(large, NUM_BUCKETS - 1)
    return jnp.where(dist < max_exact, dist, large)


def _bucket_tile(d):
    qi = jnp.arange(BLOCK)[:, None]
    kj = jnp.arange(2 * BLOCK)[None, :]
    off = qi + BLOCK - kj
    band = (off >= 0) & (off <= SUB_WINDOW)
    bucket = _t5_bucket(jnp.clip(off, 0, SUB_WINDOW) * d)
    return jnp.where(band, bucket, -1).astype(jnp.int32)


def kernel(x, rel_bias, norm1_g, w_in, q_norm_g, k_norm_g, conv_dw_w, conv_dw_b, conv_ln_g, conv_ln_b,
           w_conv_out, w_attn_out, w_out, norm2_g, w_ff1, w_ff2):
    batch, seq, _ = x.shape
    depth = w_in.shape[0]
    m_rows = batch * seq
    assert seq % (BLOCK * max(DILATIONS)) == 0 and x.shape[2] == D_MODEL

    w_in_b, w_conv_b, w_attn_b, w_out_b = (w.astype(BF16) for w in (w_in, w_conv_out, w_attn_out, w_out))
    w_ff1_b, w_ff2_b = w_ff1.astype(BF16), w_ff2.astype(BF16)
    qg_all = jnp.tile(q_norm_g, (1, HEADS_PER_GROUP))
    kg_all = jnp.tile(k_norm_g, (1, HEADS_PER_GROUP))
    head_of_col = jnp.arange(GROUP_DIM) // HEAD_DIM
    bsum = (head_of_col[:, None] == head_of_col[None, :]).astype(BF16)
    bkts = [_bucket_tile(d) for d in DILATIONS]
    rbs = [rel_bias[:, g * HEADS_PER_GROUP:(g + 1) * HEADS_PER_GROUP].reshape(-1).astype(F32)
           for g in range(N_GROUPS)]

    vec3 = lambda a: a.reshape(depth, 1, a.shape[-1])
    norm1_g, norm2_g, conv_dw_b, conv_ln_g, conv_ln_b = (vec3(a) for a in (norm1_g, norm2_g, conv_dw_b, conv_ln_g, conv_ln_b))
    x2d = x.reshape(m_rows, D_MODEL)
    for layer in range(depth):
        u2d = _in_proj(x2d, norm1_g, w_in_b, layer, batch, seq)
        u3 = u2d.reshape(batch, seq, IN_COLS)
        c = _conv_branch(u3, conv_dw_w, conv_dw_b, conv_ln_g, conv_ln_b, layer, batch, seq)
        outs, lses = [], []
        for g in range(N_GROUPS):
            o, lse = _attn_group(u3, bkts[g], rbs[g], bsum, qg_all[layer:layer + 1], kg_all[layer:layer + 1],
                                 g, batch, seq)
            outs.append(o.reshape(m_rows, GROUP_DIM))
            lses.append(lse.reshape(m_rows, GROUP_DIM))
        x2d = _merge(x2d, c.reshape(m_rows, CONV_DIM), outs, lses, u2d, w_conv_b, w_attn_b, w_out_b,
                     layer, m_rows)
        x2d = _ffn(x2d, norm2_g, w_ff1_b, w_ff2_b, layer, m_rows)
    return x2d.reshape(batch, seq, D_MODEL)
```

```python
import functools
import math

import jax
import jax.numpy as jnp
from jax import lax
from jax.experimental import pallas as pl
from jax.experimental.pallas import tpu as pltpu

F32 = jnp.float32
BF16 = jnp.bfloat16

D_MODEL = 1024
CONV_DIM = 512
CONV_WIDTH = 31
N_GROUPS = 3
HEADS_PER_GROUP = 8
HEAD_DIM = 64
GROUP_DIM = HEADS_PER_GROUP * HEAD_DIM
ATTN_DIM = N_GROUPS * GROUP_DIM
DILATIONS = (1, 4, 16)
SUB_WINDOW = 128
BLOCK = 128
NUM_BUCKETS = 32
MAX_REL_DISTANCE = 2048
D_FF = 4 * D_MODEL
EPS = 1e-6
NEG_INF = -1e30
IN_COLS = 2 * CONV_DIM + 3 * ATTN_DIM + 2 * D_MODEL

LANES = 128
SUBLANES = 8
COL_TILE = 512
N_COL_TILES = IN_COLS // COL_TILE
N_SLABS = D_MODEL // LANES
CONV_HALO = 32
VMEM_LIMIT = 56 * 1024 * 1024


def _cparams(sem):
    return pltpu.CompilerParams(dimension_semantics=sem, vmem_limit_bytes=VMEM_LIMIT)


def _in_proj_kernel(*refs, seq):
    x_refs = refs[:N_SLABS]
    g_ref, w_ref, u_ref, h_ref = refs[N_SLABS:]
    j = pl.program_id(1)
    rc = 128

    def fill(variant, d):
        sub_len = seq // d

        def body(ci, carry):
            p0 = pl.multiple_of(ci * rc, rc)
            if d == 1:
                xs = [x_refs[c][pl.ds(p0, rc), :] for c in range(N_SLABS)]
            else:
                r = p0 // sub_len
                src = (p0 - r * sub_len) * d + r
                xs = [x_refs[c][pl.ds(src, rc, stride=d), :] for c in range(N_SLABS)]
            ss = xs[0] * xs[0]
            for c in range(1, N_SLABS):
                ss = ss + xs[c] * xs[c]
            rn = lax.rsqrt(jnp.sum(ss, axis=-1, keepdims=True) * (1.0 / D_MODEL) + EPS)
            for c in range(N_SLABS):
                cs = slice(c * LANES, (c + 1) * LANES)
                h_ref[variant, pl.ds(p0, rc), cs] = (xs[c] * rn * g_ref[:, cs]).astype(BF16)
            return carry

        lax.fori_loop(0, seq // rc, body, 0)

    @pl.when(j == 0)
    def _():
        for variant, d in enumerate(DILATIONS):
            fill(variant, d)

    jj = jnp.clip(j - 2, 0, 8)
    sel = jnp.where((j >= 2) & (j <= 10), lax.rem(jj, 3), 0)
    u_ref[...] = jnp.dot(h_ref[sel], w_ref[...], preferred_element_type=F32).astype(BF16)


def _in_proj(x2d, g, w, layer, batch, seq):
    x_specs = [pl.BlockSpec((seq, LANES), functools.partial(lambda b, j, c: (b, c), c=c))
               for c in range(N_SLABS)]
    return pl.pallas_call(
        functools.partial(_in_proj_kernel, seq=seq),
        grid=(batch, N_COL_TILES),
        in_specs=x_specs + [
            pl.BlockSpec((None, 1, D_MODEL), lambda b, j: (layer, 0, 0)),
            pl.BlockSpec((None, D_MODEL, COL_TILE), lambda b, j: (layer, 0, j)),
        ],
        out_specs=pl.BlockSpec((seq, COL_TILE), lambda b, j: (b, j)),
        out_shape=jax.ShapeDtypeStruct((batch * seq, IN_COLS), BF16),
        scratch_shapes=[pltpu.VMEM((N_GROUPS, seq, D_MODEL), BF16)],
        compiler_params=_cparams(("arbitrary", "arbitrary")),
        name="in_proj",
    )(*([x2d] * N_SLABS), g, w)


def _conv_kernel(u_ref, w_ref, b_ref, lg_ref, lb_ref, c_ref, z_ref, acc_ref, zs_ref, *, tile, chunk, ln_rows):
    t = pl.program_id(1)

    @pl.when(t == 0)
    def _():
        z_ref[0:CONV_HALO, :] = jnp.zeros((CONV_HALO, CONV_DIM), F32)

    @pl.when(t != 0)
    def _():
        z_ref[0:CONV_HALO, :] = z_ref[tile:tile + CONV_HALO, :]

    a = u_ref[0, :, 0:CONV_DIM].astype(F32)
    gt = u_ref[0, :, CONV_DIM:2 * CONV_DIM].astype(F32)
    z_ref[CONV_HALO:CONV_HALO + tile, :] = a * jax.nn.sigmoid(gt)

    first_tap = CONV_HALO - (CONV_WIDTH - 1)
    for c0 in range(0, tile, chunk):
        for lt in range(CONV_DIM // LANES):
            ls = slice(lt * LANES, (lt + 1) * LANES)
            acc = None
            for rho in range(SUBLANES):
                offs = [o for o in range(first_tap, first_tap + CONV_WIDTH) if o % SUBLANES == rho]
                lo, hi = offs[0], offs[-1]
                zs_ref[rho, 0:hi - lo + chunk, :] = z_ref[c0 + lo:c0 + hi + chunk, ls]
                for o in offs:
                    k = o - first_tap
                    term = w_ref[k:k + 1, ls] * zs_ref[rho, o - lo:o - lo + chunk, :]
                    acc = term if acc is None else acc + term
            acc_ref[c0:c0 + chunk, ls] = acc + b_ref[:, ls]

    for c0 in range(0, tile, ln_rows):
        acc = acc_ref[c0:c0 + ln_rows, :]
        mu = jnp.mean(acc, axis=-1, keepdims=True)
        xc = acc - mu
        y = xc * lax.rsqrt(jnp.mean(xc * xc, axis=-1, keepdims=True) + EPS)
        y = y * lg_ref[...] + lb_ref[...]
        c_ref[0, c0:c0 + ln_rows, :] = (y * jax.nn.sigmoid(y)).astype(BF16)


def _conv_branch(u3, dw_w, dw_b, ln_g, ln_b, layer, batch, seq):
    tile, chunk, ln_rows = 256, 128, 64
    vec = lambda: pl.BlockSpec((None, 1, CONV_DIM), lambda b, t: (layer, 0, 0))
    return pl.pallas_call(
        functools.partial(_conv_kernel, tile=tile, chunk=chunk, ln_rows=ln_rows),
        grid=(batch, seq // tile),
        in_specs=[
            pl.BlockSpec((1, tile, 2 * CONV_DIM), lambda b, t: (b, t, 0)),
            pl.BlockSpec((None, CONV_WIDTH, CONV_DIM), lambda b, t: (layer, 0, 0)),
            vec(), vec(), vec(),
        ],
        out_specs=pl.BlockSpec((1, tile, CONV_DIM), lambda b, t: (b, t, 0)),
        out_shape=jax.ShapeDtypeStruct((batch, seq, CONV_DIM), BF16),
        scratch_shapes=[pltpu.VMEM((tile + CONV_HALO, CONV_DIM), F32), pltpu.VMEM((tile, CONV_DIM), F32),
                        pltpu.VMEM((SUBLANES, chunk + CONV_HALO, LANES), F32)],
        compiler_params=_cparams(("arbitrary", "arbitrary")),
        name="conv_branch",
    )(u3, dw_w, dw_b, ln_g, ln_b)


def _attn_kernel(bkt_ref, rb_ref, bsum_ref, qg_ref, kg_ref, q_ref, k_ref, v_ref,
                 o_ref, lse_ref,
                 bm_ref, qe_ref, qo_ref, kn_ref, va_ref, *stage, seq, d):
    sub_len = seq // d
    nb = sub_len // BLOCK
    n_pairs = HEADS_PER_GROUP // 2

    @pl.when(pl.program_id(0) == 0)
    def _():
        bk = bkt_ref[...]
        for h in range(HEADS_PER_GROUP):
            acc = jnp.full((BLOCK, 2 * BLOCK), NEG_INF, F32)
            for b in range(NUM_BUCKETS):
                acc = jnp.where(bk == b, rb_ref[b * HEADS_PER_GROUP + h], acc)
            bm_ref[h // 2, (h % 2) * BLOCK:(h % 2 + 1) * BLOCK, :] = acc
        va_ref[:, :, LANES:2 * LANES] = jnp.ones((n_pairs, seq, LANES), BF16)

    rc = 256
    even = (lax.broadcasted_iota(jnp.int32, (rc, GROUP_DIM), 1) % LANES) < HEAD_DIM

    def norm_body(ci, carry):
        r0 = pl.multiple_of(ci * rc, rc)
        rows = pl.ds(r0, rc)
        q = q_ref[0, rows, :].astype(F32)
        ssq = jnp.dot((q * q).astype(BF16), bsum_ref[...], preferred_element_type=F32)
        qn = q * lax.rsqrt(ssq * (1.0 / HEAD_DIM) + EPS) * qg_ref[...] * (HEAD_DIM ** -0.5)
        qe_ref[rows, :] = jnp.where(even, qn, 0.0).astype(BF16)
        qo_ref[rows, :] = jnp.where(even, 0.0, qn).astype(BF16)
        k = k_ref[0, rows, :].astype(F32)
        ssk = jnp.dot((k * k).astype(BF16), bsum_ref[...], preferred_element_type=F32)
        kn_ref[rows, :] = (k * lax.rsqrt(ssk * (1.0 / HEAD_DIM) + EPS) * kg_ref[...]).astype(BF16)
        for p in range(n_pairs):
            va_ref[p, rows, 0:LANES] = v_ref[0, rows, p * LANES:(p + 1) * LANES]
        return carry

    lax.fori_loop(0, seq // rc, norm_body, 0)

    lt64 = lax.broadcasted_iota(jnp.int32, (BLOCK, LANES), 1) < HEAD_DIM

    def do_block(row0, t0, first):
        qrows = pl.ds(row0, BLOCK)
        krows = qrows if first else pl.ds(row0 - BLOCK, 2 * BLOCK)
        pairs = [slice(p * LANES, (p + 1) * LANES) for p in range(n_pairs)]
        scores = []
        for p, cs in enumerate(pairs):
            q2 = jnp.concatenate([qe_ref[qrows, cs], qo_ref[qrows, cs]], axis=0)
            scores.append(lax.dot_general(q2, kn_ref[krows, cs], (((1,), (1,)), ((), ())),
                                          preferred_element_type=F32))
        probs, maxes = [], []
        for p in range(n_pairs):
            s = scores[p] + (bm_ref[p, :, BLOCK:2 * BLOCK] if first else bm_ref[p])
            m = jnp.max(s, axis=-1, keepdims=True)
            probs.append(jnp.exp(s - m).astype(BF16))
            maxes.append(m)
        results = [jnp.dot(probs[p], va_ref[p, krows, :], preferred_element_type=F32)
                   for p in range(n_pairs)]
        for p, cs in enumerate(pairs):
            re, ro = results[p][:BLOCK], results[p][BLOCK:]
            me, mo = maxes[p][:BLOCK], maxes[p][BLOCK:]
            denom = jnp.where(lt64, re[:, LANES:], ro[:, LANES:])
            o_pair = jnp.where(lt64, re[:, :LANES], ro[:, :LANES]) / denom
            lse_pair = jnp.where(lt64, me, mo) + jnp.log(denom)
            if d == 1:
                o_ref[0, qrows, cs] = o_pair.astype(BF16)
                lse_ref[0, qrows, cs] = lse_pair
            else:
                nat = pl.ds(t0, BLOCK, stride=d)
                stage[0][p, nat, :] = o_pair
                stage[1][p, nat, :] = lse_pair

    def residue_body(r, carry):
        base = r * sub_len
        do_block(base, r, True)
        if nb > 1:
            def blk_body(n, c):
                do_block(pl.multiple_of(base + n * BLOCK, BLOCK), n * (BLOCK * d) + r, False)
                return c
            lax.fori_loop(1, nb, blk_body, 0)
        return carry

    if d == 1:
        residue_body(0, 0)
    else:
        lax.fori_loop(0, d, residue_body, 0)

        def copy_body(ci, carry):
            rows = pl.ds(pl.multiple_of(ci * rc, rc), rc)
            for p in range(n_pairs):
                cs = slice(p * LANES, (p + 1) * LANES)
                o_ref[0, rows, cs] = stage[0][p, rows, :].astype(BF16)
                lse_ref[0, rows, cs] = stage[1][p, rows, :]
            return carry

        lax.fori_loop(0, seq // rc, copy_body, 0)


def _attn_group(u3, bkt, rb, bsum, qg, kg, group, batch, seq):
    d = DILATIONS[group]
    n_pairs = HEADS_PER_GROUP // 2
    const = lambda shape: pl.BlockSpec(shape, lambda b: (0,) * len(shape))
    col = lambda first_tile: pl.BlockSpec((1, seq, GROUP_DIM), lambda b: (b, 0, first_tile + group))
    scratch = [
        pltpu.VMEM((n_pairs, 2 * BLOCK, 2 * BLOCK), F32),
        pltpu.VMEM((seq, GROUP_DIM), BF16),
        pltpu.VMEM((seq, GROUP_DIM), BF16),
        pltpu.VMEM((seq, GROUP_DIM), BF16),
        pltpu.VMEM((n_pairs, seq, 2 * LANES), BF16),
    ]
    if d > 1:
        scratch += [pltpu.VMEM((n_pairs, seq, LANES), F32), pltpu.VMEM((n_pairs, seq, LANES), F32)]
    return pl.pallas_call(
        functools.partial(_attn_kernel, seq=seq, d=d),
        grid=(batch,),
        in_specs=[
            const((BLOCK, 2 * BLOCK)),
            pl.BlockSpec(memory_space=pltpu.SMEM),
            const((GROUP_DIM, GROUP_DIM)),
            const((1, GROUP_DIM)), const((1, GROUP_DIM)),
            col(2), col(5), col(8),
        ],
        out_specs=[pl.BlockSpec((1, seq, GROUP_DIM), lambda b: (b, 0, 0))] * 2,
        out_shape=[jax.ShapeDtypeStruct((batch, seq, GROUP_DIM), BF16),
                   jax.ShapeDtypeStruct((batch, seq, GROUP_DIM), F32)],
        scratch_shapes=scratch,
        compiler_params=_cparams(("arbitrary",)),
        name=f"attn_d{d}",
    )(bkt, rb, bsum, qg, kg, u3, u3, u3)


def _merge_kernel(x_ref, c_ref, o0_ref, o1_ref, o2_ref, l0_ref, l1_ref, l2_ref,
                  gc0_ref, gc1_ref, ga0_ref, ga1_ref, wc_ref, wa_ref, wo_ref, out_ref):
    l0, l1, l2 = l0_ref[...], l1_ref[...], l2_ref[...]
    m = jnp.maximum(jnp.maximum(l0, l1), l2)
    e0, e1, e2 = jnp.exp(l0 - m), jnp.exp(l1 - m), jnp.exp(l2 - m)
    o = (e0 * o0_ref[...].astype(F32) + e1 * o1_ref[...].astype(F32)
         + e2 * o2_ref[...].astype(F32)) / (e0 + e1 + e2)
    y_attn = jnp.dot(o.astype(BF16), wa_ref[...], preferred_element_type=F32)
    y_conv = jnp.dot(c_ref[...], wc_ref[...], preferred_element_type=F32)
    half = D_MODEL // 2
    mix = []
    for hs, gc_ref, ga_ref in ((slice(0, half), gc0_ref, ga0_ref), (slice(half, D_MODEL), gc1_ref, ga1_ref)):
        gc = jax.nn.sigmoid(gc_ref[...].astype(F32))
        ga = jax.nn.sigmoid(ga_ref[...].astype(F32))
        mix.append((gc * y_conv[:, hs] + ga * y_attn[:, hs]).astype(BF16))
    y = (jnp.dot(mix[0], wo_ref[0:half, :], preferred_element_type=F32)
         + jnp.dot(mix[1], wo_ref[half:D_MODEL, :], preferred_element_type=F32))
    out_ref[...] = x_ref[...] + y


def _merge(x2d, c2d, outs, lses, u2d, wc, wa, wo, layer, m_rows):
    tm = 512
    row = lambda width: pl.BlockSpec((tm, width), lambda i: (i, 0))
    gate = lambda tile: pl.BlockSpec((tm, COL_TILE), lambda i: (i, tile))
    wspec = lambda k: pl.BlockSpec((None, k, D_MODEL), lambda i: (layer, 0, 0))
    gate0 = (2 * CONV_DIM + 3 * ATTN_DIM) // COL_TILE
    return pl.pallas_call(
        _merge_kernel,
        grid=(m_rows // tm,),
        in_specs=[row(D_MODEL), row(CONV_DIM)] + [row(GROUP_DIM)] * 6
                 + [gate(gate0), gate(gate0 + 1), gate(gate0 + 2), gate(gate0 + 3)]
                 + [wspec(CONV_DIM), wspec(GROUP_DIM), wspec(D_MODEL)],
        out_specs=row(D_MODEL),
        out_shape=jax.ShapeDtypeStruct((m_rows, D_MODEL), F32),
        compiler_params=_cparams(("arbitrary",)),
        name="merge",
    )(x2d, c2d, *outs, *lses, u2d, u2d, u2d, u2d, wc, wa, wo)


def _ffn_kernel(x_ref, g_ref, w1_ref, w2_ref, out_ref, h_ref):
    j = pl.program_id(1)

    @pl.when(j == 0)
    def _():
        x = x_ref[...]
        rn = lax.rsqrt(jnp.mean(x * x, axis=-1, keepdims=True) + EPS)
        h_ref[...] = (x * rn * g_ref[...]).astype(BF16)

    a = jnp.dot(h_ref[...], w1_ref[...], preferred_element_type=F32)
    a = jnp.square(jnp.maximum(a, 0.0)).astype(BF16)
    y = jnp.dot(a, w2_ref[...], preferred_element_type=F32)

    @pl.when(j == 0)
    def _():
        out_ref[...] = x_ref[...] + y

    @pl.when(j != 0)
    def _():
        out_ref[...] += y


def _ffn(x2d, g, w1, w2, layer, m_rows):
    tm, tf = 1024, 512
    return pl.pallas_call(
        _ffn_kernel,
        grid=(m_rows // tm, D_FF // tf),
        in_specs=[
            pl.BlockSpec((tm, D_MODEL), lambda i, j: (i, 0)),
            pl.BlockSpec((None, 1, D_MODEL), lambda i, j: (layer, 0, 0)),
            pl.BlockSpec((None, D_MODEL, tf), lambda i, j: (layer, 0, j)),
            pl.BlockSpec((None, tf, D_MODEL), lambda i, j: (layer, j, 0)),
        ],
        out_specs=pl.BlockSpec((tm, D_MODEL), lambda i, j: (i, 0)),
        out_shape=jax.ShapeDtypeStruct((m_rows, D_MODEL), F32),
        scratch_shapes=[pltpu.VMEM((tm, D_MODEL), BF16)],
        compiler_params=_cparams(("arbitrary", "arbitrary")),
        name="ffn",
    )(x2d, g, w1, w2)


def _t5_bucket(dist):
    max_exact = NUM_BUCKETS // 2
    nf = jnp.maximum(dist, 1).astype(jnp.float32)
    large = max_exact + (jnp.log(nf / max_exact) / math.log(MAX_REL_DISTANCE / max_exact)
                         * (NUM_BUCKETS - max_exact)).astype(jnp.int32)
    large = jnp.minimum(large, NUM_BUCKETS - 1)
    return jnp.where(dist < max_exact, dist, large)


def _bucket_tile(d):
    qi = jnp.arange(BLOCK)[:, None]
    kj = jnp.arange(2 * BLOCK)[None, :]
    off = qi + BLOCK - kj
    band = (off >= 0) & (off <= SUB_WINDOW)
    bucket = _t5_bucket(jnp.clip(off, 0, SUB_WINDOW) * d)
    return jnp.where(band, bucket, -1).astype(jnp.int32)


def kernel(x, rel_bias, norm1_g, w_in, q_norm_g, k_norm_g, conv_dw_w, conv_dw_b, conv_ln_g, conv_ln_b,
           w_conv_out, w_attn_out, w_out, norm2_g, w_ff1, w_ff2):
    batch, seq, _ = x.shape
    depth = w_in.shape[0]
    m_rows = batch * seq
    assert seq % (BLOCK * max(DILATIONS)) == 0 and x.shape[2] == D_MODEL

    w_in_b, w_conv_b, w_attn_b, w_out_b = (w.astype(BF16) for w in (w_in, w_conv_out, w_attn_out, w_out))
    w_ff1_b, w_ff2_b = w_ff1.astype(BF16), w_ff2.astype(BF16)
    qg_all = jnp.tile(q_norm_g, (1, HEADS_PER_GROUP))
    kg_all = jnp.tile(k_norm_g, (1, HEADS_PER_GROUP))
    head_of_col = jnp.arange(GROUP_DIM) // HEAD_DIM
    bsum = (head_of_col[:, None] == head_of_col[None, :]).astype(BF16)
    bkts = [_bucket_tile(d) for d in DILATIONS]
    rbs = [rel_bias[:, g * HEADS_PER_GROUP:(g + 1) * HEADS_PER_GROUP].reshape(-1).astype(F32)
           for g in range(N_GROUPS)]

    vec3 = lambda a: a.reshape(depth, 1, a.shape[-1])
    norm1_g, norm2_g, conv_dw_b, conv_ln_g, conv_ln_b = (vec3(a) for a in (norm1_g, norm2_g, conv_dw_b, conv_ln_g, conv_ln_b))
    x2d = x.reshape(m_rows, D_MODEL)
    for layer in range(depth):
        u2d = _in_proj(x2d, norm1_g, w_in_b, layer, batch, seq)
        u3 = u2d.reshape(batch, seq, IN_COLS)
        c = _conv_branch(u3, conv_dw_w, conv_dw_b, conv_ln_g, conv_ln_b, layer, batch, seq)
        outs, lses = [], []
        for g in range(N_GROUPS):
            o, lse = _attn_group(u3, bkts[g], rbs[g], bsum, qg_all[layer:layer + 1], kg_all[layer:layer + 1],
                                 g, batch, seq)
            outs.append(o.reshape(m_rows, GROUP_DIM))
            lses.append(lse.reshape(m_rows, GROUP_DIM))
        x2d = _merge(x2d, c.reshape(m_rows, CONV_DIM), outs, lses, u2d, w_conv_b, w_attn_b, w_out_b,
                     layer, m_rows)
        x2d = _ffn(x2d, norm2_g, w_ff1_b, w_ff2_b, layer, m_rows)
    return x2d.reshape(batch, seq, D_MODEL)
```

```python
import functools
import math

import jax
import jax.numpy as jnp
from jax import lax
from jax.experimental import pallas as pl
from jax.experimental.pallas import tpu as pltpu

F32 = jnp.float32
BF16 = jnp.bfloat16

D_MODEL = 1024
CONV_DIM = 512
CONV_WIDTH = 31
N_GROUPS = 3
HEADS_PER_GROUP = 8
HEAD_DIM = 64
GROUP_DIM = HEADS_PER_GROUP * HEAD_DIM
ATTN_DIM = N_GROUPS * GROUP_DIM
DILATIONS = (1, 4, 16)
SUB_WINDOW = 128
BLOCK = 128
NUM_BUCKETS = 32
MAX_REL_DISTANCE = 2048
D_FF = 4 * D_MODEL
EPS = 1e-6
NEG_INF = -1e30
IN_COLS = 2 * CONV_DIM + 3 * ATTN_DIM + 2 * D_MODEL

LANES = 128
SUBLANES = 8
COL_TILE = 512
N_COL_TILES = IN_COLS // COL_TILE
N_SLABS = D_MODEL // LANES
N_PAIRS = HEADS_PER_GROUP // 2
CONV_HALO = 32
VMEM_LIMIT = 56 * 1024 * 1024


def _cparams(sem):
    return pltpu.CompilerParams(dimension_semantics=sem, vmem_limit_bytes=VMEM_LIMIT)


def _slab(c):
    return slice(c * LANES, (c + 1) * LANES)


def _in_proj_kernel(*refs, seq):
    x_refs = refs[:N_SLABS]
    g_ref, w_ref, u_ref, h_ref, rn_ref, tmp_ref = refs[N_SLABS:]
    j = pl.program_id(1)
    rc = 128
    quarter = seq // 4

    @pl.when(j == 0)
    def _():
        def natural(ci, carry):
            rows = pl.ds(pl.multiple_of(ci * rc, rc), rc)
            xs = [x_refs[c][rows, :] for c in range(N_SLABS)]
            ss = xs[0] * xs[0]
            for c in range(1, N_SLABS):
                ss = ss + xs[c] * xs[c]
            rn = lax.rsqrt(jnp.sum(ss, axis=-1, keepdims=True) * (1.0 / D_MODEL) + EPS)
            rn_ref[rows, :] = jnp.broadcast_to(rn, (rc, LANES))
            for c in range(N_SLABS):
                h_ref[0, rows, _slab(c)] = (xs[c] * rn * g_ref[:, _slab(c)]).astype(BF16)
            return carry

        lax.fori_loop(0, seq // rc, natural, 0)

        def by_four(r4, carry):
            base = pl.multiple_of(r4 * quarter, quarter)
            for a0 in range(0, quarter, rc):
                src = pl.ds(a0 * 4 + r4, rc, stride=4)
                rn = rn_ref[src, :]
                for c in range(N_SLABS):
                    y = x_refs[c][src, :] * rn * g_ref[:, _slab(c)]
                    h_ref[1, pl.ds(base + a0, rc), _slab(c)] = y.astype(BF16)
                    tmp_ref[c, a0:a0 + rc, :] = y
            for r2 in range(4):
                for c in range(N_SLABS):
                    h_ref[2, pl.ds(base + r2 * rc, rc), _slab(c)] = (
                        tmp_ref[c, pl.ds(r2, rc, stride=4), :].astype(BF16))
            return carry

        lax.fori_loop(0, 4, by_four, 0)

    jj = jnp.clip(j - 2, 0, 8)
    sel = jnp.where((j >= 2) & (j <= 10), lax.rem(jj, 3), 0)
    u_ref[...] = jnp.dot(h_ref[sel], w_ref[...], preferred_element_type=F32).astype(BF16)


def _in_proj(x2d, g, w, layer, batch, seq):
    assert seq // 16 == 128
    x_specs = [pl.BlockSpec((seq, LANES), functools.partial(lambda b, j, c: (b, c), c=c))
               for c in range(N_SLABS)]
    return pl.pallas_call(
        functools.partial(_in_proj_kernel, seq=seq),
        grid=(batch, N_COL_TILES),
        in_specs=x_specs + [
            pl.BlockSpec((None, 1, D_MODEL), lambda b, j: (layer, 0, 0)),
            pl.BlockSpec((None, D_MODEL, COL_TILE), lambda b, j: (layer, 0, j)),
        ],
        out_specs=pl.BlockSpec((seq, COL_TILE), lambda b, j: (b, j)),
        out_shape=jax.ShapeDtypeStruct((batch * seq, IN_COLS), BF16),
        scratch_shapes=[pltpu.VMEM((N_GROUPS, seq, D_MODEL), BF16),
                        pltpu.VMEM((seq, LANES), F32),
                        pltpu.VMEM((N_SLABS, seq // 4, LANES), F32)],
        compiler_params=_cparams(("arbitrary", "arbitrary")),
        name="in_proj",
    )(*([x2d] * N_SLABS), g, w)


def _conv_kernel(u_ref, w_ref, b_ref, lg_ref, lb_ref, c_ref, z_ref, acc_ref, zs_ref, *, tile, chunk, ln_rows):
    t = pl.program_id(1)

    @pl.when(t == 0)
    def _():
        z_ref[0:CONV_HALO, :] = jnp.zeros((CONV_HALO, CONV_DIM), F32)

    @pl.when(t != 0)
    def _():
        z_ref[0:CONV_HALO, :] = z_ref[tile:tile + CONV_HALO, :]

    a = u_ref[0, :, 0:CONV_DIM].astype(F32)
    gt = u_ref[0, :, CONV_DIM:2 * CONV_DIM].astype(F32)
    z_ref[CONV_HALO:CONV_HALO + tile, :] = a * jax.nn.sigmoid(gt)

    first_tap = CONV_HALO - (CONV_WIDTH - 1)
    for c0 in range(0, tile, chunk):
        for lt in range(CONV_DIM // LANES):
            ls = _slab(lt)
            acc = None
            for rho in range(SUBLANES):
                offs = [o for o in range(first_tap, first_tap + CONV_WIDTH) if o % SUBLANES == rho]
                lo, hi = offs[0], offs[-1]
                zs_ref[rho, 0:hi - lo + chunk, :] = z_ref[c0 + lo:c0 + hi + chunk, ls]
                for o in offs:
                    k = o - first_tap
                    term = w_ref[k:k + 1, ls] * zs_ref[rho, o - lo:o - lo + chunk, :]
                    acc = term if acc is None else acc + term
            acc_ref[c0:c0 + chunk, ls] = acc + b_ref[:, ls]

    for c0 in range(0, tile, ln_rows):
        acc = acc_ref[c0:c0 + ln_rows, :]
        mu = jnp.mean(acc, axis=-1, keepdims=True)
        xc = acc - mu
        y = xc * lax.rsqrt(jnp.mean(xc * xc, axis=-1, keepdims=True) + EPS)
        y = y * lg_ref[...] + lb_ref[...]
        c_ref[0, c0:c0 + ln_rows, :] = (y * jax.nn.sigmoid(y)).astype(BF16)


def _conv_branch(u3, dw_w, dw_b, ln_g, ln_b, layer, batch, seq):
    tile, chunk, ln_rows = 256, 128, 64
    vec = lambda: pl.BlockSpec((None, 1, CONV_DIM), lambda b, t: (layer, 0, 0))
    return pl.pallas_call(
        functools.partial(_conv_kernel, tile=tile, chunk=chunk, ln_rows=ln_rows),
        grid=(batch, seq // tile),
        in_specs=[
            pl.BlockSpec((1, tile, 2 * CONV_DIM), lambda b, t: (b, t, 0)),
            pl.BlockSpec((None, CONV_WIDTH, CONV_DIM), lambda b, t: (layer, 0, 0)),
            vec(), vec(), vec(),
        ],
        out_specs=pl.BlockSpec((1, tile, CONV_DIM), lambda b, t: (b, t, 0)),
        out_shape=jax.ShapeDtypeStruct((batch, seq, CONV_DIM), BF16),
        scratch_shapes=[pltpu.VMEM((tile + CONV_HALO, CONV_DIM), F32), pltpu.VMEM((tile, CONV_DIM), F32),
                        pltpu.VMEM((SUBLANES, chunk + CONV_HALO, LANES), F32)],
        compiler_params=_cparams(("arbitrary", "arbitrary")),
        name="conv_branch",
    )(u3, dw_w, dw_b, ln_g, ln_b)


def _attn_kernel(bkt_ref, rb_ref, bsum_ref, qge_ref, qgo_ref, kg_ref, q_ref, k_ref, v_ref,
                 o_ref, lse_ref,
                 bm_ref, qe_ref, qo_ref, kn_ref, va_ref, s_ref, *stage, seq, d):
    sub_len = seq // d
    nb = sub_len // BLOCK

    @pl.when(pl.program_id(0) == 0)
    def _():
        bk = bkt_ref[...]
        for h in range(HEADS_PER_GROUP):
            acc = jnp.full((BLOCK, 2 * BLOCK), NEG_INF, F32)
            for b in range(NUM_BUCKETS):
                acc = jnp.where(bk == b, rb_ref[b * HEADS_PER_GROUP + h], acc)
            bm_ref[h // 2, (h % 2) * BLOCK:(h % 2 + 1) * BLOCK, :] = acc
        va_ref[:, :, LANES:2 * LANES] = jnp.ones((N_PAIRS, seq, LANES), BF16)

    rc = 256

    def norm_body(ci, carry):
        rows = pl.ds(pl.multiple_of(ci * rc, rc), rc)
        q = q_ref[0, rows, :].astype(F32)
        qr = q * lax.rsqrt(jnp.dot((q * q).astype(BF16), bsum_ref[...], preferred_element_type=F32) + EPS)
        qe_ref[rows, :] = (qr * qge_ref[...]).astype(BF16)
        qo_ref[rows, :] = (qr * qgo_ref[...]).astype(BF16)
        k = k_ref[0, rows, :].astype(F32)
        kr = k * lax.rsqrt(jnp.dot((k * k).astype(BF16), bsum_ref[...], preferred_element_type=F32) + EPS)
        kn_ref[rows, :] = (kr * kg_ref[...]).astype(BF16)
        for p in range(N_PAIRS):
            va_ref[p, rows, 0:LANES] = v_ref[0, rows, _slab(p)]
        return carry

    lax.fori_loop(0, seq // rc, norm_body, 0)

    lt64 = lax.broadcasted_iota(jnp.int32, (BLOCK, LANES), 1) < HEAD_DIM

    def block_aligned(row):
        return row if isinstance(row, int) else pl.multiple_of(row, BLOCK)

    def key_rows(row0, first):
        return pl.ds(row0, BLOCK) if first else pl.ds(block_aligned(row0 - BLOCK), 2 * BLOCK)

    def score_products(row0, first):
        row0 = block_aligned(row0)
        qrows, krows = pl.ds(row0, BLOCK), key_rows(row0, first)
        out = []
        for p in range(N_PAIRS):
            q2 = jnp.concatenate([qe_ref[qrows, _slab(p)], qo_ref[qrows, _slab(p)]], axis=0)
            out.append(lax.dot_general(q2, kn_ref[krows, _slab(p)], (((1,), (1,)), ((), ())),
                                       preferred_element_type=F32))
        return out

    def park(slot, scores, first):
        width = BLOCK if first else 2 * BLOCK
        for p in range(N_PAIRS):
            s_ref[slot, p, :, 0:width] = scores[p]

    def unpark(slot, first):
        width = BLOCK if first else 2 * BLOCK
        return [s_ref[slot, p, :, 0:width] for p in range(N_PAIRS)]

    def finish_block(scores, row0, t0, first):
        row0 = block_aligned(row0)
        qrows, krows = pl.ds(row0, BLOCK), key_rows(row0, first)
        probs, maxes = [], []
        for p in range(N_PAIRS):
            s = scores[p] + (bm_ref[p, :, BLOCK:2 * BLOCK] if first else bm_ref[p])
            m = jnp.max(s, axis=-1, keepdims=True)
            probs.append(jnp.exp(s - m).astype(BF16))
            maxes.append(m)
        results = [jnp.dot(probs[p], va_ref[p, krows, :], preferred_element_type=F32)
                   for p in range(N_PAIRS)]
        for p in range(N_PAIRS):
            re, ro = results[p][:BLOCK], results[p][BLOCK:]
            me, mo = maxes[p][:BLOCK], maxes[p][BLOCK:]
            denom = jnp.where(lt64, re[:, LANES:], ro[:, LANES:])
            o_pair = jnp.where(lt64, re[:, :LANES], ro[:, :LANES]) / denom
            lse_pair = jnp.where(lt64, me, mo) + jnp.log(denom)
            if d == 1:
                o_ref[0, qrows, _slab(p)] = o_pair.astype(BF16)
                lse_ref[0, qrows, _slab(p)] = lse_pair
            else:
                nat = pl.ds(t0, BLOCK, stride=d)
                stage[0][p, nat, :] = o_pair
                stage[1][p, nat, :] = lse_pair

    if nb > 1:
        def class_body(r, carry):
            base = r * sub_len
            park(0, score_products(base, True), True)
            prev = unpark(0, True)
            park(1, score_products(base + BLOCK, False), False)
            finish_block(prev, base, r, True)

            def blk_body(n, c):
                slot = n & 1
                prev = unpark(1 - slot, False)
                row0 = pl.multiple_of(base + n * BLOCK, BLOCK)
                park(slot, score_products(row0, False), False)
                finish_block(prev, row0 - BLOCK, (n - 1) * (BLOCK * d) + r, False)
                return c

            lax.fori_loop(2, nb, blk_body, 0)
            finish_block(unpark((nb - 1) & 1, False), base + (nb - 1) * BLOCK, (nb - 1) * (BLOCK * d) + r, False)
            return carry

        if d == 1:
            class_body(0, 0)
        else:
            lax.fori_loop(0, d, class_body, 0)
    else:
        def residue(c):
            return lax.shift_right_logical(c, 2) + 4 * (c & 3)

        park(0, score_products(0, True), True)

        def cls_body(c, carry):
            slot = c & 1
            prev = unpark(1 - slot, True)
            row0 = pl.multiple_of(c * BLOCK, BLOCK)
            park(slot, score_products(row0, True), True)
            finish_block(prev, row0 - BLOCK, residue(c - 1), True)
            return carry

        lax.fori_loop(1, d, cls_body, 0)
        finish_block(unpark((d - 1) & 1, True), (d - 1) * BLOCK, (d - 1) // 4 + 4 * ((d - 1) % 4), True)

    if d > 1:
        def copy_body(ci, carry):
            rows = pl.ds(pl.multiple_of(ci * rc, rc), rc)
            for p in range(N_PAIRS):
                o_ref[0, rows, _slab(p)] = stage[0][p, rows, :].astype(BF16)
                lse_ref[0, rows, _slab(p)] = stage[1][p, rows, :]
            return carry

        lax.fori_loop(0, seq // rc, copy_body, 0)


def _attn_group(u3, bkt, rb, bsum, qge, qgo, kg, group, batch, seq):
    d = DILATIONS[group]
    assert d in (1, 4, 16) and (seq // d) % BLOCK == 0
    const = lambda shape: pl.BlockSpec(shape, lambda b: (0,) * len(shape))
    col = lambda first_tile: pl.BlockSpec((1, seq, GROUP_DIM), lambda b: (b, 0, first_tile + group))
    scratch = [
        pltpu.VMEM((N_PAIRS, 2 * BLOCK, 2 * BLOCK), F32),
        pltpu.VMEM((seq, GROUP_DIM), BF16),
        pltpu.VMEM((seq, GROUP_DIM), BF16),
        pltpu.VMEM((seq, GROUP_DIM), BF16),
        pltpu.VMEM((N_PAIRS, seq, 2 * LANES), BF16),
        pltpu.VMEM((2, N_PAIRS, 2 * BLOCK, 2 * BLOCK), F32),
    ]
    if d > 1:
        scratch += [pltpu.VMEM((N_PAIRS, seq, LANES), F32), pltpu.VMEM((N_PAIRS, seq, LANES), F32)]
    return pl.pallas_call(
        functools.partial(_attn_kernel, seq=seq, d=d),
        grid=(batch,),
        in_specs=[
            const((BLOCK, 2 * BLOCK)),
            pl.BlockSpec(memory_space=pltpu.SMEM),
            const((GROUP_DIM, GROUP_DIM)),
            const((1, GROUP_DIM)), const((1, GROUP_DIM)), const((1, GROUP_DIM)),
            col(2), col(5), col(8),
        ],
        out_specs=[pl.BlockSpec((1, seq, GROUP_DIM), lambda b: (b, 0, 0))] * 2,
        out_shape=[jax.ShapeDtypeStruct((batch, seq, GROUP_DIM), BF16),
                   jax.ShapeDtypeStruct((batch, seq, GROUP_DIM), F32)],
        scratch_shapes=scratch,
        compiler_params=_cparams(("arbitrary",)),
        name=f"attn_d{d}",
    )(bkt, rb, bsum, qge, qgo, kg, u3, u3, u3)


def _merge_kernel(x_ref, c_ref, o0_ref, o1_ref, o2_ref, l0_ref, l1_ref, l2_ref,
                  gc0_ref, gc1_ref, ga0_ref, ga1_ref, wc_ref, wa_ref, wo_ref, out_ref):
    l0, l1, l2 = l0_ref[...], l1_ref[...], l2_ref[...]
    m = jnp.maximum(jnp.maximum(l0, l1), l2)
    e0, e1, e2 = jnp.exp(l0 - m), jnp.exp(l1 - m), jnp.exp(l2 - m)
    o = (e0 * o0_ref[...].astype(F32) + e1 * o1_ref[...].astype(F32)
         + e2 * o2_ref[...].astype(F32)) / (e0 + e1 + e2)
    y_attn = jnp.dot(o.astype(BF16), wa_ref[...], preferred_element_type=F32)
    y_conv = jnp.dot(c_ref[...], wc_ref[...], preferred_element_type=F32)
    half = D_MODEL // 2
    mix = []
    for hs, gc_ref, ga_ref in ((slice(0, half), gc0_ref, ga0_ref), (slice(half, D_MODEL), gc1_ref, ga1_ref)):
        gc = jax.nn.sigmoid(gc_ref[...].astype(F32))
        ga = jax.nn.sigmoid(ga_ref[...].astype(F32))
        mix.append((gc * y_conv[:, hs] + ga * y_attn[:, hs]).astype(BF16))
    y = (jnp.dot(mix[0], wo_ref[0:half, :], preferred_element_type=F32)
         + jnp.dot(mix[1], wo_ref[half:D_MODEL, :], preferred_element_type=F32))
    out_ref[...] = x_ref[...] + y


def _merge(x2d, c2d, outs, lses, u2d, wc, wa, wo, layer, m_rows):
    tm = 512
    row = lambda width: pl.BlockSpec((tm, width), lambda i: (i, 0))
    gate = lambda tile: pl.BlockSpec((tm, COL_TILE), lambda i: (i, tile))
    wspec = lambda k: pl.BlockSpec((None, k, D_MODEL), lambda i: (layer, 0, 0))
    gate0 = (2 * CONV_DIM + 3 * ATTN_DIM) // COL_TILE
    return pl.pallas_call(
        _merge_kernel,
        grid=(m_rows // tm,),
        in_specs=[row(D_MODEL), row(CONV_DIM)] + [row(GROUP_DIM)] * 6
                 + [gate(gate0), gate(gate0 + 1), gate(gate0 + 2), gate(gate0 + 3)]
                 + [wspec(CONV_DIM), wspec(GROUP_DIM), wspec(D_MODEL)],
        out_specs=row(D_MODEL),
        out_shape=jax.ShapeDtypeStruct((m_rows, D_MODEL), F32),
        compiler_params=_cparams(("arbitrary",)),
        name="merge",
    )(x2d, c2d, *outs, *lses, u2d, u2d, u2d, u2d, wc, wa, wo)


def _ffn_kernel(x_ref, g_ref, w1_ref, w2_ref, out_ref):
    x = x_ref[...]
    rn = lax.rsqrt(jnp.mean(x * x, axis=-1, keepdims=True) + EPS)
    h = (x * rn * g_ref[...]).astype(BF16)
    a = jnp.dot(h, w1_ref[...], preferred_element_type=F32)
    a = jnp.square(jnp.maximum(a, 0.0)).astype(BF16)
    out_ref[...] = x + jnp.dot(a, w2_ref[...], preferred_element_type=F32)


def _ffn(x2d, g, w1, w2, layer, m_rows):
    tm = 512
    resident = pl.Buffered(1)
    return pl.pallas_call(
        _ffn_kernel,
        grid=(m_rows // tm,),
        in_specs=[
            pl.BlockSpec((tm, D_MODEL), lambda i: (i, 0)),
            pl.BlockSpec((None, 1, D_MODEL), lambda i: (layer, 0, 0)),
            pl.BlockSpec((None, D_MODEL, D_FF), lambda i: (layer, 0, 0), pipeline_mode=resident),
            pl.BlockSpec((None, D_FF, D_MODEL), lambda i: (layer, 0, 0), pipeline_mode=resident),
        ],
        out_specs=pl.BlockSpec((tm, D_MODEL), lambda i: (i, 0)),
        out_shape=jax.ShapeDtypeStruct((m_rows, D_MODEL), F32),
        compiler_params=_cparams(("arbitrary",)),
        name="ffn",
    )(x2d, g, w1, w2)


def _t5_bucket(dist):
    max_exact = NUM_BUCKETS // 2
    nf = jnp.maximum(dist, 1).astype(jnp.float32)
    large = max_exact + (jnp.log(nf / max_exact) / math.log(MAX_REL_DISTANCE / max_exact)
                         * (NUM_BUCKETS - max_exact)).astype(jnp.int32)
    large = jnp.minimum(large, NUM_BUCKETS - 1)
    return jnp.where(dist < max_exact, dist, large)


def _bucket_tile(d):
    qi = jnp.arange(BLOCK)[:, None]
    kj = jnp.arange(2 * BLOCK)[None, :]
    off = qi + BLOCK - kj
    band = (off >= 0) & (off <= SUB_WINDOW)
    bucket = _t5_bucket(jnp.clip(off, 0, SUB_WINDOW) * d)
    return jnp.where(band, bucket, -1).astype(jnp.int32)


def kernel(x, rel_bias, norm1_g, w_in, q_norm_g, k_norm_g, conv_dw_w, conv_dw_b, conv_ln_g, conv_ln_b,
           w_conv_out, w_attn_out, w_out, norm2_g, w_ff1, w_ff2):
    batch, seq, _ = x.shape
    depth = w_in.shape[0]
    m_rows = batch * seq
    assert seq % (BLOCK * max(DILATIONS)) == 0 and x.shape[2] == D_MODEL

    w_in_b, w_conv_b, w_attn_b, w_out_b = (w.astype(BF16) for w in (w_in, w_conv_out, w_attn_out, w_out))
    w_ff1_b, w_ff2_b = w_ff1.astype(BF16), w_ff2.astype(BF16)
    head_of_col = jnp.arange(GROUP_DIM) // HEAD_DIM
    even_head = (head_of_col % 2 == 0).astype(F32)[None, :]
    qg_all = jnp.tile(q_norm_g, (1, HEADS_PER_GROUP)) * (HEAD_DIM ** -0.5)
    qge_all, qgo_all = qg_all * even_head, qg_all * (1.0 - even_head)
    kg_all = jnp.tile(k_norm_g, (1, HEADS_PER_GROUP))
    bsum = ((head_of_col[:, None] == head_of_col[None, :]).astype(F32) * (1.0 / HEAD_DIM)).astype(BF16)
    bkts = [_bucket_tile(d) for d in DILATIONS]
    rbs = [rel_bias[:, g * HEADS_PER_GROUP:(g + 1) * HEADS_PER_GROUP].reshape(-1).astype(F32)
           for g in range(N_GROUPS)]

    vec3 = lambda a: a.reshape(depth, 1, a.shape[-1])
    norm1_g, norm2_g, conv_dw_b, conv_ln_g, conv_ln_b = (vec3(a) for a in (norm1_g, norm2_g, conv_dw_b, conv_ln_g, conv_ln_b))
    x2d = x.reshape(m_rows, D_MODEL)
    for layer in range(depth):
        u2d = _in_proj(x2d, norm1_g, w_in_b, layer, batch, seq)
        u3 = u2d.reshape(batch, seq, IN_COLS)
        c = _conv_branch(u3, conv_dw_w, conv_dw_b, conv_ln_g, conv_ln_b, layer, batch, seq)
        outs, lses = [], []
        for g in range(N_GROUPS):
            sl = slice(layer, layer + 1)
            o, lse = _attn_group(u3, bkts[g], rbs[g], bsum, qge_all[sl], qgo_all[sl], kg_all[sl], g, batch, seq)
            outs.append(o.reshape(m_rows, GROUP_DIM))
            lses.append(lse.reshape(m_rows, GROUP_DIM))
        x2d = _merge(x2d, c.reshape(m_rows, CONV_DIM), outs, lses, u2d, w_conv_b, w_attn_b, w_out_b,
                     layer, m_rows)
        x2d = _ffn(x2d, norm2_g, w_ff1_b, w_ff2_b, layer, m_rows)
    return x2d.reshape(batch, seq, D_MODEL)
```

```python
import functools
import math

import jax
import jax.numpy as jnp
from jax import lax
from jax.experimental import pallas as pl
from jax.experimental.pallas import tpu as pltpu

F32 = jnp.float32
BF16 = jnp.bfloat16

D_MODEL = 1024
CONV_DIM = 512
CONV_WIDTH = 31
N_GROUPS = 3
HEADS_PER_GROUP = 8
HEAD_DIM = 64
GROUP_DIM = HEADS_PER_GROUP * HEAD_DIM
ATTN_DIM = N_GROUPS * GROUP_DIM
DILATIONS = (1, 4, 16)
SUB_WINDOW = 128
BLOCK = 128
NUM_BUCKETS = 32
MAX_REL_DISTANCE = 2048
D_FF = 4 * D_MODEL
EPS = 1e-6
NEG_INF = -1e30
IN_COLS = 2 * CONV_DIM + 3 * ATTN_DIM + 2 * D_MODEL

LANES = 128
SUBLANES = 8
COL_TILE = 512
N_COL_TILES = IN_COLS // COL_TILE
N_SLABS = D_MODEL // LANES
N_PAIRS = HEADS_PER_GROUP // 2
CONV_HALO = 32
CONV_TILE = 256
CONV_CHUNK = 64
VMEM_LIMIT = 56 * 1024 * 1024


def _cparams(sem):
    return pltpu.CompilerParams(dimension_semantics=sem, vmem_limit_bytes=VMEM_LIMIT)


def _slab(c):
    return slice(c * LANES, (c + 1) * LANES)


def _in_proj_kernel(*refs, seq):
    x_refs = refs[:N_SLABS]
    (g_ref, w_ref, dw_ref, db_ref, lg_ref, lb_ref, u_ref, c_ref,
     h_ref, rn_ref, tmp_ref, uc_ref, z_ref, acc_ref) = refs[N_SLABS:]
    j = pl.program_id(1)
    rc = 128
    quarter = seq // 4

    @pl.when((pl.program_id(0) == 0) & (j == 0))
    def _():
        uc_ref[...] = jnp.zeros(uc_ref.shape, BF16)
        z_ref[...] = jnp.zeros(z_ref.shape, F32)
        acc_ref[...] = jnp.zeros(acc_ref.shape, F32)

    @pl.when(j == 0)
    def _():
        def natural(ci, carry):
            rows = pl.ds(pl.multiple_of(ci * rc, rc), rc)
            xs = [x_refs[c][rows, :] for c in range(N_SLABS)]
            ss = xs[0] * xs[0]
            for c in range(1, N_SLABS):
                ss = ss + xs[c] * xs[c]
            rn = lax.rsqrt(jnp.sum(ss, axis=-1, keepdims=True) * (1.0 / D_MODEL) + EPS)
            rn_ref[rows, :] = jnp.broadcast_to(rn, (rc, LANES))
            for c in range(N_SLABS):
                h_ref[0, rows, _slab(c)] = (xs[c] * rn * g_ref[:, _slab(c)]).astype(BF16)
            return carry

        lax.fori_loop(0, seq // rc, natural, 0)

        def by_four(r4, carry):
            base = pl.multiple_of(r4 * quarter, quarter)
            for a0 in range(0, quarter, rc):
                src = pl.ds(a0 * 4 + r4, rc, stride=4)
                rn = rn_ref[src, :]
                for c in range(N_SLABS):
                    y = x_refs[c][src, :] * rn * g_ref[:, _slab(c)]
                    h_ref[1, pl.ds(base + a0, rc), _slab(c)] = y.astype(BF16)
                    tmp_ref[c, a0:a0 + rc, :] = y
            for r2 in range(4):
                for c in range(N_SLABS):
                    h_ref[2, pl.ds(base + r2 * rc, rc), _slab(c)] = (
                        tmp_ref[c, pl.ds(r2, rc, stride=4), :].astype(BF16))
            return carry

        lax.fori_loop(0, 4, by_four, 0)

    jj = jnp.clip(j - 2, 0, 8)
    sel = jnp.where((j >= 2) & (j <= 10), lax.rem(jj, 3), 0)

    n_iter = CONV_TILE // CONV_CHUNK
    n_chunks = seq // CONV_CHUNK
    dot_rows = seq // n_iter

    def body(i, carry):
        rows = pl.ds(pl.multiple_of(i * dot_rows, dot_rows), dot_rows)
        u_ref[rows, :] = jnp.dot(h_ref[sel, rows, :], w_ref[...], preferred_element_type=F32).astype(BF16)
        tap_chunk = jnp.clip(j - 3, 0, seq // CONV_TILE - 1) * n_iter + i
        glu_chunk = jnp.clip((j - 3) * n_iter + i + 1, 0, n_chunks - 1)
        chunk_rows = pl.ds(pl.multiple_of(i * CONV_CHUNK, CONV_CHUNK), CONV_CHUNK)
        _conv_norm(acc_ref, (j + 1) & 1, chunk_rows, lg_ref, lb_ref, c_ref)
        _conv_taps(z_ref, pl.multiple_of(tap_chunk * CONV_CHUNK, CONV_CHUNK), dw_ref, db_ref,
                   acc_ref, j & 1, chunk_rows)
        _conv_glu(uc_ref, z_ref, pl.multiple_of(glu_chunk * CONV_CHUNK, CONV_CHUNK))
        return carry

    lax.fori_loop(0, n_iter, body, 0)

    @pl.when(j == 0)
    def _():
        uc_ref[:, 0:COL_TILE] = u_ref[...]

    @pl.when(j == 1)
    def _():
        uc_ref[:, COL_TILE:2 * COL_TILE] = u_ref[...]


def _conv_glu(uc_ref, z_ref, t0):
    a = uc_ref[pl.ds(t0, CONV_CHUNK), 0:CONV_DIM].astype(F32)
    gt = uc_ref[pl.ds(t0, CONV_CHUNK), CONV_DIM:2 * CONV_DIM].astype(F32)
    z_ref[pl.ds(t0 + CONV_HALO, CONV_CHUNK), :] = a * jax.nn.sigmoid(gt)


def _conv_norm(acc_ref, slot, chunk_rows, lg_ref, lb_ref, c_ref):
    acc = acc_ref[slot, chunk_rows, :]
    mu = jnp.mean(acc, axis=-1, keepdims=True)
    xc = acc - mu
    y = xc * lax.rsqrt(jnp.mean(xc * xc, axis=-1, keepdims=True) + EPS)
    y = y * lg_ref[...] + lb_ref[...]
    c_ref[chunk_rows, :] = (y * jax.nn.sigmoid(y)).astype(BF16)


def _conv_taps(z_ref, t0, w_ref, b_ref, acc_ref, slot, chunk_rows):
    chunk = CONV_CHUNK
    first_tap = CONV_HALO - (CONV_WIDTH - 1)
    n_win = chunk + CONV_HALO
    for lt in range(CONV_DIM // LANES):
        ls = _slab(lt)
        window = z_ref[pl.ds(t0, n_win), ls]
        acc = None
        for rho in range(SUBLANES):
            offs = [o for o in range(first_tap, first_tap + CONV_WIDTH) if o % SUBLANES == rho]
            rolled = window if rho == 0 else pltpu.roll(window, n_win - rho, axis=0)
            terms = [w_ref[o - first_tap:o - first_tap + 1, ls] * rolled[o - rho:o - rho + chunk, :]
                     for o in offs]
            while len(terms) > 1:
                terms = [terms[i] + terms[i + 1] for i in range(0, len(terms) - 1, 2)] + (
                    [terms[-1]] if len(terms) % 2 else [])
            acc = terms[0] if acc is None else acc + terms[0]
        acc_ref[slot, chunk_rows, ls] = acc + b_ref[:, ls]


def _in_proj(x2d, g, w, dw_w, dw_b, ln_g, ln_b, layer, batch, seq):
    assert seq // 16 == 128 and COL_TILE == CONV_DIM
    n_conv_tiles = seq // CONV_TILE
    x_specs = [pl.BlockSpec((seq, LANES), functools.partial(lambda b, j, c: (b, c), c=c))
               for c in range(N_SLABS)]
    vec = lambda: pl.BlockSpec((None, 1, CONV_DIM), lambda b, j: (layer, 0, 0))
    return pl.pallas_call(
        functools.partial(_in_proj_kernel, seq=seq),
        grid=(batch, N_COL_TILES),
        in_specs=x_specs + [
            pl.BlockSpec((None, 1, D_MODEL), lambda b, j: (layer, 0, 0)),
            pl.BlockSpec((None, D_MODEL, COL_TILE), lambda b, j: (layer, 0, j)),
            pl.BlockSpec((None, CONV_WIDTH, CONV_DIM), lambda b, j: (layer, 0, 0)),
            vec(), vec(), vec(),
        ],
        out_specs=[
            pl.BlockSpec((seq, COL_TILE), lambda b, j: (b, j)),
            pl.BlockSpec((CONV_TILE, CONV_DIM),
                         lambda b, j: (b * n_conv_tiles + jnp.clip(j - 4, 0, n_conv_tiles - 1), 0)),
        ],
        out_shape=[jax.ShapeDtypeStruct((batch * seq, IN_COLS), BF16),
                   jax.ShapeDtypeStruct((batch * seq, CONV_DIM), BF16)],
        scratch_shapes=[pltpu.VMEM((N_GROUPS, seq, D_MODEL), BF16),
                        pltpu.VMEM((seq, LANES), F32),
                        pltpu.VMEM((N_SLABS, seq // 4, LANES), F32),
                        pltpu.VMEM((seq, 2 * CONV_DIM), BF16),
                        pltpu.VMEM((CONV_HALO + seq, CONV_DIM), F32),
                        pltpu.VMEM((2, CONV_TILE, CONV_DIM), F32)],
        compiler_params=_cparams(("arbitrary", "arbitrary")),
        name="in_proj",
    )(*([x2d] * N_SLABS), g, w, dw_w, dw_b, ln_g, ln_b)


def _attn_kernel(bkt_ref, rb_ref, bsum_ref, qge_ref, qgo_ref, kg_ref, q_ref, k_ref, v_ref,
                 o_ref, lse_ref,
                 bm_ref, qe_ref, qo_ref, kn_ref, va_ref, s_ref, *stage, seq, d):
    sub_len = seq // d
    nb = sub_len // BLOCK

    @pl.when(pl.program_id(0) == 0)
    def _():
        bk = bkt_ref[...]
        for h in range(HEADS_PER_GROUP):
            acc = jnp.full((BLOCK, 2 * BLOCK), NEG_INF, F32)
            for b in range(NUM_BUCKETS):
                acc = jnp.where(bk == b, rb_ref[b * HEADS_PER_GROUP + h], acc)
            bm_ref[h // 2, (h % 2) * BLOCK:(h % 2 + 1) * BLOCK, :] = acc
        va_ref[:, :, LANES:2 * LANES] = jnp.ones((N_PAIRS, seq, LANES), BF16)

    rc = 256

    def head_mean_sq(t):
        sq = (t * t).astype(BF16)
        half = GROUP_DIM // 2
        return jnp.concatenate(
            [jnp.dot(sq[:, 0:half], bsum_ref[...], preferred_element_type=F32),
             jnp.dot(sq[:, half:GROUP_DIM], bsum_ref[...], preferred_element_type=F32)], axis=1)

    def norm_body(ci, carry):
        rows = pl.ds(pl.multiple_of(ci * rc, rc), rc)
        q = q_ref[0, rows, :].astype(F32)
        qr = q * lax.rsqrt(head_mean_sq(q) + EPS)
        qe_ref[rows, :] = (qr * qge_ref[...]).astype(BF16)
        qo_ref[rows, :] = (qr * qgo_ref[...]).astype(BF16)
        k = k_ref[0, rows, :].astype(F32)
        kr = k * lax.rsqrt(head_mean_sq(k) + EPS)
        kn_ref[rows, :] = (kr * kg_ref[...]).astype(BF16)
        for p in range(N_PAIRS):
            va_ref[p, rows, 0:LANES] = v_ref[0, rows, _slab(p)]
        return carry

    lax.fori_loop(0, seq // rc, norm_body, 0)

    lt64 = lax.broadcasted_iota(jnp.int32, (BLOCK, LANES), 1) < HEAD_DIM

    def block_aligned(row):
        return row if isinstance(row, int) else pl.multiple_of(row, BLOCK)

    def key_rows(row0, first):
        return pl.ds(row0, BLOCK) if first else pl.ds(block_aligned(row0 - BLOCK), 2 * BLOCK)

    def score_products(row0, first):
        row0 = block_aligned(row0)
        qrows, krows = pl.ds(row0, BLOCK), key_rows(row0, first)
        out = []
        for p in range(N_PAIRS):
            q2 = jnp.concatenate([qe_ref[qrows, _slab(p)], qo_ref[qrows, _slab(p)]], axis=0)
            out.append(lax.dot_general(q2, kn_ref[krows, _slab(p)], (((1,), (1,)), ((), ())),
                                       preferred_element_type=F32))
        return out

    def park(slot, scores, first):
        width = BLOCK if first else 2 * BLOCK
        for p in range(N_PAIRS):
            s_ref[slot, p, :, 0:width] = scores[p]

    def unpark(slot, first):
        width = BLOCK if first else 2 * BLOCK
        return [s_ref[slot, p, :, 0:width] for p in range(N_PAIRS)]

    def finish_block(scores, row0, t0, first):
        row0 = block_aligned(row0)
        qrows, krows = pl.ds(row0, BLOCK), key_rows(row0, first)
        probs, maxes = [], []
        for p in range(N_PAIRS):
            s = scores[p] + (bm_ref[p, :, BLOCK:2 * BLOCK] if first else bm_ref[p])
            m = jnp.max(s, axis=-1, keepdims=True)
            probs.append(jnp.exp(s - m).astype(BF16))
            maxes.append(m)
        results = [jnp.dot(probs[p], va_ref[p, krows, :], preferred_element_type=F32)
                   for p in range(N_PAIRS)]
        for p in range(N_PAIRS):
            re, ro = results[p][:BLOCK], results[p][BLOCK:]
            me, mo = maxes[p][:BLOCK], maxes[p][BLOCK:]
            denom = jnp.where(lt64, re[:, LANES:], ro[:, LANES:])
            o_pair = jnp.where(lt64, re[:, :LANES], ro[:, :LANES]) / denom
            lse_pair = jnp.where(lt64, me, mo) + jnp.log(denom)
            if d == 1:
                o_ref[0, qrows, _slab(p)] = o_pair.astype(BF16)
                lse_ref[0, qrows, _slab(p)] = lse_pair
            else:
                nat = pl.ds(t0, BLOCK, stride=d)
                stage[0][p, nat, :] = o_pair
                stage[1][p, nat, :] = lse_pair

    if nb > 1:
        def class_body(r, carry):
            base = r * sub_len
            park(0, score_products(base, True), True)
            prev = unpark(0, True)
            park(1, score_products(base + BLOCK, False), False)
            finish_block(prev, base, r, True)

            def blk_body(n, c):
                slot = n & 1
                prev = unpark(1 - slot, False)
                row0 = pl.multiple_of(base + n * BLOCK, BLOCK)
                park(slot, score_products(row0, False), False)
                finish_block(prev, row0 - BLOCK, (n - 1) * (BLOCK * d) + r, False)
                return c

            lax.fori_loop(2, nb, blk_body, 0)
            finish_block(unpark((nb - 1) & 1, False), base + (nb - 1) * BLOCK, (nb - 1) * (BLOCK * d) + r, False)
            return carry

        if d == 1:
            class_body(0, 0)
        else:
            lax.fori_loop(0, d, class_body, 0)
    else:
        def residue(c):
            return lax.shift_right_logical(c, 2) + 4 * (c & 3)

        park(0, score_products(0, True), True)

        def cls_body(c, carry):
            slot = c & 1
            prev = unpark(1 - slot, True)
            row0 = pl.multiple_of(c * BLOCK, BLOCK)
            park(slot, score_products(row0, True), True)
            finish_block(prev, row0 - BLOCK, residue(c - 1), True)
            return carry

        lax.fori_loop(1, d, cls_body, 0)
        finish_block(unpark((d - 1) & 1, True), (d - 1) * BLOCK, (d - 1) // 4 + 4 * ((d - 1) % 4), True)

    if d > 1:
        def copy_body(ci, carry):
            rows = pl.ds(pl.multiple_of(ci * rc, rc), rc)
            for p in range(N_PAIRS):
                o_ref[0, rows, _slab(p)] = stage[0][p, rows, :].astype(BF16)
                lse_ref[0, rows, _slab(p)] = stage[1][p, rows, :]
            return carry

        lax.fori_loop(0, seq // rc, copy_body, 0)


def _attn_group(u3, bkt, rb, bsum, qge, qgo, kg, group, batch, seq):
    d = DILATIONS[group]
    assert d in (1, 4, 16) and (seq // d) % BLOCK == 0
    const = lambda shape: pl.BlockSpec(shape, lambda b: (0,) * len(shape))
    col = lambda first_tile: pl.BlockSpec((1, seq, GROUP_DIM), lambda b: (b, 0, first_tile + group))
    scratch = [
        pltpu.VMEM((N_PAIRS, 2 * BLOCK, 2 * BLOCK), F32),
        pltpu.VMEM((seq, GROUP_DIM), BF16),
        pltpu.VMEM((seq, GROUP_DIM), BF16),
        pltpu.VMEM((seq, GROUP_DIM), BF16),
        pltpu.VMEM((N_PAIRS, seq, 2 * LANES), BF16),
        pltpu.VMEM((2, N_PAIRS, 2 * BLOCK, 2 * BLOCK), F32),
    ]
    if d > 1:
        scratch += [pltpu.VMEM((N_PAIRS, seq, LANES), F32), pltpu.VMEM((N_PAIRS, seq, LANES), F32)]
    return pl.pallas_call(
        functools.partial(_attn_kernel, seq=seq, d=d),
        grid=(batch,),
        in_specs=[
            const((BLOCK, 2 * BLOCK)),
            pl.BlockSpec(memory_space=pltpu.SMEM),
            const((GROUP_DIM // 2, GROUP_DIM // 2)),
            const((1, GROUP_DIM)), const((1, GROUP_DIM)), const((1, GROUP_DIM)),
            col(2), col(5), col(8),
        ],
        out_specs=[pl.BlockSpec((1, seq, GROUP_DIM), lambda b: (b, 0, 0))] * 2,
        out_shape=[jax.ShapeDtypeStruct((batch, seq, GROUP_DIM), BF16),
                   jax.ShapeDtypeStruct((batch, seq, GROUP_DIM), F32)],
        scratch_shapes=scratch,
        compiler_params=_cparams(("arbitrary",)),
        name=f"attn_d{d}",
    )(bkt, rb, bsum, qge, qgo, kg, u3, u3, u3)


def _merge_kernel(x_ref, c_ref, o0_ref, o1_ref, o2_ref, l0_ref, l1_ref, l2_ref,
                  gc0_ref, gc1_ref, ga0_ref, ga1_ref, wc_ref, wa_ref, wo_ref, out_ref):
    l0, l1, l2 = l0_ref[...], l1_ref[...], l2_ref[...]
    m = jnp.maximum(jnp.maximum(l0, l1), l2)
    e0, e1, e2 = jnp.exp(l0 - m), jnp.exp(l1 - m), jnp.exp(l2 - m)
    o = (e0 * o0_ref[...].astype(F32) + e1 * o1_ref[...].astype(F32)
         + e2 * o2_ref[...].astype(F32)) / (e0 + e1 + e2)
    y_attn = jnp.dot(o.astype(BF16), wa_ref[...], preferred_element_type=F32)
    y_conv = jnp.dot(c_ref[...], wc_ref[...], preferred_element_type=F32)
    half = D_MODEL // 2
    mix = []
    for hs, gc_ref, ga_ref in ((slice(0, half), gc0_ref, ga0_ref), (slice(half, D_MODEL), gc1_ref, ga1_ref)):
        gc = jax.nn.sigmoid(gc_ref[...].astype(F32))
        ga = jax.nn.sigmoid(ga_ref[...].astype(F32))
        mix.append((gc * y_conv[:, hs] + ga * y_attn[:, hs]).astype(BF16))
    y = (jnp.dot(mix[0], wo_ref[0:half, :], preferred_element_type=F32)
         + jnp.dot(mix[1], wo_ref[half:D_MODEL, :], preferred_element_type=F32))
    out_ref[...] = x_ref[...] + y


def _merge(x2d, c2d, outs, lses, u2d, wc, wa, wo, layer, m_rows):
    tm = 512
    row = lambda width: pl.BlockSpec((tm, width), lambda i: (i, 0))
    gate = lambda tile: pl.BlockSpec((tm, COL_TILE), lambda i: (i, tile))
    wspec = lambda k: pl.BlockSpec((None, k, D_MODEL), lambda i: (layer, 0, 0))
    gate0 = (2 * CONV_DIM + 3 * ATTN_DIM) // COL_TILE
    return pl.pallas_call(
        _merge_kernel,
        grid=(m_rows // tm,),
        in_specs=[row(D_MODEL), row(CONV_DIM)] + [row(GROUP_DIM)] * 6
                 + [gate(gate0), gate(gate0 + 1), gate(gate0 + 2), gate(gate0 + 3)]
                 + [wspec(CONV_DIM), wspec(GROUP_DIM), wspec(D_MODEL)],
        out_specs=row(D_MODEL),
        out_shape=jax.ShapeDtypeStruct((m_rows, D_MODEL), F32),
        compiler_params=_cparams(("arbitrary",)),
        name="merge",
    )(x2d, c2d, *outs, *lses, u2d, u2d, u2d, u2d, wc, wa, wo)


def _ffn_kernel(x_ref, g_ref, w1_ref, w2_ref, out_ref):
    x = x_ref[...]
    rn = lax.rsqrt(jnp.mean(x * x, axis=-1, keepdims=True) + EPS)
    h = (x * rn * g_ref[...]).astype(BF16)
    a = jnp.dot(h, w1_ref[...], preferred_element_type=F32)
    a = jnp.square(jnp.maximum(a, 0.0)).astype(BF16)
    out_ref[...] = x + jnp.dot(a, w2_ref[...], preferred_element_type=F32)


def _ffn(x2d, g, w1, w2, layer, m_rows):
    tm = 512
    resident = pl.Buffered(1)
    return pl.pallas_call(
        _ffn_kernel,
        grid=(m_rows // tm,),
        in_specs=[
            pl.BlockSpec((tm, D_MODEL), lambda i: (i, 0)),
            pl.BlockSpec((None, 1, D_MODEL), lambda i: (layer, 0, 0)),
            pl.BlockSpec((None, D_MODEL, D_FF), lambda i: (layer, 0, 0), pipeline_mode=resident),
            pl.BlockSpec((None, D_FF, D_MODEL), lambda i: (layer, 0, 0), pipeline_mode=resident),
        ],
        out_specs=pl.BlockSpec((tm, D_MODEL), lambda i: (i, 0)),
        out_shape=jax.ShapeDtypeStruct((m_rows, D_MODEL), F32),
        compiler_params=_cparams(("arbitrary",)),
        name="ffn",
    )(x2d, g, w1, w2)


def _t5_bucket(dist):
    max_exact = NUM_BUCKETS // 2
    nf = jnp.maximum(dist, 1).astype(jnp.float32)
    large = max_exact + (jnp.log(nf / max_exact) / math.log(MAX_REL_DISTANCE / max_exact)
                         * (NUM_BUCKETS - max_exact)).astype(jnp.int32)
    large = jnp.minimum(large, NUM_BUCKETS - 1)
    return jnp.where(dist < max_exact, dist, large)


def _bucket_tile(d):
    qi = jnp.arange(BLOCK)[:, None]
    kj = jnp.arange(2 * BLOCK)[None, :]
    off = qi + BLOCK - kj
    band = (off >= 0) & (off <= SUB_WINDOW)
    bucket = _t5_bucket(jnp.clip(off, 0, SUB_WINDOW) * d)
    return jnp.where(band, bucket, -1).astype(jnp.int32)


def kernel(x, rel_bias, norm1_g, w_in, q_norm_g, k_norm_g, conv_dw_w, conv_dw_b, conv_ln_g, conv_ln_b,
           w_conv_out, w_attn_out, w_out, norm2_g, w_ff1, w_ff2):
    batch, seq, _ = x.shape
    depth = w_in.shape[0]
    m_rows = batch * seq
    assert seq % (BLOCK * max(DILATIONS)) == 0 and x.shape[2] == D_MODEL

    w_in_b, w_conv_b, w_attn_b, w_out_b = (w.astype(BF16) for w in (w_in, w_conv_out, w_attn_out, w_out))
    w_ff1_b, w_ff2_b = w_ff1.astype(BF16), w_ff2.astype(BF16)
    head_of_col = jnp.arange(GROUP_DIM) // HEAD_DIM
    even_head = (head_of_col % 2 == 0).astype(F32)[None, :]
    qg_all = jnp.tile(q_norm_g, (1, HEADS_PER_GROUP)) * (HEAD_DIM ** -0.5)
    qge_all, qgo_all = qg_all * even_head, qg_all * (1.0 - even_head)
    kg_all = jnp.tile(k_norm_g, (1, HEADS_PER_GROUP))
    head_of_half = head_of_col[:GROUP_DIM // 2]
    bsum = ((head_of_half[:, None] == head_of_half[None, :]).astype(F32) * (1.0 / HEAD_DIM)).astype(BF16)
    bkts = [_bucket_tile(d) for d in DILATIONS]
    rbs = [rel_bias[:, g * HEADS_PER_GROUP:(g + 1) * HEADS_PER_GROUP].reshape(-1).astype(F32)
           for g in range(N_GROUPS)]

    vec3 = lambda a: a.reshape(depth, 1, a.shape[-1])
    norm1_g, norm2_g, conv_dw_b, conv_ln_g, conv_ln_b = (vec3(a) for a in (norm1_g, norm2_g, conv_dw_b, conv_ln_g, conv_ln_b))
    x2d = x.reshape(m_rows, D_MODEL)
    for layer in range(depth):
        u2d, c2d = _in_proj(x2d, norm1_g, w_in_b, conv_dw_w, conv_dw_b, conv_ln_g, conv_ln_b, layer, batch, seq)
        u3 = u2d.reshape(batch, seq, IN_COLS)
        outs, lses = [], []
        for g in range(N_GROUPS):
            sl = slice(layer, layer + 1)
            o, lse = _attn_group(u3, bkts[g], rbs[g], bsum, qge_all[sl], qgo_all[sl], kg_all[sl], g, batch, seq)
            outs.append(o.reshape(m_rows, GROUP_DIM))
            lses.append(lse.reshape(m_rows, GROUP_DIM))
        x2d = _merge(x2d, c2d, outs, lses, u2d, w_conv_b, w_attn_b, w_out_b, layer, m_rows)
        x2d = _ffn(x2d, norm2_g, w_ff1_b, w_ff2_b, layer, m_rows)
    return x2d.reshape(batch, seq, D_MODEL)
```

```python
import functools
import math

import jax
import jax.numpy as jnp
from jax import lax
from jax.experimental import pallas as pl
from jax.experimental.pallas import tpu as pltpu

F32 = jnp.float32
BF16 = jnp.bfloat16

D_MODEL = 1024
CONV_DIM = 512
CONV_WIDTH = 31
N_GROUPS = 3
HEADS_PER_GROUP = 8
HEAD_DIM = 64
GROUP_DIM = HEADS_PER_GROUP * HEAD_DIM
ATTN_DIM = N_GROUPS * GROUP_DIM
DILATIONS = (1, 4, 16)
SUB_WINDOW = 128
BLOCK = 128
NUM_BUCKETS = 32
MAX_REL_DISTANCE = 2048
D_FF = 4 * D_MODEL
EPS = 1e-6
NEG_INF = -1e30
IN_COLS = 2 * CONV_DIM + 3 * ATTN_DIM + 2 * D_MODEL

LANES = 128
SUBLANES = 8
COL_TILE = 512
N_COL_TILES = IN_COLS // COL_TILE
N_SLABS = D_MODEL // LANES
N_PAIRS = HEADS_PER_GROUP // 2
CONV_HALO = 32
CONV_TILE = 256
CONV_CHUNK = 64
VMEM_LIMIT = 56 * 1024 * 1024


def _cparams(sem):
    return pltpu.CompilerParams(dimension_semantics=sem, vmem_limit_bytes=VMEM_LIMIT)


def _slab(c):
    return slice(c * LANES, (c + 1) * LANES)


def _in_proj_kernel(*refs, seq):
    x_refs = refs[:N_SLABS]
    (g_ref, w_ref, dw_ref, db_ref, lg_ref, lb_ref, u_ref, c_ref,
     h_ref, rn_ref, tmp_ref, uc_ref, z_ref, acc_ref) = refs[N_SLABS:]
    j = pl.program_id(1)
    rc = 128
    quarter = seq // 4

    @pl.when((pl.program_id(0) == 0) & (j == 0))
    def _():
        uc_ref[...] = jnp.zeros(uc_ref.shape, BF16)
        z_ref[...] = jnp.zeros(z_ref.shape, F32)
        acc_ref[...] = jnp.zeros(acc_ref.shape, F32)

    @pl.when(j == 0)
    def _():
        def natural(ci, carry):
            rows = pl.ds(pl.multiple_of(ci * rc, rc), rc)
            xs = [x_refs[c][rows, :] for c in range(N_SLABS)]
            ss = xs[0] * xs[0]
            for c in range(1, N_SLABS):
                ss = ss + xs[c] * xs[c]
            rn = lax.rsqrt(jnp.sum(ss, axis=-1, keepdims=True) * (1.0 / D_MODEL) + EPS)
            rn_ref[rows, :] = jnp.broadcast_to(rn, (rc, LANES))
            for c in range(N_SLABS):
                h_ref[0, rows, _slab(c)] = (xs[c] * rn * g_ref[:, _slab(c)]).astype(BF16)
            return carry

        lax.fori_loop(0, seq // rc, natural, 0)

        def by_four(r4, carry):
            base = pl.multiple_of(r4 * quarter, quarter)
            for a0 in range(0, quarter, rc):
                src = pl.ds(a0 * 4 + r4, rc, stride=4)
                rn = rn_ref[src, :]
                for c in range(N_SLABS):
                    y = x_refs[c][src, :] * rn * g_ref[:, _slab(c)]
                    h_ref[1, pl.ds(base + a0, rc), _slab(c)] = y.astype(BF16)
                    tmp_ref[c, a0:a0 + rc, :] = y
            for r2 in range(4):
                for c in range(N_SLABS):
                    h_ref[2, pl.ds(base + r2 * rc, rc), _slab(c)] = (
                        tmp_ref[c, pl.ds(r2, rc, stride=4), :].astype(BF16))
            return carry

        lax.fori_loop(0, 4, by_four, 0)

    jj = jnp.clip(j - 2, 0, 8)
    sel = jnp.where((j >= 2) & (j <= 10), lax.rem(jj, 3), 0)

    n_iter = CONV_TILE // CONV_CHUNK
    n_chunks = seq // CONV_CHUNK
    dot_rows = seq // n_iter

    def body(i, carry):
        rows = pl.ds(pl.multiple_of(i * dot_rows, dot_rows), dot_rows)
        u_ref[rows, :] = jnp.dot(h_ref[sel, rows, :], w_ref[...], preferred_element_type=F32).astype(BF16)
        tap_chunk = jnp.clip(j - 3, 0, seq // CONV_TILE - 1) * n_iter + i
        glu_chunk = jnp.clip((j - 3) * n_iter + i + 1, 0, n_chunks - 1)
        chunk_rows = pl.ds(pl.multiple_of(i * CONV_CHUNK, CONV_CHUNK), CONV_CHUNK)
        _conv_norm(acc_ref, (j + 1) & 1, chunk_rows, lg_ref, lb_ref, c_ref)
        _conv_taps(z_ref, pl.multiple_of(tap_chunk * CONV_CHUNK, CONV_CHUNK), dw_ref, db_ref,
                   acc_ref, j & 1, chunk_rows)
        _conv_glu(uc_ref, z_ref, pl.multiple_of(glu_chunk * CONV_CHUNK, CONV_CHUNK))
        return carry

    conv_active = (j >= 2) & (j <= seq // CONV_TILE + 3)

    @pl.when(conv_active)
    def _():
        lax.fori_loop(0, n_iter, body, 0)

    @pl.when(jnp.logical_not(conv_active))
    def _():
        u_ref[...] = jnp.dot(h_ref[sel], w_ref[...], preferred_element_type=F32).astype(BF16)

    @pl.when(j == 0)
    def _():
        uc_ref[:, 0:COL_TILE] = u_ref[...]

    @pl.when(j == 1)
    def _():
        uc_ref[:, COL_TILE:2 * COL_TILE] = u_ref[...]


def _conv_glu(uc_ref, z_ref, t0):
    a = uc_ref[pl.ds(t0, CONV_CHUNK), 0:CONV_DIM].astype(F32)
    gt = uc_ref[pl.ds(t0, CONV_CHUNK), CONV_DIM:2 * CONV_DIM].astype(F32)
    z_ref[pl.ds(t0 + CONV_HALO, CONV_CHUNK), :] = a * jax.nn.sigmoid(gt)


def _conv_norm(acc_ref, slot, chunk_rows, lg_ref, lb_ref, c_ref):
    acc = acc_ref[slot, chunk_rows, :]
    mu = jnp.mean(acc, axis=-1, keepdims=True)
    xc = acc - mu
    y = xc * lax.rsqrt(jnp.mean(xc * xc, axis=-1, keepdims=True) + EPS)
    y = y * lg_ref[...] + lb_ref[...]
    c_ref[chunk_rows, :] = (y * jax.nn.sigmoid(y)).astype(BF16)


def _conv_taps(z_ref, t0, w_ref, b_ref, acc_ref, slot, chunk_rows):
    chunk = CONV_CHUNK
    first_tap = CONV_HALO - (CONV_WIDTH - 1)
    n_win = chunk + CONV_HALO
    for lt in range(CONV_DIM // LANES):
        ls = _slab(lt)
        window = z_ref[pl.ds(t0, n_win), ls]
        acc = None
        for rho in range(SUBLANES):
            offs = [o for o in range(first_tap, first_tap + CONV_WIDTH) if o % SUBLANES == rho]
            rolled = window if rho == 0 else pltpu.roll(window, n_win - rho, axis=0)
            terms = [w_ref[o - first_tap:o - first_tap + 1, ls] * rolled[o - rho:o - rho + chunk, :]
                     for o in offs]
            while len(terms) > 1:
                terms = [terms[i] + terms[i + 1] for i in range(0, len(terms) - 1, 2)] + (
                    [terms[-1]] if len(terms) % 2 else [])
            acc = terms[0] if acc is None else acc + terms[0]
        acc_ref[slot, chunk_rows, ls] = acc + b_ref[:, ls]


def _in_proj(x2d, g, w, dw_w, dw_b, ln_g, ln_b, layer, batch, seq):
    assert seq // 16 == 128 and COL_TILE == CONV_DIM
    n_conv_tiles = seq // CONV_TILE
    x_specs = [pl.BlockSpec((seq, LANES), functools.partial(lambda b, j, c: (b, c), c=c))
               for c in range(N_SLABS)]
    vec = lambda: pl.BlockSpec((None, 1, CONV_DIM), lambda b, j: (layer, 0, 0))
    return pl.pallas_call(
        functools.partial(_in_proj_kernel, seq=seq),
        grid=(batch, N_COL_TILES),
        in_specs=x_specs + [
            pl.BlockSpec((None, 1, D_MODEL), lambda b, j: (layer, 0, 0)),
            pl.BlockSpec((None, D_MODEL, COL_TILE), lambda b, j: (layer, 0, j)),
            pl.BlockSpec((None, CONV_WIDTH, CONV_DIM), lambda b, j: (layer, 0, 0)),
            vec(), vec(), vec(),
        ],
        out_specs=[
            pl.BlockSpec((seq, COL_TILE), lambda b, j: (b, j)),
            pl.BlockSpec((CONV_TILE, CONV_DIM),
                         lambda b, j: (b * n_conv_tiles + jnp.clip(j - 4, 0, n_conv_tiles - 1), 0)),
        ],
        out_shape=[jax.ShapeDtypeStruct((batch * seq, IN_COLS), BF16),
                   jax.ShapeDtypeStruct((batch * seq, CONV_DIM), BF16)],
        scratch_shapes=[pltpu.VMEM((N_GROUPS, seq, D_MODEL), BF16),
                        pltpu.VMEM((seq, LANES), F32),
                        pltpu.VMEM((N_SLABS, seq // 4, LANES), F32),
                        pltpu.VMEM((seq, 2 * CONV_DIM), BF16),
                        pltpu.VMEM((CONV_HALO + seq, CONV_DIM), F32),
                        pltpu.VMEM((2, CONV_TILE, CONV_DIM), F32)],
        compiler_params=_cparams(("arbitrary", "arbitrary")),
        name="in_proj",
    )(*([x2d] * N_SLABS), g, w, dw_w, dw_b, ln_g, ln_b)


def _attn_kernel(bkt_ref, rb_ref, bsum_ref, qge_ref, qgo_ref, kg_ref, q_ref, k_ref, v_ref,
                 o_ref, lse_ref,
                 bm_ref, qe_ref, qo_ref, kn_ref, va_ref, s_ref, *stage, seq, d):
    sub_len = seq // d
    nb = sub_len // BLOCK

    @pl.when(pl.program_id(0) == 0)
    def _():
        bk = bkt_ref[...]
        for h in range(HEADS_PER_GROUP):
            acc = jnp.full((BLOCK, 2 * BLOCK), NEG_INF, F32)
            for b in range(NUM_BUCKETS):
                acc = jnp.where(bk == b, rb_ref[b * HEADS_PER_GROUP + h], acc)
            bm_ref[h // 2, (h % 2) * BLOCK:(h % 2 + 1) * BLOCK, :] = acc
        va_ref[:, :, LANES:2 * LANES] = jnp.ones((N_PAIRS, seq, LANES), BF16)

    rc = 256

    def head_mean_sq(t):
        sq = (t * t).astype(BF16)
        half = GROUP_DIM // 2
        return jnp.concatenate(
            [jnp.dot(sq[:, 0:half], bsum_ref[...], preferred_element_type=F32),
             jnp.dot(sq[:, half:GROUP_DIM], bsum_ref[...], preferred_element_type=F32)], axis=1)

    def norm_body(ci, carry):
        rows = pl.ds(pl.multiple_of(ci * rc, rc), rc)
        q = q_ref[0, rows, :].astype(F32)
        qr = q * lax.rsqrt(head_mean_sq(q) + EPS)
        qe_ref[rows, :] = (qr * qge_ref[...]).astype(BF16)
        qo_ref[rows, :] = (qr * qgo_ref[...]).astype(BF16)
        k = k_ref[0, rows, :].astype(F32)
        kr = k * lax.rsqrt(head_mean_sq(k) + EPS)
        kn_ref[rows, :] = (kr * kg_ref[...]).astype(BF16)
        for p in range(N_PAIRS):
            va_ref[p, rows, 0:LANES] = v_ref[0, rows, _slab(p)]
        return carry

    lax.fori_loop(0, seq // rc, norm_body, 0)

    lt64 = lax.broadcasted_iota(jnp.int32, (BLOCK, LANES), 1) < HEAD_DIM

    def block_aligned(row):
        return row if isinstance(row, int) else pl.multiple_of(row, BLOCK)

    def key_rows(row0, first):
        return pl.ds(row0, BLOCK) if first else pl.ds(block_aligned(row0 - BLOCK), 2 * BLOCK)

    def score_products(row0, first):
        row0 = block_aligned(row0)
        qrows, krows = pl.ds(row0, BLOCK), key_rows(row0, first)
        out = []
        for p in range(N_PAIRS):
            q2 = jnp.concatenate([qe_ref[qrows, _slab(p)], qo_ref[qrows, _slab(p)]], axis=0)
            out.append(lax.dot_general(q2, kn_ref[krows, _slab(p)], (((1,), (1,)), ((), ())),
                                       preferred_element_type=F32))
        return out

    def park(slot, scores, first):
        width = BLOCK if first else 2 * BLOCK
        for p in range(N_PAIRS):
            s_ref[slot, p, :, 0:width] = scores[p]

    def unpark(slot, first):
        width = BLOCK if first else 2 * BLOCK
        return [s_ref[slot, p, :, 0:width] for p in range(N_PAIRS)]

    def finish_block(scores, row0, t0, first):
        row0 = block_aligned(row0)
        qrows, krows = pl.ds(row0, BLOCK), key_rows(row0, first)
        probs, maxes = [], []
        for p in range(N_PAIRS):
            s = scores[p] + (bm_ref[p, :, BLOCK:2 * BLOCK] if first else bm_ref[p])
            m = jnp.max(s, axis=-1, keepdims=True)
            probs.append(jnp.exp(s - m).astype(BF16))
            maxes.append(m)
        results = [jnp.dot(probs[p], va_ref[p, krows, :], preferred_element_type=F32)
                   for p in range(N_PAIRS)]
        for p in range(N_PAIRS):
            re, ro = results[p][:BLOCK], results[p][BLOCK:]
            me, mo = maxes[p][:BLOCK], maxes[p][BLOCK:]
            denom = jnp.where(lt64, re[:, LANES:], ro[:, LANES:])
            o_pair = jnp.where(lt64, re[:, :LANES], ro[:, :LANES]) / denom
            lse_pair = jnp.where(lt64, me, mo) + jnp.log(denom)
            if d == 1:
                o_ref[0, qrows, _slab(p)] = o_pair.astype(BF16)
                lse_ref[0, qrows, _slab(p)] = lse_pair
            else:
                nat = pl.ds(t0, BLOCK, stride=d)
                stage[0][p, nat, :] = o_pair
                stage[1][p, nat, :] = lse_pair

    if nb > 1:
        def class_body(r, carry):
            base = r * sub_len
            park(0, score_products(base, True), True)
            prev = unpark(0, True)
            park(1, score_products(base + BLOCK, False), False)
            finish_block(prev, base, r, True)

            def blk_body(n, c):
                slot = n & 1
                prev = unpark(1 - slot, False)
                row0 = pl.multiple_of(base + n * BLOCK, BLOCK)
                park(slot, score_products(row0, False), False)
                finish_block(prev, row0 - BLOCK, (n - 1) * (BLOCK * d) + r, False)
                return c

            lax.fori_loop(2, nb, blk_body, 0)
            finish_block(unpark((nb - 1) & 1, False), base + (nb - 1) * BLOCK, (nb - 1) * (BLOCK * d) + r, False)
            return carry

        if d == 1:
            class_body(0, 0)
        else:
            lax.fori_loop(0, d, class_body, 0)
    else:
        def residue(c):
            return lax.shift_right_logical(c, 2) + 4 * (c & 3)

        park(0, score_products(0, True), True)

        def cls_body(c, carry):
            slot = c & 1
            prev = unpark(1 - slot, True)
            row0 = pl.multiple_of(c * BLOCK, BLOCK)
            park(slot, score_products(row0, True), True)
            finish_block(prev, row0 - BLOCK, residue(c - 1), True)
            return carry

        lax.fori_loop(1, d, cls_body, 0)
        finish_block(unpark((d - 1) & 1, True), (d - 1) * BLOCK, (d - 1) // 4 + 4 * ((d - 1) % 4), True)

    if d > 1:
        def copy_body(ci, carry):
            rows = pl.ds(pl.multiple_of(ci * rc, rc), rc)
            for p in range(N_PAIRS):
                o_ref[0, rows, _slab(p)] = stage[0][p, rows, :].astype(BF16)
                lse_ref[0, rows, _slab(p)] = stage[1][p, rows, :]
            return carry

        lax.fori_loop(0, seq // rc, copy_body, 0)


def _attn_group(u3, bkt, rb, bsum, qge, qgo, kg, group, batch, seq):
    d = DILATIONS[group]
    assert d in (1, 4, 16) and (seq // d) % BLOCK == 0
    const = lambda shape: pl.BlockSpec(shape, lambda b: (0,) * len(shape))
    col = lambda first_tile: pl.BlockSpec((1, seq, GROUP_DIM), lambda b: (b, 0, first_tile + group))
    scratch = [
        pltpu.VMEM((N_PAIRS, 2 * BLOCK, 2 * BLOCK), F32),
        pltpu.VMEM((seq, GROUP_DIM), BF16),
        pltpu.VMEM((seq, GROUP_DIM), BF16),
        pltpu.VMEM((seq, GROUP_DIM), BF16),
        pltpu.VMEM((N_PAIRS, seq, 2 * LANES), BF16),
        pltpu.VMEM((2, N_PAIRS, 2 * BLOCK, 2 * BLOCK), F32),
    ]
    if d > 1:
        scratch += [pltpu.VMEM((N_PAIRS, seq, LANES), F32), pltpu.VMEM((N_PAIRS, seq, LANES), F32)]
    return pl.pallas_call(
        functools.partial(_attn_kernel, seq=seq, d=d),
        grid=(batch,),
        in_specs=[
            const((BLOCK, 2 * BLOCK)),
            pl.BlockSpec(memory_space=pltpu.SMEM),
            const((GROUP_DIM // 2, GROUP_DIM // 2)),
            const((1, GROUP_DIM)), const((1, GROUP_DIM)), const((1, GROUP_DIM)),
            col(2), col(5), col(8),
        ],
        out_specs=[pl.BlockSpec((1, seq, GROUP_DIM), lambda b: (b, 0, 0))] * 2,
        out_shape=[jax.ShapeDtypeStruct((batch, seq, GROUP_DIM), BF16),
                   jax.ShapeDtypeStruct((batch, seq, GROUP_DIM), F32)],
        scratch_shapes=scratch,
        compiler_params=_cparams(("arbitrary",)),
        name=f"attn_d{d}",
    )(bkt, rb, bsum, qge, qgo, kg, u3, u3, u3)


def _merge_kernel(x_ref, c_ref, o0_ref, o1_ref, o2_ref, l0_ref, l1_ref, l2_ref,
                  gc0_ref, gc1_ref, ga0_ref, ga1_ref, wc_ref, wa_ref, wo_ref, out_ref):
    l0, l1, l2 = l0_ref[...], l1_ref[...], l2_ref[...]
    m = jnp.maximum(jnp.maximum(l0, l1), l2)
    e0, e1, e2 = jnp.exp(l0 - m), jnp.exp(l1 - m), jnp.exp(l2 - m)
    o = (e0 * o0_ref[...].astype(F32) + e1 * o1_ref[...].astype(F32)
         + e2 * o2_ref[...].astype(F32)) / (e0 + e1 + e2)
    y_attn = jnp.dot(o.astype(BF16), wa_ref[...], preferred_element_type=F32)
    y_conv = jnp.dot(c_ref[...], wc_ref[...], preferred_element_type=F32)
    half = D_MODEL // 2
    mix = []
    for hs, gc_ref, ga_ref in ((slice(0, half), gc0_ref, ga0_ref), (slice(half, D_MODEL), gc1_ref, ga1_ref)):
        gc = jax.nn.sigmoid(gc_ref[...].astype(F32))
        ga = jax.nn.sigmoid(ga_ref[...].astype(F32))
        mix.append((gc * y_conv[:, hs] + ga * y_attn[:, hs]).astype(BF16))
    y = (jnp.dot(mix[0], wo_ref[0:half, :], preferred_element_type=F32)
         + jnp.dot(mix[1], wo_ref[half:D_MODEL, :], preferred_element_type=F32))
    out_ref[...] = x_ref[...] + y


def _merge(x2d, c2d, outs, lses, u2d, wc, wa, wo, layer, m_rows):
    tm = 512
    row = lambda width: pl.BlockSpec((tm, width), lambda i: (i, 0))
    gate = lambda tile: pl.BlockSpec((tm, COL_TILE), lambda i: (i, tile))
    wspec = lambda k: pl.BlockSpec((None, k, D_MODEL), lambda i: (layer, 0, 0))
    gate0 = (2 * CONV_DIM + 3 * ATTN_DIM) // COL_TILE
    return pl.pallas_call(
        _merge_kernel,
        grid=(m_rows // tm,),
        in_specs=[row(D_MODEL), row(CONV_DIM)] + [row(GROUP_DIM)] * 6
                 + [gate(gate0), gate(gate0 + 1), gate(gate0 + 2), gate(gate0 + 3)]
                 + [wspec(CONV_DIM), wspec(GROUP_DIM), wspec(D_MODEL)],
        out_specs=row(D_MODEL),
        out_shape=jax.ShapeDtypeStruct((m_rows, D_MODEL), F32),
        compiler_params=_cparams(("arbitrary",)),
        name="merge",
    )(x2d, c2d, *outs, *lses, u2d, u2d, u2d, u2d, wc, wa, wo)


def _ffn_kernel(x_ref, g_ref, w1_ref, w2_ref, out_ref):
    x = x_ref[...]
    rn = lax.rsqrt(jnp.mean(x * x, axis=-1, keepdims=True) + EPS)
    h = (x * rn * g_ref[...]).astype(BF16)
    a = jnp.dot(h, w1_ref[...], preferred_element_type=F32)
    a = jnp.square(jnp.maximum(a, 0.0)).astype(BF16)
    out_ref[...] = x + jnp.dot(a, w2_ref[...], preferred_element_type=F32)


def _ffn(x2d, g, w1, w2, layer, m_rows):
    tm = 512
    resident = pl.Buffered(1)
    return pl.pallas_call(
        _ffn_kernel,
        grid=(m_rows // tm,),
        in_specs=[
            pl.BlockSpec((tm, D_MODEL), lambda i: (i, 0)),
            pl.BlockSpec((None, 1, D_MODEL), lambda i: (layer, 0, 0)),
            pl.BlockSpec((None, D_MODEL, D_FF), lambda i: (layer, 0, 0), pipeline_mode=resident),
            pl.BlockSpec((None, D_FF, D_MODEL), lambda i: (layer, 0, 0), pipeline_mode=resident),
        ],
        out_specs=pl.BlockSpec((tm, D_MODEL), lambda i: (i, 0)),
        out_shape=jax.ShapeDtypeStruct((m_rows, D_MODEL), F32),
        compiler_params=_cparams(("arbitrary",)),
        name="ffn",
    )(x2d, g, w1, w2)


def _t5_bucket(dist):
    max_exact = NUM_BUCKETS // 2
    nf = jnp.maximum(dist, 1).astype(jnp.float32)
    large = max_exact + (jnp.log(nf / max_exact) / math.log(MAX_REL_DISTANCE / max_exact)
                         * (NUM_BUCKETS - max_exact)).astype(jnp.int32)
    large = jnp.minimum(large, NUM_BUCKETS - 1)
    return jnp.where(dist < max_exact, dist, large)


def _bucket_tile(d):
    qi = jnp.arange(BLOCK)[:, None]
    kj = jnp.arange(2 * BLOCK)[None, :]
    off = qi + BLOCK - kj
    band = (off >= 0) & (off <= SUB_WINDOW)
    bucket = _t5_bucket(jnp.clip(off, 0, SUB_WINDOW) * d)
    return jnp.where(band, bucket, -1).astype(jnp.int32)


def kernel(x, rel_bias, norm1_g, w_in, q_norm_g, k_norm_g, conv_dw_w, conv_dw_b, conv_ln_g, conv_ln_b,
           w_conv_out, w_attn_out, w_out, norm2_g, w_ff1, w_ff2):
    batch, seq, _ = x.shape
    depth = w_in.shape[0]
    m_rows = batch * seq
    assert seq % (BLOCK * max(DILATIONS)) == 0 and x.shape[2] == D_MODEL

    w_in_b, w_conv_b, w_attn_b, w_out_b = (w.astype(BF16) for w in (w_in, w_conv_out, w_attn_out, w_out))
    w_ff1_b, w_ff2_b = w_ff1.astype(BF16), w_ff2.astype(BF16)
    head_of_col = jnp.arange(GROUP_DIM) // HEAD_DIM
    even_head = (head_of_col % 2 == 0).astype(F32)[None, :]
    qg_all = jnp.tile(q_norm_g, (1, HEADS_PER_GROUP)) * (HEAD_DIM ** -0.5)
    qge_all, qgo_all = qg_all * even_head, qg_all * (1.0 - even_head)
    kg_all = jnp.tile(k_norm_g, (1, HEADS_PER_GROUP))
    head_of_half = head_of_col[:GROUP_DIM // 2]
    bsum = ((head_of_half[:, None] == head_of_half[None, :]).astype(F32) * (1.0 / HEAD_DIM)).astype(BF16)
    bkts = [_bucket_tile(d) for d in DILATIONS]
    rbs = [rel_bias[:, g * HEADS_PER_GROUP:(g + 1) * HEADS_PER_GROUP].reshape(-1).astype(F32)
           for g in range(N_GROUPS)]

    vec3 = lambda a: a.reshape(depth, 1, a.shape[-1])
    norm1_g, norm2_g, conv_dw_b, conv_ln_g, conv_ln_b = (vec3(a) for a in (norm1_g, norm2_g, conv_dw_b, conv_ln_g, conv_ln_b))
    x2d = x.reshape(m_rows, D_MODEL)
    for layer in range(depth):
        u2d, c2d = _in_proj(x2d, norm1_g, w_in_b, conv_dw_w, conv_dw_b, conv_ln_g, conv_ln_b, layer, batch, seq)
        u3 = u2d.reshape(batch, seq, IN_COLS)
        outs, lses = [], []
        for g in range(N_GROUPS):
            sl = slice(layer, layer + 1)
            o, lse = _attn_group(u3, bkts[g], rbs[g], bsum, qge_all[sl], qgo_all[sl], kg_all[sl], g, batch, seq)
            outs.append(o.reshape(m_rows, GROUP_DIM))
            lses.append(lse.reshape(m_rows, GROUP_DIM))
        x2d = _merge(x2d, c2d, outs, lses, u2d, w_conv_b, w_attn_b, w_out_b, layer, m_rows)
        x2d = _ffn(x2d, norm2_g, w_ff1_b, w_ff2_b, layer, m_rows)
    return x2d.reshape(batch, seq, D_MODEL)
```

```python
import functools
import math

import jax
import jax.numpy as jnp
from jax import lax
from jax.experimental import pallas as pl
from jax.experimental.pallas import tpu as pltpu

F32 = jnp.float32
BF16 = jnp.bfloat16

D_MODEL = 1024
CONV_DIM = 512
CONV_WIDTH = 31
N_GROUPS = 3
HEADS_PER_GROUP = 8
HEAD_DIM = 64
GROUP_DIM = HEADS_PER_GROUP * HEAD_DIM
ATTN_DIM = N_GROUPS * GROUP_DIM
DILATIONS = (1, 4, 16)
SUB_WINDOW = 128
BLOCK = 128
NUM_BUCKETS = 32
MAX_REL_DISTANCE = 2048
D_FF = 4 * D_MODEL
EPS = 1e-6
NEG_INF = -1e30
IN_COLS = 2 * CONV_DIM + 3 * ATTN_DIM + 2 * D_MODEL

LANES = 128
SUBLANES = 8
COL_TILE = 512
N_COL_TILES = IN_COLS // COL_TILE
N_CONV_COL_TILES = 2 * CONV_DIM // COL_TILE
U_COLS = IN_COLS - 2 * CONV_DIM
N_SLABS = D_MODEL // LANES
N_PAIRS = HEADS_PER_GROUP // 2
MIX_ROWS = 512
FF_CHUNK = 2048
CONV_HALO = 32
CONV_TILE = 256
CONV_CHUNK = 64
VMEM_LIMIT = 56 * 1024 * 1024


def _cparams(sem):
    return pltpu.CompilerParams(dimension_semantics=sem, vmem_limit_bytes=VMEM_LIMIT)


def _slab(c):
    return slice(c * LANES, (c + 1) * LANES)


def _in_proj_kernel(*refs, seq):
    x_refs = refs[:N_SLABS]
    (g_ref, w_ref, dw_ref, db_ref, lg_ref, lb_ref, u_ref, c_ref,
     h_ref, rn_ref, tmp_ref, uc_ref, z_ref, acc_ref) = refs[N_SLABS:]
    j = pl.program_id(1)
    rc = 128
    quarter = seq // 4

    @pl.when((pl.program_id(0) == 0) & (j == 0))
    def _():
        uc_ref[...] = jnp.zeros(uc_ref.shape, BF16)
        z_ref[...] = jnp.zeros(z_ref.shape, F32)
        acc_ref[...] = jnp.zeros(acc_ref.shape, F32)

    @pl.when(j == 0)
    def _():
        def natural(ci, carry):
            rows = pl.ds(pl.multiple_of(ci * rc, rc), rc)
            xs = [x_refs[c][rows, :] for c in range(N_SLABS)]
            ss = xs[0] * xs[0]
            for c in range(1, N_SLABS):
                ss = ss + xs[c] * xs[c]
            rn = lax.rsqrt(jnp.sum(ss, axis=-1, keepdims=True) * (1.0 / D_MODEL) + EPS)
            rn_ref[rows, :] = jnp.broadcast_to(rn, (rc, LANES))
            for c in range(N_SLABS):
                h_ref[0, rows, _slab(c)] = (xs[c] * rn * g_ref[:, _slab(c)]).astype(BF16)
            return carry

        lax.fori_loop(0, seq // rc, natural, 0)

        def by_four(r4, carry):
            base = pl.multiple_of(r4 * quarter, quarter)
            for a0 in range(0, quarter, rc):
                src = pl.ds(a0 * 4 + r4, rc, stride=4)
                rn = rn_ref[src, :]
                for c in range(N_SLABS):
                    y = x_refs[c][src, :] * rn * g_ref[:, _slab(c)]
                    h_ref[1, pl.ds(base + a0, rc), _slab(c)] = y.astype(BF16)
                    tmp_ref[c, a0:a0 + rc, :] = y
            for r2 in range(4):
                for c in range(N_SLABS):
                    h_ref[2, pl.ds(base + r2 * rc, rc), _slab(c)] = (
                        tmp_ref[c, pl.ds(r2, rc, stride=4), :].astype(BF16))
            return carry

        lax.fori_loop(0, 4, by_four, 0)

    jj = jnp.clip(j - 2, 0, 8)
    sel = jnp.where((j >= 2) & (j <= 10), lax.rem(jj, 3), 0)

    n_iter = CONV_TILE // CONV_CHUNK
    n_chunks = seq // CONV_CHUNK
    dot_rows = seq // n_iter

    def body(i, carry):
        rows = pl.ds(pl.multiple_of(i * dot_rows, dot_rows), dot_rows)
        u_ref[rows, :] = jnp.dot(h_ref[sel, rows, :], w_ref[...], preferred_element_type=F32).astype(BF16)
        tap_chunk = jnp.clip(j - 3, 0, seq // CONV_TILE - 1) * n_iter + i
        glu_chunk = jnp.clip((j - 3) * n_iter + i + 1, 0, n_chunks - 1)
        chunk_rows = pl.ds(pl.multiple_of(i * CONV_CHUNK, CONV_CHUNK), CONV_CHUNK)
        _conv_norm(acc_ref, (j + 1) & 1, chunk_rows, lg_ref, lb_ref, c_ref)
        _conv_taps(z_ref, pl.multiple_of(tap_chunk * CONV_CHUNK, CONV_CHUNK), dw_ref, db_ref,
                   acc_ref, j & 1, chunk_rows)
        _conv_glu(uc_ref, z_ref, pl.multiple_of(glu_chunk * CONV_CHUNK, CONV_CHUNK))
        return carry

    conv_active = (j >= 2) & (j <= seq // CONV_TILE + 3)

    @pl.when(conv_active)
    def _():
        lax.fori_loop(0, n_iter, body, 0)

    @pl.when(jnp.logical_not(conv_active))
    def _():
        u_ref[...] = jnp.dot(h_ref[sel], w_ref[...], preferred_element_type=F32).astype(BF16)

    @pl.when(j == 0)
    def _():
        uc_ref[:, 0:COL_TILE] = u_ref[...]

    @pl.when(j == 1)
    def _():
        uc_ref[:, COL_TILE:2 * COL_TILE] = u_ref[...]


def _conv_glu(uc_ref, z_ref, t0):
    a = uc_ref[pl.ds(t0, CONV_CHUNK), 0:CONV_DIM].astype(F32)
    gt = uc_ref[pl.ds(t0, CONV_CHUNK), CONV_DIM:2 * CONV_DIM].astype(F32)
    z_ref[pl.ds(t0 + CONV_HALO, CONV_CHUNK), :] = a * jax.nn.sigmoid(gt)


def _conv_norm(acc_ref, slot, chunk_rows, lg_ref, lb_ref, c_ref):
    acc = acc_ref[slot, chunk_rows, :]
    mu = jnp.mean(acc, axis=-1, keepdims=True)
    xc = acc - mu
    y = xc * lax.rsqrt(jnp.mean(xc * xc, axis=-1, keepdims=True) + EPS)
    y = y * lg_ref[...] + lb_ref[...]
    c_ref[chunk_rows, :] = (y * jax.nn.sigmoid(y)).astype(BF16)


def _conv_taps(z_ref, t0, w_ref, b_ref, acc_ref, slot, chunk_rows):
    chunk = CONV_CHUNK
    first_tap = CONV_HALO - (CONV_WIDTH - 1)
    n_win = chunk + CONV_HALO
    for lt in range(CONV_DIM // LANES):
        ls = _slab(lt)
        window = z_ref[pl.ds(t0, n_win), ls]
        acc = None
        for rho in range(SUBLANES):
            offs = [o for o in range(first_tap, first_tap + CONV_WIDTH) if o % SUBLANES == rho]
            rolled = window if rho == 0 else pltpu.roll(window, n_win - rho, axis=0)
            terms = [w_ref[o - first_tap:o - first_tap + 1, ls] * rolled[o - rho:o - rho + chunk, :]
                     for o in offs]
            while len(terms) > 1:
                terms = [terms[i] + terms[i + 1] for i in range(0, len(terms) - 1, 2)] + (
                    [terms[-1]] if len(terms) % 2 else [])
            acc = terms[0] if acc is None else acc + terms[0]
        acc_ref[slot, chunk_rows, ls] = acc + b_ref[:, ls]


def _in_proj(x2d, g, w, dw_w, dw_b, ln_g, ln_b, layer, batch, seq):
    assert seq // 16 == 128 and COL_TILE == CONV_DIM
    n_conv_tiles = seq // CONV_TILE
    x_specs = [pl.BlockSpec((seq, LANES), functools.partial(lambda b, j, c: (b, c), c=c))
               for c in range(N_SLABS)]
    vec = lambda: pl.BlockSpec((None, 1, CONV_DIM), lambda b, j: (layer, 0, 0))
    return pl.pallas_call(
        functools.partial(_in_proj_kernel, seq=seq),
        grid=(batch, N_COL_TILES),
        in_specs=x_specs + [
            pl.BlockSpec((None, 1, D_MODEL), lambda b, j: (layer, 0, 0)),
            pl.BlockSpec((None, D_MODEL, COL_TILE), lambda b, j: (layer, 0, j)),
            pl.BlockSpec((None, CONV_WIDTH, CONV_DIM), lambda b, j: (layer, 0, 0)),
            vec(), vec(), vec(),
        ],
        out_specs=[
            pl.BlockSpec((seq, COL_TILE), lambda b, j: (b, jnp.maximum(j - N_CONV_COL_TILES, 0))),
            pl.BlockSpec((CONV_TILE, CONV_DIM),
                         lambda b, j: (b * n_conv_tiles + jnp.clip(j - 4, 0, n_conv_tiles - 1), 0)),
        ],
        out_shape=[jax.ShapeDtypeStruct((batch * seq, U_COLS), BF16),
                   jax.ShapeDtypeStruct((batch * seq, CONV_DIM), BF16)],
        scratch_shapes=[pltpu.VMEM((N_GROUPS, seq, D_MODEL), BF16),
                        pltpu.VMEM((seq, LANES), F32),
                        pltpu.VMEM((N_SLABS, seq // 4, LANES), F32),
                        pltpu.VMEM((seq, 2 * CONV_DIM), BF16),
                        pltpu.VMEM((CONV_HALO + seq, CONV_DIM), F32),
                        pltpu.VMEM((2, CONV_TILE, CONV_DIM), F32)],
        compiler_params=_cparams(("arbitrary", "arbitrary")),
        name="in_proj",
    )(*([x2d] * N_SLABS), g, w, dw_w, dw_b, ln_g, ln_b)


def _attn_kernel(bkt_ref, rb_ref, bsum_ref, qge_ref, qgo_ref, kg_ref, q_ref, k_ref, v_ref,
                 o_ref, lse_ref,
                 bm_ref, qe_ref, qo_ref, kn_ref, va_ref, s_ref, *stage, seq, d):
    sub_len = seq // d
    nb = sub_len // BLOCK

    @pl.when(pl.program_id(0) == 0)
    def _():
        bk = bkt_ref[...]
        for h in range(HEADS_PER_GROUP):
            acc = jnp.full((BLOCK, 2 * BLOCK), NEG_INF, F32)
            for b in range(NUM_BUCKETS):
                acc = jnp.where(bk == b, rb_ref[b * HEADS_PER_GROUP + h], acc)
            bm_ref[h // 2, (h % 2) * BLOCK:(h % 2 + 1) * BLOCK, :] = acc
        va_ref[:, :, LANES:2 * LANES] = jnp.ones((N_PAIRS, seq, LANES), BF16)

    rc = 256

    def head_mean_sq(t):
        sq = (t * t).astype(BF16)
        half = GROUP_DIM // 2
        return jnp.concatenate(
            [jnp.dot(sq[:, 0:half], bsum_ref[...], preferred_element_type=F32),
             jnp.dot(sq[:, half:GROUP_DIM], bsum_ref[...], preferred_element_type=F32)], axis=1)

    def norm_body(ci, carry):
        rows = pl.ds(pl.multiple_of(ci * rc, rc), rc)
        q = q_ref[0, rows, :].astype(F32)
        qr = q * lax.rsqrt(head_mean_sq(q) + EPS)
        qe_ref[rows, :] = (qr * qge_ref[...]).astype(BF16)
        qo_ref[rows, :] = (qr * qgo_ref[...]).astype(BF16)
        k = k_ref[0, rows, :].astype(F32)
        kr = k * lax.rsqrt(head_mean_sq(k) + EPS)
        kn_ref[rows, :] = (kr * kg_ref[...]).astype(BF16)
        for p in range(N_PAIRS):
            va_ref[p, rows, 0:LANES] = v_ref[0, rows, _slab(p)]
        return carry

    lax.fori_loop(0, seq // rc, norm_body, 0)

    lt64 = lax.broadcasted_iota(jnp.int32, (BLOCK, LANES), 1) < HEAD_DIM

    def block_aligned(row):
        return row if isinstance(row, int) else pl.multiple_of(row, BLOCK)

    def key_rows(row0, first):
        return pl.ds(row0, BLOCK) if first else pl.ds(block_aligned(row0 - BLOCK), 2 * BLOCK)

    def score_products(row0, first):
        row0 = block_aligned(row0)
        qrows, krows = pl.ds(row0, BLOCK), key_rows(row0, first)
        out = []
        for p in range(N_PAIRS):
            q2 = jnp.concatenate([qe_ref[qrows, _slab(p)], qo_ref[qrows, _slab(p)]], axis=0)
            out.append(lax.dot_general(q2, kn_ref[krows, _slab(p)], (((1,), (1,)), ((), ())),
                                       preferred_element_type=F32))
        return out

    def park(slot, chain, scores, first):
        width = BLOCK if first else 2 * BLOCK
        for p in range(N_PAIRS):
            s_ref[slot, chain, p, :, 0:width] = scores[p]

    def unpark(slot, chain, first):
        width = BLOCK if first else 2 * BLOCK
        return [s_ref[slot, chain, p, :, 0:width] for p in range(N_PAIRS)]

    def finish_blocks(blocks):
        soft = []
        for scores, row0, t0, first in blocks:
            probs, maxes = [], []
            for p in range(N_PAIRS):
                s = scores[p] + (bm_ref[p, :, BLOCK:2 * BLOCK] if first else bm_ref[p])
                m = jnp.max(s, axis=-1, keepdims=True)
                probs.append(jnp.exp(s - m).astype(BF16))
                maxes.append(m)
            soft.append((probs, maxes))
        results = []
        for (probs, _), (_, row0, _, first) in zip(soft, blocks):
            krows = key_rows(block_aligned(row0), first)
            results.append([jnp.dot(probs[p], va_ref[p, krows, :], preferred_element_type=F32)
                            for p in range(N_PAIRS)])
        for res, (_, maxes), (_, row0, t0, _) in zip(results, soft, blocks):
            qrows = pl.ds(block_aligned(row0), BLOCK)
            for p in range(N_PAIRS):
                re, ro = res[p][:BLOCK], res[p][BLOCK:]
                me, mo = maxes[p][:BLOCK], maxes[p][BLOCK:]
                denom = jnp.where(lt64, re[:, LANES:], ro[:, LANES:])
                o_pair = jnp.where(lt64, re[:, :LANES], ro[:, :LANES]) / denom
                lse_pair = jnp.where(lt64, me, mo) + jnp.log(denom)
                if d == 1:
                    o_ref[0, qrows, _slab(p)] = o_pair.astype(BF16)
                    lse_ref[0, qrows, _slab(p)] = lse_pair
                else:
                    nat = pl.ds(t0, BLOCK, stride=d)
                    stage[0][p, nat, :] = o_pair
                    stage[1][p, nat, :] = lse_pair

    def run_chains(chains, n_blocks):
        def is_first(mode, n):
            return mode == "all" or (mode == "head" and n == 0)

        def row(base, n):
            return base + n * BLOCK

        def issue(slot, n, static_n):
            for ci, (base, mode, _) in enumerate(chains):
                first = is_first(mode, static_n)
                park(slot, ci, score_products(row(base, n), first), first)

        def finish(slot, n, static_n):
            finish_blocks([(unpark(slot, ci, is_first(mode, static_n)), row(base, n), t0_of(n), is_first(mode, static_n))
                           for ci, (base, mode, t0_of) in enumerate(chains)])

        issue(0, 0, 0)
        issue(1, 1, 1)
        finish(0, 0, 0)

        def blk_body(n, c):
            slot = n & 1
            prev = [(unpark(1 - slot, ci, is_first(mode, 2)), row(base, n - 1), t0_of(n - 1), is_first(mode, 2))
                    for ci, (base, mode, t0_of) in enumerate(chains)]
            issue(slot, n, 2)
            finish_blocks(prev)
            return c

        lax.fori_loop(2, n_blocks, blk_body, 0)
        finish((n_blocks - 1) & 1, n_blocks - 1, 2)

    def residue_of_class(c):
        if isinstance(c, int):
            return c // 4 + 4 * (c % 4)
        return lax.shift_right_logical(c, 2) + 4 * (c & 3)

    if d == 1:
        run_chains([(0, "head", lambda n: n * BLOCK)], nb)
    elif nb > 1:
        def class_body(r, carry):
            run_chains([(pl.multiple_of(r * sub_len, BLOCK), "head", lambda n: n * (BLOCK * d) + r)], nb)
            return carry

        lax.fori_loop(0, d, class_body, 0)
    else:
        run_chains([(0, "all", residue_of_class)], d)

    if d > 1:
        def copy_body(ci, carry):
            rows = pl.ds(pl.multiple_of(ci * rc, rc), rc)
            for p in range(N_PAIRS):
                o_ref[0, rows, _slab(p)] = stage[0][p, rows, :].astype(BF16)
                lse_ref[0, rows, _slab(p)] = stage[1][p, rows, :]
            return carry

        lax.fori_loop(0, seq // rc, copy_body, 0)


def _attn_group(u3, bkt, rb, bsum, qge, qgo, kg, group, batch, seq):
    d = DILATIONS[group]
    assert d in (1, 4, 16) and (seq // d) % BLOCK == 0
    const = lambda shape: pl.BlockSpec(shape, lambda b: (0,) * len(shape))
    col = lambda first_tile: pl.BlockSpec((1, seq, GROUP_DIM), lambda b: (b, 0, first_tile + group))
    scratch = [
        pltpu.VMEM((N_PAIRS, 2 * BLOCK, 2 * BLOCK), F32),
        pltpu.VMEM((seq, GROUP_DIM), BF16),
        pltpu.VMEM((seq, GROUP_DIM), BF16),
        pltpu.VMEM((seq, GROUP_DIM), BF16),
        pltpu.VMEM((N_PAIRS, seq, 2 * LANES), BF16),
        pltpu.VMEM((2, 1, N_PAIRS, 2 * BLOCK, 2 * BLOCK), F32),
    ]
    if d > 1:
        scratch += [pltpu.VMEM((N_PAIRS, seq, LANES), F32), pltpu.VMEM((N_PAIRS, seq, LANES), F32)]
    return pl.pallas_call(
        functools.partial(_attn_kernel, seq=seq, d=d),
        grid=(batch,),
        in_specs=[
            const((BLOCK, 2 * BLOCK)),
            pl.BlockSpec(memory_space=pltpu.SMEM),
            const((GROUP_DIM // 2, GROUP_DIM // 2)),
            const((1, GROUP_DIM)), const((1, GROUP_DIM)), const((1, GROUP_DIM)),
            col(0), col(N_GROUPS), col(2 * N_GROUPS),
        ],
        out_specs=[pl.BlockSpec((1, seq, GROUP_DIM), lambda b: (b, 0, 0))] * 2,
        out_shape=[jax.ShapeDtypeStruct((batch, seq, GROUP_DIM), BF16),
                   jax.ShapeDtypeStruct((batch, seq, GROUP_DIM), F32)],
        scratch_shapes=scratch,
        compiler_params=_cparams(("arbitrary",)),
        name=f"attn_d{d}",
    )(bkt, rb, bsum, qge, qgo, kg, u3, u3, u3)


def _merge_ffn_kernel(x_ref, c_ref, o0_ref, o1_ref, o2_ref, l0_ref, l1_ref, l2_ref,
                      gc0_ref, gc1_ref, ga0_ref, ga1_ref, wc_ref, wa_ref, wo_ref,
                      g2_ref, w1_ref, w2_ref, out_ref):
    l0, l1, l2 = l0_ref[...], l1_ref[...], l2_ref[...]
    m = jnp.maximum(jnp.maximum(l0, l1), l2)
    e0, e1, e2 = jnp.exp(l0 - m), jnp.exp(l1 - m), jnp.exp(l2 - m)
    o = (e0 * o0_ref[...].astype(F32) + e1 * o1_ref[...].astype(F32)
         + e2 * o2_ref[...].astype(F32)) / (e0 + e1 + e2)
    y_attn = jnp.dot(o.astype(BF16), wa_ref[...], preferred_element_type=F32)
    y_conv = jnp.dot(c_ref[...], wc_ref[...], preferred_element_type=F32)
    half = D_MODEL // 2
    mix = []
    for hs, gc_ref, ga_ref in ((slice(0, half), gc0_ref, ga0_ref), (slice(half, D_MODEL), gc1_ref, ga1_ref)):
        gc = jax.nn.sigmoid(gc_ref[...].astype(F32))
        ga = jax.nn.sigmoid(ga_ref[...].astype(F32))
        mix.append((gc * y_conv[:, hs] + ga * y_attn[:, hs]).astype(BF16))
    y = (jnp.dot(mix[0], wo_ref[0:half, :], preferred_element_type=F32)
         + jnp.dot(mix[1], wo_ref[half:D_MODEL, :], preferred_element_type=F32))
    x = x_ref[...] + y

    rn = lax.rsqrt(jnp.mean(x * x, axis=-1, keepdims=True) + EPS)
    h = (x * rn * g2_ref[...]).astype(BF16)
    y2 = None
    for f0 in range(0, D_FF, FF_CHUNK):
        a = jnp.dot(h, w1_ref[:, f0:f0 + FF_CHUNK], preferred_element_type=F32)
        a = jnp.square(jnp.maximum(a, 0.0)).astype(BF16)
        t = jnp.dot(a, w2_ref[f0:f0 + FF_CHUNK, :], preferred_element_type=F32)
        y2 = t if y2 is None else y2 + t
    out_ref[...] = x + y2


def _merge_ffn(x2d, c2d, outs, lses, u2d, wc, wa, wo, g2, w1, w2, layer, m_rows):
    tm = MIX_ROWS
    resident = pl.Buffered(1)
    row = lambda width: pl.BlockSpec((tm, width), lambda i: (i, 0))
    gate = lambda tile: pl.BlockSpec((tm, COL_TILE), lambda i: (i, tile))
    wspec = lambda k, n: pl.BlockSpec((None, k, n), lambda i: (layer, 0, 0), pipeline_mode=resident)
    gate0 = 3 * ATTN_DIM // COL_TILE
    return pl.pallas_call(
        _merge_ffn_kernel,
        grid=(m_rows // tm,),
        in_specs=[row(D_MODEL), row(CONV_DIM)] + [row(GROUP_DIM)] * 6
                 + [gate(gate0), gate(gate0 + 1), gate(gate0 + 2), gate(gate0 + 3)]
                 + [wspec(CONV_DIM, D_MODEL), wspec(GROUP_DIM, D_MODEL), wspec(D_MODEL, D_MODEL),
                    pl.BlockSpec((None, 1, D_MODEL), lambda i: (layer, 0, 0)),
                    wspec(D_MODEL, D_FF), wspec(D_FF, D_MODEL)],
        out_specs=row(D_MODEL),
        out_shape=jax.ShapeDtypeStruct((m_rows, D_MODEL), F32),
        compiler_params=_cparams(("arbitrary",)),
        name="merge_ffn",
    )(x2d, c2d, *outs, *lses, u2d, u2d, u2d, u2d, wc, wa, wo, g2, w1, w2)


def _t5_bucket(dist):
    max_exact = NUM_BUCKETS // 2
    nf = jnp.maximum(dist, 1).astype(jnp.float32)
    large = max_exact + (jnp.log(nf / max_exact) / math.log(MAX_REL_DISTANCE / max_exact)
                         * (NUM_BUCKETS - max_exact)).astype(jnp.int32)
    large = jnp.minimum(large, NUM_BUCKETS - 1)
    return jnp.where(dist < max_exact, dist, large)


def _bucket_tile(d):
    qi = jnp.arange(BLOCK)[:, None]
    kj = jnp.arange(2 * BLOCK)[None, :]
    off = qi + BLOCK - kj
    band = (off >= 0) & (off <= SUB_WINDOW)
    bucket = _t5_bucket(jnp.clip(off, 0, SUB_WINDOW) * d)
    return jnp.where(band, bucket, -1).astype(jnp.int32)


def kernel(x, rel_bias, norm1_g, w_in, q_norm_g, k_norm_g, conv_dw_w, conv_dw_b, conv_ln_g, conv_ln_b,
           w_conv_out, w_attn_out, w_out, norm2_g, w_ff1, w_ff2):
    batch, seq, _ = x.shape
    depth = w_in.shape[0]
    m_rows = batch * seq
    assert seq % (BLOCK * max(DILATIONS)) == 0 and x.shape[2] == D_MODEL

    w_in_b, w_conv_b, w_attn_b, w_out_b = (w.astype(BF16) for w in (w_in, w_conv_out, w_attn_out, w_out))
    w_ff1_b, w_ff2_b = w_ff1.astype(BF16), w_ff2.astype(BF16)
    head_of_col = jnp.arange(GROUP_DIM) // HEAD_DIM
    even_head = (head_of_col % 2 == 0).astype(F32)[None, :]
    qg_all = jnp.tile(q_norm_g, (1, HEADS_PER_GROUP)) * (HEAD_DIM ** -0.5)
    qge_all, qgo_all = qg_all * even_head, qg_all * (1.0 - even_head)
    kg_all = jnp.tile(k_norm_g, (1, HEADS_PER_GROUP))
    head_of_half = head_of_col[:GROUP_DIM // 2]
    bsum = ((head_of_half[:, None] == head_of_half[None, :]).astype(F32) * (1.0 / HEAD_DIM)).astype(BF16)
    bkts = [_bucket_tile(d) for d in DILATIONS]
    rbs = [rel_bias[:, g * HEADS_PER_GROUP:(g + 1) * HEADS_PER_GROUP].reshape(-1).astype(F32)
           for g in range(N_GROUPS)]

    vec3 = lambda a: a.reshape(depth, 1, a.shape[-1])
    norm1_g, norm2_g, conv_dw_b, conv_ln_g, conv_ln_b = (vec3(a) for a in (norm1_g, norm2_g, conv_dw_b, conv_ln_g, conv_ln_b))
    x2d = x.reshape(m_rows, D_MODEL)
    for layer in range(depth):
        u2d, c2d = _in_proj(x2d, norm1_g, w_in_b, conv_dw_w, conv_dw_b, conv_ln_g, conv_ln_b, layer, batch, seq)
        u3 = u2d.reshape(batch, seq, U_COLS)
        outs, lses = [], []
        for g in range(N_GROUPS):
            sl = slice(layer, layer + 1)
            o, lse = _attn_group(u3, bkts[g], rbs[g], bsum, qge_all[sl], qgo_all[sl], kg_all[sl], g, batch, seq)
            outs.append(o.reshape(m_rows, GROUP_DIM))
            lses.append(lse.reshape(m_rows, GROUP_DIM))
        x2d = _merge_ffn(x2d, c2d, outs, lses, u2d, w_conv_b, w_attn_b, w_out_b,
                         norm2_g, w_ff1_b, w_ff2_b, layer, m_rows)
    return x2d.reshape(batch, seq, D_MODEL)
```

```python
import functools
import math

import jax
import jax.numpy as jnp
from jax import lax
from jax.experimental import pallas as pl
from jax.experimental.pallas import tpu as pltpu

F32 = jnp.float32
BF16 = jnp.bfloat16

D_MODEL = 1024
CONV_DIM = 512
CONV_WIDTH = 31
N_GROUPS = 3
HEADS_PER_GROUP = 8
HEAD_DIM = 64
GROUP_DIM = HEADS_PER_GROUP * HEAD_DIM
ATTN_DIM = N_GROUPS * GROUP_DIM
DILATIONS = (1, 4, 16)
SUB_WINDOW = 128
BLOCK = 128
NUM_BUCKETS = 32
MAX_REL_DISTANCE = 2048
D_FF = 4 * D_MODEL
EPS = 1e-6
NEG_INF = -1e30
IN_COLS = 2 * CONV_DIM + 3 * ATTN_DIM + 2 * D_MODEL

LANES = 128
SUBLANES = 8
COL_TILE = 512
N_COL_TILES = IN_COLS // COL_TILE
N_CONV_COL_TILES = 2 * CONV_DIM // COL_TILE
U_COLS = IN_COLS - 2 * CONV_DIM
N_SLABS = D_MODEL // LANES
N_PAIRS = HEADS_PER_GROUP // 2
MIX_ROWS = 512
FF_CHUNK = 2048
CONV_HALO = 32
CONV_TILE = 256
CONV_CHUNK = 128
VMEM_LIMIT = 56 * 1024 * 1024


def _cparams(sem):
    return pltpu.CompilerParams(dimension_semantics=sem, vmem_limit_bytes=VMEM_LIMIT)


def _slab(c):
    return slice(c * LANES, (c + 1) * LANES)


def _in_proj_kernel(*refs, seq):
    x_refs = refs[:N_SLABS]
    (g_ref, w_ref, dw_ref, db_ref, lg_ref, lb_ref, u_ref, c_ref,
     h_ref, rn_ref, tmp_ref, uc_ref, z_ref, acc_ref) = refs[N_SLABS:]
    j = pl.program_id(1)
    rc = 128
    quarter = seq // 4

    @pl.when((pl.program_id(0) == 0) & (j == 0))
    def _():
        uc_ref[...] = jnp.zeros(uc_ref.shape, BF16)
        z_ref[...] = jnp.zeros(z_ref.shape, F32)
        acc_ref[...] = jnp.zeros(acc_ref.shape, F32)

    @pl.when(j == 0)
    def _():
        def natural(ci, carry):
            rows = pl.ds(pl.multiple_of(ci * rc, rc), rc)
            xs = [x_refs[c][rows, :] for c in range(N_SLABS)]
            ss = xs[0] * xs[0]
            for c in range(1, N_SLABS):
                ss = ss + xs[c] * xs[c]
            rn = lax.rsqrt(jnp.sum(ss, axis=-1, keepdims=True) * (1.0 / D_MODEL) + EPS)
            rn_ref[rows, :] = jnp.broadcast_to(rn, (rc, LANES))
            for c in range(N_SLABS):
                h_ref[0, rows, _slab(c)] = (xs[c] * rn * g_ref[:, _slab(c)]).astype(BF16)
            return carry

        lax.fori_loop(0, seq // rc, natural, 0)

        def by_four(r4, carry):
            base = pl.multiple_of(r4 * quarter, quarter)
            for a0 in range(0, quarter, rc):
                src = pl.ds(a0 * 4 + r4, rc, stride=4)
                rn = rn_ref[src, :]
                for c in range(N_SLABS):
                    y = x_refs[c][src, :] * rn * g_ref[:, _slab(c)]
                    h_ref[1, pl.ds(base + a0, rc), _slab(c)] = y.astype(BF16)
                    tmp_ref[c, a0:a0 + rc, :] = y
            for r2 in range(4):
                for c in range(N_SLABS):
                    h_ref[2, pl.ds(base + r2 * rc, rc), _slab(c)] = (
                        tmp_ref[c, pl.ds(r2, rc, stride=4), :].astype(BF16))
            return carry

        lax.fori_loop(0, 4, by_four, 0)

    jj = jnp.clip(j - 2, 0, 8)
    sel = jnp.where((j >= 2) & (j <= 10), lax.rem(jj, 3), 0)

    n_iter = CONV_TILE // CONV_CHUNK
    n_chunks = seq // CONV_CHUNK
    dot_rows = seq // n_iter

    def body(i, carry):
        rows = pl.ds(pl.multiple_of(i * dot_rows, dot_rows), dot_rows)
        u_ref[rows, :] = jnp.dot(h_ref[sel, rows, :], w_ref[...], preferred_element_type=F32).astype(BF16)
        tap_chunk = jnp.clip(j - 3, 0, seq // CONV_TILE - 1) * n_iter + i
        glu_chunk = jnp.clip((j - 3) * n_iter + i + 1, 0, n_chunks - 1)
        chunk_rows = pl.ds(pl.multiple_of(i * CONV_CHUNK, CONV_CHUNK), CONV_CHUNK)
        _conv_norm(acc_ref, (j + 1) & 1, chunk_rows, lg_ref, lb_ref, c_ref)
        _conv_taps(z_ref, pl.multiple_of(tap_chunk * CONV_CHUNK, CONV_CHUNK), dw_ref, db_ref,
                   acc_ref, j & 1, chunk_rows)
        _conv_glu(uc_ref, z_ref, pl.multiple_of(glu_chunk * CONV_CHUNK, CONV_CHUNK))
        return carry

    conv_active = (j >= 2) & (j <= seq // CONV_TILE + 3)

    @pl.when(conv_active)
    def _():
        lax.fori_loop(0, n_iter, body, 0)

    @pl.when(jnp.logical_not(conv_active))
    def _():
        u_ref[...] = jnp.dot(h_ref[sel], w_ref[...], preferred_element_type=F32).astype(BF16)

    @pl.when(j == 0)
    def _():
        uc_ref[:, 0:COL_TILE] = u_ref[...]

    @pl.when(j == 1)
    def _():
        uc_ref[:, COL_TILE:2 * COL_TILE] = u_ref[...]


def _conv_glu(uc_ref, z_ref, t0):
    a = uc_ref[pl.ds(t0, CONV_CHUNK), 0:CONV_DIM].astype(F32)
    gt = uc_ref[pl.ds(t0, CONV_CHUNK), CONV_DIM:2 * CONV_DIM].astype(F32)
    z_ref[pl.ds(t0 + CONV_HALO, CONV_CHUNK), :] = a * jax.nn.sigmoid(gt)


def _conv_norm(acc_ref, slot, chunk_rows, lg_ref, lb_ref, c_ref):
    acc = acc_ref[slot, chunk_rows, :]
    mu = jnp.mean(acc, axis=-1, keepdims=True)
    xc = acc - mu
    y = xc * lax.rsqrt(jnp.mean(xc * xc, axis=-1, keepdims=True) + EPS)
    y = y * lg_ref[...] + lb_ref[...]
    c_ref[chunk_rows, :] = (y * jax.nn.sigmoid(y)).astype(BF16)


def _conv_taps(z_ref, t0, w_ref, b_ref, acc_ref, slot, chunk_rows):
    chunk = CONV_CHUNK
    first_tap = CONV_HALO - (CONV_WIDTH - 1)
    n_win = chunk + CONV_HALO
    for lt in range(CONV_DIM // LANES):
        ls = _slab(lt)
        window = z_ref[pl.ds(t0, n_win), ls]
        acc = None
        for rho in range(SUBLANES):
            offs = [o for o in range(first_tap, first_tap + CONV_WIDTH) if o % SUBLANES == rho]
            rolled = window if rho == 0 else pltpu.roll(window, n_win - rho, axis=0)
            terms = [w_ref[o - first_tap:o - first_tap + 1, ls] * rolled[o - rho:o - rho + chunk, :]
                     for o in offs]
            while len(terms) > 1:
                terms = [terms[i] + terms[i + 1] for i in range(0, len(terms) - 1, 2)] + (
                    [terms[-1]] if len(terms) % 2 else [])
            acc = terms[0] if acc is None else acc + terms[0]
        acc_ref[slot, chunk_rows, ls] = acc + b_ref[:, ls]


def _in_proj(x2d, g, w, dw_w, dw_b, ln_g, ln_b, layer, batch, seq):
    assert seq // 16 == 128 and COL_TILE == CONV_DIM
    n_conv_tiles = seq // CONV_TILE
    x_specs = [pl.BlockSpec((seq, LANES),
                            functools.partial(lambda b, j, c: (jnp.minimum(b + jnp.minimum(j, 1), batch - 1), c), c=c))
               for c in range(N_SLABS)]
    vec = lambda: pl.BlockSpec((None, 1, CONV_DIM), lambda b, j: (layer, 0, 0))
    return pl.pallas_call(
        functools.partial(_in_proj_kernel, seq=seq),
        grid=(batch, N_COL_TILES),
        in_specs=x_specs + [
            pl.BlockSpec((None, 1, D_MODEL), lambda b, j: (layer, 0, 0)),
            pl.BlockSpec((None, D_MODEL, COL_TILE), lambda b, j: (layer, 0, j)),
            pl.BlockSpec((None, CONV_WIDTH, CONV_DIM), lambda b, j: (layer, 0, 0)),
            vec(), vec(), vec(),
        ],
        out_specs=[
            pl.BlockSpec((seq, COL_TILE), lambda b, j: (b, jnp.maximum(j - N_CONV_COL_TILES, 0))),
            pl.BlockSpec((CONV_TILE, CONV_DIM),
                         lambda b, j: (b * n_conv_tiles + jnp.clip(j - 4, 0, n_conv_tiles - 1), 0)),
        ],
        out_shape=[jax.ShapeDtypeStruct((batch * seq, U_COLS), BF16),
                   jax.ShapeDtypeStruct((batch * seq, CONV_DIM), BF16)],
        scratch_shapes=[pltpu.VMEM((N_GROUPS, seq, D_MODEL), BF16),
                        pltpu.VMEM((seq, LANES), F32),
                        pltpu.VMEM((N_SLABS, seq // 4, LANES), F32),
                        pltpu.VMEM((seq, 2 * CONV_DIM), BF16),
                        pltpu.VMEM((CONV_HALO + seq, CONV_DIM), F32),
                        pltpu.VMEM((2, CONV_TILE, CONV_DIM), F32)],
        compiler_params=_cparams(("arbitrary", "arbitrary")),
        name="in_proj",
    )(*([x2d] * N_SLABS), g, w, dw_w, dw_b, ln_g, ln_b)


def _attn_kernel(bkt_ref, rb_ref, bsum_ref, qge_ref, qgo_ref, kg_ref, q_ref, k_ref, v_ref,
                 o_ref, lse_ref,
                 bm_ref, qe_ref, qo_ref, kn_ref, va_ref, s_ref, *stage, seq, d):
    sub_len = seq // d
    nb = sub_len // BLOCK

    @pl.when(pl.program_id(0) == 0)
    def _():
        bk = bkt_ref[...]
        for h in range(HEADS_PER_GROUP):
            acc = jnp.full((BLOCK, 2 * BLOCK), NEG_INF, F32)
            for b in range(NUM_BUCKETS):
                acc = jnp.where(bk == b, rb_ref[b * HEADS_PER_GROUP + h], acc)
            bm_ref[h // 2, (h % 2) * BLOCK:(h % 2 + 1) * BLOCK, :] = acc
        va_ref[:, :, LANES:2 * LANES] = jnp.ones((N_PAIRS, seq, LANES), BF16)

    rc = 256

    def head_mean_sq(t):
        sq = (t * t).astype(BF16)
        half = GROUP_DIM // 2
        return jnp.concatenate(
            [jnp.dot(sq[:, 0:half], bsum_ref[...], preferred_element_type=F32),
             jnp.dot(sq[:, half:GROUP_DIM], bsum_ref[...], preferred_element_type=F32)], axis=1)

    def norm_body(ci, carry):
        rows = pl.ds(pl.multiple_of(ci * rc, rc), rc)
        q = q_ref[0, rows, :].astype(F32)
        qr = q * lax.rsqrt(head_mean_sq(q) + EPS)
        qe_ref[rows, :] = (qr * qge_ref[...]).astype(BF16)
        qo_ref[rows, :] = (qr * qgo_ref[...]).astype(BF16)
        k = k_ref[0, rows, :].astype(F32)
        kr = k * lax.rsqrt(head_mean_sq(k) + EPS)
        kn_ref[rows, :] = (kr * kg_ref[...]).astype(BF16)
        for p in range(N_PAIRS):
            va_ref[p, rows, 0:LANES] = v_ref[0, rows, _slab(p)]
        return carry

    lax.fori_loop(0, seq // rc, norm_body, 0)

    lt64 = lax.broadcasted_iota(jnp.int32, (BLOCK, LANES), 1) < HEAD_DIM

    def block_aligned(row):
        return row if isinstance(row, int) else pl.multiple_of(row, BLOCK)

    def key_rows(row0, first):
        return pl.ds(row0, BLOCK) if first else pl.ds(block_aligned(row0 - BLOCK), 2 * BLOCK)

    def score_products(row0, first):
        row0 = block_aligned(row0)
        qrows, krows = pl.ds(row0, BLOCK), key_rows(row0, first)
        out = []
        for p in range(N_PAIRS):
            q2 = jnp.concatenate([qe_ref[qrows, _slab(p)], qo_ref[qrows, _slab(p)]], axis=0)
            out.append(lax.dot_general(q2, kn_ref[krows, _slab(p)], (((1,), (1,)), ((), ())),
                                       preferred_element_type=F32))
        return out

    def park(slot, chain, scores, first):
        width = BLOCK if first else 2 * BLOCK
        for p in range(N_PAIRS):
            s_ref[slot, chain, p, :, 0:width] = scores[p]

    def unpark(slot, chain, first):
        width = BLOCK if first else 2 * BLOCK
        return [s_ref[slot, chain, p, :, 0:width] for p in range(N_PAIRS)]

    def finish_blocks(blocks):
        soft = []
        for scores, row0, t0, first in blocks:
            probs, maxes = [], []
            for p in range(N_PAIRS):
                s = scores[p] + (bm_ref[p, :, BLOCK:2 * BLOCK] if first else bm_ref[p])
                m = jnp.max(s, axis=-1, keepdims=True)
                probs.append(jnp.exp(s - m).astype(BF16))
                maxes.append(m)
            soft.append((probs, maxes))
        results = []
        for (probs, _), (_, row0, _, first) in zip(soft, blocks):
            krows = key_rows(block_aligned(row0), first)
            results.append([jnp.dot(probs[p], va_ref[p, krows, :], preferred_element_type=F32)
                            for p in range(N_PAIRS)])
        for res, (_, maxes), (_, row0, t0, _) in zip(results, soft, blocks):
            qrows = pl.ds(block_aligned(row0), BLOCK)
            for p in range(N_PAIRS):
                re, ro = res[p][:BLOCK], res[p][BLOCK:]
                me, mo = maxes[p][:BLOCK], maxes[p][BLOCK:]
                denom = jnp.where(lt64, re[:, LANES:], ro[:, LANES:])
                o_pair = jnp.where(lt64, re[:, :LANES], ro[:, :LANES]) / denom
                lse_pair = jnp.where(lt64, me, mo) + jnp.log(denom)
                if d == 1:
                    o_ref[0, qrows, _slab(p)] = o_pair.astype(BF16)
                    lse_ref[0, qrows, _slab(p)] = lse_pair
                else:
                    nat = pl.ds(t0, BLOCK, stride=d)
                    stage[0][p, nat, :] = o_pair
                    stage[1][p, nat, :] = lse_pair

    def run_chains(chains, n_blocks):
        def is_first(mode, n):
            return mode == "all" or (mode == "head" and n == 0)

        def row(base, n):
            return base + n * BLOCK

        def issue(slot, n, static_n):
            for ci, (base, mode, _) in enumerate(chains):
                first = is_first(mode, static_n)
                park(slot, ci, score_products(row(base, n), first), first)

        def finish(slot, n, static_n):
            finish_blocks([(unpark(slot, ci, is_first(mode, static_n)), row(base, n), t0_of(n), is_first(mode, static_n))
                           for ci, (base, mode, t0_of) in enumerate(chains)])

        issue(0, 0, 0)
        issue(1, 1, 1)
        finish(0, 0, 0)

        def blk_body(n, c):
            slot = n & 1
            prev = [(unpark(1 - slot, ci, is_first(mode, 2)), row(base, n - 1), t0_of(n - 1), is_first(mode, 2))
                    for ci, (base, mode, t0_of) in enumerate(chains)]
            issue(slot, n, 2)
            finish_blocks(prev)
            return c

        lax.fori_loop(2, n_blocks, blk_body, 0)
        finish((n_blocks - 1) & 1, n_blocks - 1, 2)

    def residue_of_class(c):
        if isinstance(c, int):
            return c // 4 + 4 * (c % 4)
        return lax.shift_right_logical(c, 2) + 4 * (c & 3)

    if d == 1:
        run_chains([(0, "head", lambda n: n * BLOCK)], nb)
    elif nb > 1:
        def class_body(r, carry):
            run_chains([(pl.multiple_of(r * sub_len, BLOCK), "head", lambda n: n * (BLOCK * d) + r)], nb)
            return carry

        lax.fori_loop(0, d, class_body, 0)
    else:
        run_chains([(0, "all", residue_of_class)], d)

    if d > 1:
        def copy_body(ci, carry):
            rows = pl.ds(pl.multiple_of(ci * rc, rc), rc)
            for p in range(N_PAIRS):
                o_ref[0, rows, _slab(p)] = stage[0][p, rows, :].astype(BF16)
                lse_ref[0, rows, _slab(p)] = stage[1][p, rows, :]
            return carry

        lax.fori_loop(0, seq // rc, copy_body, 0)


def _attn_group(u3, bkt, rb, bsum, qge, qgo, kg, group, batch, seq):
    d = DILATIONS[group]
    assert d in (1, 4, 16) and (seq // d) % BLOCK == 0
    const = lambda shape: pl.BlockSpec(shape, lambda b: (0,) * len(shape))
    col = lambda first_tile: pl.BlockSpec((1, seq, GROUP_DIM), lambda b: (b, 0, first_tile + group))
    scratch = [
        pltpu.VMEM((N_PAIRS, 2 * BLOCK, 2 * BLOCK), F32),
        pltpu.VMEM((seq, GROUP_DIM), BF16),
        pltpu.VMEM((seq, GROUP_DIM), BF16),
        pltpu.VMEM((seq, GROUP_DIM), BF16),
        pltpu.VMEM((N_PAIRS, seq, 2 * LANES), BF16),
        pltpu.VMEM((2, 1, N_PAIRS, 2 * BLOCK, 2 * BLOCK), F32),
    ]
    if d > 1:
        scratch += [pltpu.VMEM((N_PAIRS, seq, LANES), F32), pltpu.VMEM((N_PAIRS, seq, LANES), F32)]
    return pl.pallas_call(
        functools.partial(_attn_kernel, seq=seq, d=d),
        grid=(batch,),
        in_specs=[
            const((BLOCK, 2 * BLOCK)),
            pl.BlockSpec(memory_space=pltpu.SMEM),
            const((GROUP_DIM // 2, GROUP_DIM // 2)),
            const((1, GROUP_DIM)), const((1, GROUP_DIM)), const((1, GROUP_DIM)),
            col(0), col(N_GROUPS), col(2 * N_GROUPS),
        ],
        out_specs=[pl.BlockSpec((1, seq, GROUP_DIM), lambda b: (b, 0, 0))] * 2,
        out_shape=[jax.ShapeDtypeStruct((batch, seq, GROUP_DIM), BF16),
                   jax.ShapeDtypeStruct((batch, seq, GROUP_DIM), F32)],
        scratch_shapes=scratch,
        compiler_params=_cparams(("arbitrary",)),
        name=f"attn_d{d}",
    )(bkt, rb, bsum, qge, qgo, kg, u3, u3, u3)


def _merge_ffn_kernel(x_ref, c_ref, o0_ref, o1_ref, o2_ref, l0_ref, l1_ref, l2_ref,
                      gc0_ref, gc1_ref, ga0_ref, ga1_ref, wc_ref, wa_ref, wo_ref,
                      g2_ref, w1_ref, w2_ref, out_ref):
    l0, l1, l2 = l0_ref[...], l1_ref[...], l2_ref[...]
    m = jnp.maximum(jnp.maximum(l0, l1), l2)
    e0, e1, e2 = jnp.exp(l0 - m), jnp.exp(l1 - m), jnp.exp(l2 - m)
    o = (e0 * o0_ref[...].astype(F32) + e1 * o1_ref[...].astype(F32)
         + e2 * o2_ref[...].astype(F32)) / (e0 + e1 + e2)
    y_attn = jnp.dot(o.astype(BF16), wa_ref[...], preferred_element_type=F32)
    y_conv = jnp.dot(c_ref[...], wc_ref[...], preferred_element_type=F32)
    half = D_MODEL // 2
    mix = []
    for hs, gc_ref, ga_ref in ((slice(0, half), gc0_ref, ga0_ref), (slice(half, D_MODEL), gc1_ref, ga1_ref)):
        gc = jax.nn.sigmoid(gc_ref[...].astype(F32))
        ga = jax.nn.sigmoid(ga_ref[...].astype(F32))
        mix.append((gc * y_conv[:, hs] + ga * y_attn[:, hs]).astype(BF16))
    y = (jnp.dot(mix[0], wo_ref[0:half, :], preferred_element_type=F32)
         + jnp.dot(mix[1], wo_ref[half:D_MODEL, :], preferred_element_type=F32))
    x = x_ref[...] + y

    rn = lax.rsqrt(jnp.mean(x * x, axis=-1, keepdims=True) + EPS)
    h = (x * rn * g2_ref[...]).astype(BF16)
    y2 = None
    for f0 in range(0, D_FF, FF_CHUNK):
        a = jnp.dot(h, w1_ref[:, f0:f0 + FF_CHUNK], preferred_element_type=F32)
        a = jnp.square(jnp.maximum(a, 0.0)).astype(BF16)
        t = jnp.dot(a, w2_ref[f0:f0 + FF_CHUNK, :], preferred_element_type=F32)
        y2 = t if y2 is None else y2 + t
    out_ref[...] = x + y2


def _merge_ffn(x2d, c2d, outs, lses, u2d, wc, wa, wo, g2, w1, w2, layer, m_rows):
    tm = MIX_ROWS
    resident = pl.Buffered(1)
    row = lambda width: pl.BlockSpec((tm, width), lambda i: (i, 0))
    gate = lambda tile: pl.BlockSpec((tm, COL_TILE), lambda i: (i, tile))
    wspec = lambda k, n: pl.BlockSpec((None, k, n), lambda i: (layer, 0, 0), pipeline_mode=resident)
    gate0 = 3 * ATTN_DIM // COL_TILE
    return pl.pallas_call(
        _merge_ffn_kernel,
        grid=(m_rows // tm,),
        in_specs=[row(D_MODEL), row(CONV_DIM)] + [row(GROUP_DIM)] * 6
                 + [gate(gate0), gate(gate0 + 1), gate(gate0 + 2), gate(gate0 + 3)]
                 + [wspec(CONV_DIM, D_MODEL), wspec(GROUP_DIM, D_MODEL), wspec(D_MODEL, D_MODEL),
                    pl.BlockSpec((None, 1, D_MODEL), lambda i: (layer, 0, 0)),
                    wspec(D_MODEL, D_FF), wspec(D_FF, D_MODEL)],
        out_specs=row(D_MODEL),
        out_shape=jax.ShapeDtypeStruct((m_rows, D_MODEL), F32),
        compiler_params=_cparams(("arbitrary",)),
        name="merge_ffn",
    )(x2d, c2d, *outs, *lses, u2d, u2d, u2d, u2d, wc, wa, wo, g2, w1, w2)


def _t5_bucket(dist):
    max_exact = NUM_BUCKETS // 2
    nf = jnp.maximum(dist, 1).astype(jnp.float32)
    large = max_exact + (jnp.log(nf / max_exact) / math.log(MAX_REL_DISTANCE / max_exact)
                         * (NUM_BUCKETS - max_exact)).astype(jnp.int32)
    large = jnp.minimum(large, NUM_BUCKETS - 1)
    return jnp.where(dist < max_exact, dist, large)


def _bucket_tile(d):
    qi = jnp.arange(BLOCK)[:, None]
    kj = jnp.arange(2 * BLOCK)[None, :]
    off = qi + BLOCK - kj
    band = (off >= 0) & (off <= SUB_WINDOW)
    bucket = _t5_bucket(jnp.clip(off, 0, SUB_WINDOW) * d)
    return jnp.where(band, bucket, -1).astype(jnp.int32)


def kernel(x, rel_bias, norm1_g, w_in, q_norm_g, k_norm_g, conv_dw_w, conv_dw_b, conv_ln_g, conv_ln_b,
           w_conv_out, w_attn_out, w_out, norm2_g, w_ff1, w_ff2):
    batch, seq, _ = x.shape
    depth = w_in.shape[0]
    m_rows = batch * seq
    assert seq % (BLOCK * max(DILATIONS)) == 0 and x.shape[2] == D_MODEL

    w_in_b, w_conv_b, w_attn_b, w_out_b = (w.astype(BF16) for w in (w_in, w_conv_out, w_attn_out, w_out))
    w_ff1_b, w_ff2_b = w_ff1.astype(BF16), w_ff2.astype(BF16)
    head_of_col = jnp.arange(GROUP_DIM) // HEAD_DIM
    even_head = (head_of_col % 2 == 0).astype(F32)[None, :]
    qg_all = jnp.tile(q_norm_g, (1, HEADS_PER_GROUP)) * (HEAD_DIM ** -0.5)
    qge_all, qgo_all = qg_all * even_head, qg_all * (1.0 - even_head)
    kg_all = jnp.tile(k_norm_g, (1, HEADS_PER_GROUP))
    head_of_half = head_of_col[:GROUP_DIM // 2]
    bsum = ((head_of_half[:, None] == head_of_half[None, :]).astype(F32) * (1.0 / HEAD_DIM)).astype(BF16)
    bkts = [_bucket_tile(d) for d in DILATIONS]
    rbs = [rel_bias[:, g * HEADS_PER_GROUP:(g + 1) * HEADS_PER_GROUP].reshape(-1).astype(F32)
           for g in range(N_GROUPS)]

    vec3 = lambda a: a.reshape(depth, 1, a.shape[-1])
    norm1_g, norm2_g, conv_dw_b, conv_ln_g, conv_ln_b = (vec3(a) for a in (norm1_g, norm2_g, conv_dw_b, conv_ln_g, conv_ln_b))
    x2d = x.reshape(m_rows, D_MODEL)
    for layer in range(depth):
        u2d, c2d = _in_proj(x2d, norm1_g, w_in_b, conv_dw_w, conv_dw_b, conv_ln_g, conv_ln_b, layer, batch, seq)
        u3 = u2d.reshape(batch, seq, U_COLS)
        outs, lses = [], []
        for g in range(N_GROUPS):
            sl = slice(layer, layer + 1)
            o, lse = _attn_group(u3, bkts[g], rbs[g], bsum, qge_all[sl], qgo_all[sl], kg_all[sl], g, batch, seq)
            outs.append(o.reshape(m_rows, GROUP_DIM))
            lses.append(lse.reshape(m_rows, GROUP_DIM))
        x2d = _merge_ffn(x2d, c2d, outs, lses, u2d, w_conv_b, w_attn_b, w_out_b,
                         norm2_g, w_ff1_b, w_ff2_b, layer, m_rows)
    return x2d.reshape(batch, seq, D_MODEL)
```

```python
import functools
import math

import jax
import jax.numpy as jnp
from jax import lax
from jax.experimental import pallas as pl
from jax.experimental.pallas import tpu as pltpu

F32 = jnp.float32
BF16 = jnp.bfloat16

D_MODEL = 1024
CONV_DIM = 512
CONV_WIDTH = 31
N_GROUPS = 3
HEADS_PER_GROUP = 8
HEAD_DIM = 64
GROUP_DIM = HEADS_PER_GROUP * HEAD_DIM
ATTN_DIM = N_GROUPS * GROUP_DIM
DILATIONS = (1, 4, 16)
SUB_WINDOW = 128
BLOCK = 128
NUM_BUCKETS = 32
MAX_REL_DISTANCE = 2048
D_FF = 4 * D_MODEL
EPS = 1e-6
NEG_INF = -1e30
IN_COLS = 2 * CONV_DIM + 3 * ATTN_DIM + 2 * D_MODEL

LANES = 128
SUBLANES = 8
COL_TILE = 512
N_COL_TILES = IN_COLS // COL_TILE
N_CONV_COL_TILES = 2 * CONV_DIM // COL_TILE
U_COLS = IN_COLS - 2 * CONV_DIM
N_SLABS = D_MODEL // LANES
N_PAIRS = HEADS_PER_GROUP // 2
MIX_ROWS = 512
FF_CHUNK = 2048
CONV_HALO = 32
CONV_TILE = 256
CONV_CHUNK = 128
VMEM_LIMIT = 56 * 1024 * 1024


def _cparams(sem):
    return pltpu.CompilerParams(dimension_semantics=sem, vmem_limit_bytes=VMEM_LIMIT)


def _slab(c):
    return slice(c * LANES, (c + 1) * LANES)


def _in_proj_kernel(*refs, seq):
    x_refs = refs[:N_SLABS]
    (g_ref, w_ref, dw_ref, db_ref, lg_ref, lb_ref, u_ref, c_ref,
     h_ref, rn_ref, tmp_ref, uc_ref, z_ref, acc_ref) = refs[N_SLABS:]
    j = pl.program_id(1)
    rc = 128
    quarter = seq // 4

    @pl.when((pl.program_id(0) == 0) & (j == 0))
    def _():
        uc_ref[...] = jnp.zeros(uc_ref.shape, BF16)
        z_ref[...] = jnp.zeros(z_ref.shape, F32)
        acc_ref[...] = jnp.zeros(acc_ref.shape, F32)

    @pl.when(j == 0)
    def _():
        def natural(ci, carry):
            rows = pl.ds(pl.multiple_of(ci * rc, rc), rc)
            xs = [x_refs[c][rows, :] for c in range(N_SLABS)]
            ss = xs[0] * xs[0]
            for c in range(1, N_SLABS):
                ss = ss + xs[c] * xs[c]
            rn = lax.rsqrt(jnp.sum(ss, axis=-1, keepdims=True) * (1.0 / D_MODEL) + EPS)
            rn_ref[rows, :] = jnp.broadcast_to(rn, (rc, LANES))
            for c in range(N_SLABS):
                h_ref[0, rows, _slab(c)] = (xs[c] * rn * g_ref[:, _slab(c)]).astype(BF16)
            return carry

        lax.fori_loop(0, seq // rc, natural, 0, unroll=4)

        def by_four(r4, carry):
            base = pl.multiple_of(r4 * quarter, quarter)
            for a0 in range(0, quarter, rc):
                src = pl.ds(a0 * 4 + r4, rc, stride=4)
                rn = rn_ref[src, :]
                for c in range(N_SLABS):
                    y = x_refs[c][src, :] * rn * g_ref[:, _slab(c)]
                    h_ref[1, pl.ds(base + a0, rc), _slab(c)] = y.astype(BF16)
                    tmp_ref[c, a0:a0 + rc, :] = y
            for r2 in range(4):
                for c in range(N_SLABS):
                    h_ref[2, pl.ds(base + r2 * rc, rc), _slab(c)] = (
                        tmp_ref[c, pl.ds(r2, rc, stride=4), :].astype(BF16))
            return carry

        lax.fori_loop(0, 4, by_four, 0)

    jj = jnp.clip(j - 2, 0, 8)
    sel = jnp.where((j >= 2) & (j <= 10), lax.rem(jj, 3), 0)

    n_iter = CONV_TILE // CONV_CHUNK
    n_chunks = seq // CONV_CHUNK
    dot_rows = seq // n_iter

    def body(i, carry):
        rows = pl.ds(pl.multiple_of(i * dot_rows, dot_rows), dot_rows)
        u_ref[rows, :] = jnp.dot(h_ref[sel, rows, :], w_ref[...], preferred_element_type=F32).astype(BF16)
        tap_chunk = jnp.clip(j - 3, 0, seq // CONV_TILE - 1) * n_iter + i
        glu_chunk = jnp.clip((j - 3) * n_iter + i + 1, 0, n_chunks - 1)
        chunk_rows = pl.ds(pl.multiple_of(i * CONV_CHUNK, CONV_CHUNK), CONV_CHUNK)
        _conv_norm(acc_ref, (j + 1) & 1, chunk_rows, lg_ref, lb_ref, c_ref)
        _conv_taps(z_ref, pl.multiple_of(tap_chunk * CONV_CHUNK, CONV_CHUNK), dw_ref, db_ref,
                   acc_ref, j & 1, chunk_rows)
        _conv_glu(uc_ref, z_ref, pl.multiple_of(glu_chunk * CONV_CHUNK, CONV_CHUNK))
        return carry

    conv_active = (j >= 2) & (j <= seq // CONV_TILE + 3)

    @pl.when(conv_active)
    def _():
        lax.fori_loop(0, n_iter, body, 0)

    @pl.when(jnp.logical_not(conv_active))
    def _():
        u_ref[...] = jnp.dot(h_ref[sel], w_ref[...], preferred_element_type=F32).astype(BF16)

    @pl.when(j == 0)
    def _():
        uc_ref[:, 0:COL_TILE] = u_ref[...]

    @pl.when(j == 1)
    def _():
        uc_ref[:, COL_TILE:2 * COL_TILE] = u_ref[...]


def _conv_glu(uc_ref, z_ref, t0):
    a = uc_ref[pl.ds(t0, CONV_CHUNK), 0:CONV_DIM].astype(F32)
    gt = uc_ref[pl.ds(t0, CONV_CHUNK), CONV_DIM:2 * CONV_DIM].astype(F32)
    z_ref[pl.ds(t0 + CONV_HALO, CONV_CHUNK), :] = a * jax.nn.sigmoid(gt)


def _conv_norm(acc_ref, slot, chunk_rows, lg_ref, lb_ref, c_ref):
    acc = acc_ref[slot, chunk_rows, :]
    mu = jnp.mean(acc, axis=-1, keepdims=True)
    xc = acc - mu
    y = xc * lax.rsqrt(jnp.mean(xc * xc, axis=-1, keepdims=True) + EPS)
    y = y * lg_ref[...] + lb_ref[...]
    c_ref[chunk_rows, :] = (y * jax.nn.sigmoid(y)).astype(BF16)


def _conv_taps(z_ref, t0, w_ref, b_ref, acc_ref, slot, chunk_rows):
    chunk = CONV_CHUNK
    first_tap = CONV_HALO - (CONV_WIDTH - 1)
    n_win = chunk + CONV_HALO
    for lt in range(CONV_DIM // LANES):
        ls = _slab(lt)
        window = z_ref[pl.ds(t0, n_win), ls]
        acc = None
        for rho in range(SUBLANES):
            offs = [o for o in range(first_tap, first_tap + CONV_WIDTH) if o % SUBLANES == rho]
            rolled = window if rho == 0 else pltpu.roll(window, n_win - rho, axis=0)
            terms = [w_ref[o - first_tap:o - first_tap + 1, ls] * rolled[o - rho:o - rho + chunk, :]
                     for o in offs]
            while len(terms) > 1:
                terms = [terms[i] + terms[i + 1] for i in range(0, len(terms) - 1, 2)] + (
                    [terms[-1]] if len(terms) % 2 else [])
            acc = terms[0] if acc is None else acc + terms[0]
        acc_ref[slot, chunk_rows, ls] = acc + b_ref[:, ls]


def _in_proj(x2d, g, w, dw_w, dw_b, ln_g, ln_b, layer, batch, seq):
    assert seq // 16 == 128 and COL_TILE == CONV_DIM
    n_conv_tiles = seq // CONV_TILE
    x_specs = [pl.BlockSpec((seq, LANES),
                            functools.partial(lambda b, j, c: (jnp.minimum(b + jnp.minimum(j, 1), batch - 1), c), c=c))
               for c in range(N_SLABS)]
    vec = lambda: pl.BlockSpec((None, 1, CONV_DIM), lambda b, j: (layer, 0, 0))
    return pl.pallas_call(
        functools.partial(_in_proj_kernel, seq=seq),
        grid=(batch, N_COL_TILES),
        in_specs=x_specs + [
            pl.BlockSpec((None, 1, D_MODEL), lambda b, j: (layer, 0, 0)),
            pl.BlockSpec((None, D_MODEL, COL_TILE), lambda b, j: (layer, 0, j)),
            pl.BlockSpec((None, CONV_WIDTH, CONV_DIM), lambda b, j: (layer, 0, 0)),
            vec(), vec(), vec(),
        ],
        out_specs=[
            pl.BlockSpec((seq, COL_TILE), lambda b, j: (b, jnp.maximum(j - N_CONV_COL_TILES, 0))),
            pl.BlockSpec((CONV_TILE, CONV_DIM),
                         lambda b, j: (b * n_conv_tiles + jnp.clip(j - 4, 0, n_conv_tiles - 1), 0)),
        ],
        out_shape=[jax.ShapeDtypeStruct((batch * seq, U_COLS), BF16),
                   jax.ShapeDtypeStruct((batch * seq, CONV_DIM), BF16)],
        scratch_shapes=[pltpu.VMEM((N_GROUPS, seq, D_MODEL), BF16),
                        pltpu.VMEM((seq, LANES), F32),
                        pltpu.VMEM((N_SLABS, seq // 4, LANES), F32),
                        pltpu.VMEM((seq, 2 * CONV_DIM), BF16),
                        pltpu.VMEM((CONV_HALO + seq, CONV_DIM), F32),
                        pltpu.VMEM((2, CONV_TILE, CONV_DIM), F32)],
        compiler_params=_cparams(("arbitrary", "arbitrary")),
        name="in_proj",
    )(*([x2d] * N_SLABS), g, w, dw_w, dw_b, ln_g, ln_b)


def _attn_kernel(bkt_ref, rb_ref, bsum_ref, qge_ref, qgo_ref, kg_ref, q_ref, k_ref, v_ref,
                 o_ref, lse_ref,
                 bm_ref, qe_ref, qo_ref, kn_ref, va_ref, s_ref, *stage, seq, d):
    sub_len = seq // d
    nb = sub_len // BLOCK

    @pl.when(pl.program_id(0) == 0)
    def _():
        bk = bkt_ref[...]
        for h in range(HEADS_PER_GROUP):
            acc = jnp.full((BLOCK, 2 * BLOCK), NEG_INF, F32)
            for b in range(NUM_BUCKETS):
                acc = jnp.where(bk == b, rb_ref[b * HEADS_PER_GROUP + h], acc)
            bm_ref[h // 2, (h % 2) * BLOCK:(h % 2 + 1) * BLOCK, :] = acc
        va_ref[:, :, LANES:2 * LANES] = jnp.ones((N_PAIRS, seq, LANES), BF16)

    rc = 256

    def head_mean_sq(t):
        sq = (t * t).astype(BF16)
        half = GROUP_DIM // 2
        return jnp.concatenate(
            [jnp.dot(sq[:, 0:half], bsum_ref[...], preferred_element_type=F32),
             jnp.dot(sq[:, half:GROUP_DIM], bsum_ref[...], preferred_element_type=F32)], axis=1)

    def norm_body(ci, carry):
        rows = pl.ds(pl.multiple_of(ci * rc, rc), rc)
        q = q_ref[0, rows, :].astype(F32)
        qr = q * lax.rsqrt(head_mean_sq(q) + EPS)
        qe_ref[rows, :] = (qr * qge_ref[...]).astype(BF16)
        qo_ref[rows, :] = (qr * qgo_ref[...]).astype(BF16)
        k = k_ref[0, rows, :].astype(F32)
        kr = k * lax.rsqrt(head_mean_sq(k) + EPS)
        kn_ref[rows, :] = (kr * kg_ref[...]).astype(BF16)
        for p in range(N_PAIRS):
            va_ref[p, rows, 0:LANES] = v_ref[0, rows, _slab(p)]
        return carry

    lax.fori_loop(0, seq // rc, norm_body, 0, unroll=4)

    lt64 = lax.broadcasted_iota(jnp.int32, (BLOCK, LANES), 1) < HEAD_DIM

    def block_aligned(row):
        return row if isinstance(row, int) else pl.multiple_of(row, BLOCK)

    def key_rows(row0, first):
        return pl.ds(row0, BLOCK) if first else pl.ds(block_aligned(row0 - BLOCK), 2 * BLOCK)

    def score_products(row0, first):
        row0 = block_aligned(row0)
        qrows, krows = pl.ds(row0, BLOCK), key_rows(row0, first)
        out = []
        for p in range(N_PAIRS):
            q2 = jnp.concatenate([qe_ref[qrows, _slab(p)], qo_ref[qrows, _slab(p)]], axis=0)
            out.append(lax.dot_general(q2, kn_ref[krows, _slab(p)], (((1,), (1,)), ((), ())),
                                       preferred_element_type=F32))
        return out

    def park(slot, chain, scores, first):
        width = BLOCK if first else 2 * BLOCK
        for p in range(N_PAIRS):
            s_ref[slot, chain, p, :, 0:width] = scores[p]

    def unpark(slot, chain, first):
        width = BLOCK if first else 2 * BLOCK
        return [s_ref[slot, chain, p, :, 0:width] for p in range(N_PAIRS)]

    def finish_blocks(blocks):
        soft = []
        for scores, row0, t0, first in blocks:
            probs, maxes = [], []
            for p in range(N_PAIRS):
                s = scores[p] + (bm_ref[p, :, BLOCK:2 * BLOCK] if first else bm_ref[p])
                m = jnp.max(s, axis=-1, keepdims=True)
                probs.append(jnp.exp(s - m).astype(BF16))
                maxes.append(m)
            soft.append((probs, maxes))
        results = []
        for (probs, _), (_, row0, _, first) in zip(soft, blocks):
            krows = key_rows(block_aligned(row0), first)
            results.append([jnp.dot(probs[p], va_ref[p, krows, :], preferred_element_type=F32)
                            for p in range(N_PAIRS)])
        for res, (_, maxes), (_, row0, t0, _) in zip(results, soft, blocks):
            qrows = pl.ds(block_aligned(row0), BLOCK)
            for p in range(N_PAIRS):
                re, ro = res[p][:BLOCK], res[p][BLOCK:]
                me, mo = maxes[p][:BLOCK], maxes[p][BLOCK:]
                denom = jnp.where(lt64, re[:, LANES:], ro[:, LANES:])
                o_pair = jnp.where(lt64, re[:, :LANES], ro[:, :LANES]) / denom
                lse_pair = jnp.where(lt64, me, mo) + jnp.log(denom)
                if d == 1:
                    o_ref[0, qrows, _slab(p)] = o_pair.astype(BF16)
                    lse_ref[0, qrows, _slab(p)] = lse_pair
                else:
                    nat = pl.ds(t0, BLOCK, stride=d)
                    stage[0][p, nat, :] = o_pair
                    stage[1][p, nat, :] = lse_pair

    def run_chains(chains, n_blocks):
        def is_first(mode, n):
            return mode == "all" or (mode == "head" and n == 0)

        def row(base, n):
            return base + n * BLOCK

        def issue(slot, n, static_n):
            for ci, (base, mode, _) in enumerate(chains):
                first = is_first(mode, static_n)
                park(slot, ci, score_products(row(base, n), first), first)

        def finish(slot, n, static_n):
            finish_blocks([(unpark(slot, ci, is_first(mode, static_n)), row(base, n), t0_of(n), is_first(mode, static_n))
                           for ci, (base, mode, t0_of) in enumerate(chains)])

        issue(0, 0, 0)
        issue(1, 1, 1)
        finish(0, 0, 0)

        def blk_body(n, c):
            slot = n & 1
            prev = [(unpark(1 - slot, ci, is_first(mode, 2)), row(base, n - 1), t0_of(n - 1), is_first(mode, 2))
                    for ci, (base, mode, t0_of) in enumerate(chains)]
            issue(slot, n, 2)
            finish_blocks(prev)
            return c

        lax.fori_loop(2, n_blocks, blk_body, 0, unroll=2)
        finish((n_blocks - 1) & 1, n_blocks - 1, 2)

    def residue_of_class(c):
        if isinstance(c, int):
            return c // 4 + 4 * (c % 4)
        return lax.shift_right_logical(c, 2) + 4 * (c & 3)

    if d == 1:
        run_chains([(0, "head", lambda n: n * BLOCK)], nb)
    elif nb > 1:
        def class_body(r, carry):
            run_chains([(pl.multiple_of(r * sub_len, BLOCK), "head", lambda n: n * (BLOCK * d) + r)], nb)
            return carry

        lax.fori_loop(0, d, class_body, 0)
    else:
        run_chains([(0, "all", residue_of_class)], d)

    if d > 1:
        def copy_body(ci, carry):
            rows = pl.ds(pl.multiple_of(ci * rc, rc), rc)
            for p in range(N_PAIRS):
                o_ref[0, rows, _slab(p)] = stage[0][p, rows, :].astype(BF16)
                lse_ref[0, rows, _slab(p)] = stage[1][p, rows, :]
            return carry

        lax.fori_loop(0, seq // rc, copy_body, 0, unroll=4)


def _attn_group(u3, bkt, rb, bsum, qge, qgo, kg, group, batch, seq):
    d = DILATIONS[group]
    assert d in (1, 4, 16) and (seq // d) % BLOCK == 0
    const = lambda shape: pl.BlockSpec(shape, lambda b: (0,) * len(shape))
    col = lambda first_tile: pl.BlockSpec((1, seq, GROUP_DIM), lambda b: (b, 0, first_tile + group))
    scratch = [
        pltpu.VMEM((N_PAIRS, 2 * BLOCK, 2 * BLOCK), F32),
        pltpu.VMEM((seq, GROUP_DIM), BF16),
        pltpu.VMEM((seq, GROUP_DIM), BF16),
        pltpu.VMEM((seq, GROUP_DIM), BF16),
        pltpu.VMEM((N_PAIRS, seq, 2 * LANES), BF16),
        pltpu.VMEM((2, 1, N_PAIRS, 2 * BLOCK, 2 * BLOCK), F32),
    ]
    if d > 1:
        scratch += [pltpu.VMEM((N_PAIRS, seq, LANES), F32), pltpu.VMEM((N_PAIRS, seq, LANES), F32)]
    return pl.pallas_call(
        functools.partial(_attn_kernel, seq=seq, d=d),
        grid=(batch,),
        in_specs=[
            const((BLOCK, 2 * BLOCK)),
            pl.BlockSpec(memory_space=pltpu.SMEM),
            const((GROUP_DIM // 2, GROUP_DIM // 2)),
            const((1, GROUP_DIM)), const((1, GROUP_DIM)), const((1, GROUP_DIM)),
            col(0), col(N_GROUPS), col(2 * N_GROUPS),
        ],
        out_specs=[pl.BlockSpec((1, seq, GROUP_DIM), lambda b: (b, 0, 0))] * 2,
        out_shape=[jax.ShapeDtypeStruct((batch, seq, GROUP_DIM), BF16),
                   jax.ShapeDtypeStruct((batch, seq, GROUP_DIM), F32)],
        scratch_shapes=scratch,
        compiler_params=_cparams(("arbitrary",)),
        name=f"attn_d{d}",
    )(bkt, rb, bsum, qge, qgo, kg, u3, u3, u3)


def _merge_ffn_kernel(x_ref, c_ref, o0_ref, o1_ref, o2_ref, l0_ref, l1_ref, l2_ref,
                      gc0_ref, gc1_ref, ga0_ref, ga1_ref, wc_ref, wa_ref, wo_ref,
                      g2_ref, w1_ref, w2_ref, out_ref):
    l0, l1, l2 = l0_ref[...], l1_ref[...], l2_ref[...]
    m = jnp.maximum(jnp.maximum(l0, l1), l2)
    e0, e1, e2 = jnp.exp(l0 - m), jnp.exp(l1 - m), jnp.exp(l2 - m)
    o = (e0 * o0_ref[...].astype(F32) + e1 * o1_ref[...].astype(F32)
         + e2 * o2_ref[...].astype(F32)) / (e0 + e1 + e2)
    y_attn = jnp.dot(o.astype(BF16), wa_ref[...], preferred_element_type=F32)
    y_conv = jnp.dot(c_ref[...], wc_ref[...], preferred_element_type=F32)
    half = D_MODEL // 2
    mix = []
    for hs, gc_ref, ga_ref in ((slice(0, half), gc0_ref, ga0_ref), (slice(half, D_MODEL), gc1_ref, ga1_ref)):
        gc = jax.nn.sigmoid(gc_ref[...].astype(F32))
        ga = jax.nn.sigmoid(ga_ref[...].astype(F32))
        mix.append((gc * y_conv[:, hs] + ga * y_attn[:, hs]).astype(BF16))
    y = (jnp.dot(mix[0], wo_ref[0:half, :], preferred_element_type=F32)
         + jnp.dot(mix[1], wo_ref[half:D_MODEL, :], preferred_element_type=F32))
    x = x_ref[...] + y

    rn = lax.rsqrt(jnp.mean(x * x, axis=-1, keepdims=True) + EPS)
    h = (x * rn * g2_ref[...]).astype(BF16)
    y2 = None
    for f0 in range(0, D_FF, FF_CHUNK):
        a = jnp.dot(h, w1_ref[:, f0:f0 + FF_CHUNK], preferred_element_type=F32)
        a = jnp.square(jnp.maximum(a, 0.0)).astype(BF16)
        t = jnp.dot(a, w2_ref[f0:f0 + FF_CHUNK, :], preferred_element_type=F32)
        y2 = t if y2 is None else y2 + t
    out_ref[...] = x + y2


def _merge_ffn(x2d, c2d, outs, lses, u2d, wc, wa, wo, g2, w1, w2, layer, m_rows):
    tm = MIX_ROWS
    resident = pl.Buffered(1)
    row = lambda width: pl.BlockSpec((tm, width), lambda i: (i, 0))
    gate = lambda tile: pl.BlockSpec((tm, COL_TILE), lambda i: (i, tile))
    wspec = lambda k, n: pl.BlockSpec((None, k, n), lambda i: (layer, 0, 0), pipeline_mode=resident)
    gate0 = 3 * ATTN_DIM // COL_TILE
    return pl.pallas_call(
        _merge_ffn_kernel,
        grid=(m_rows // tm,),
        in_specs=[row(D_MODEL), row(CONV_DIM)] + [row(GROUP_DIM)] * 6
                 + [gate(gate0), gate(gate0 + 1), gate(gate0 + 2), gate(gate0 + 3)]
                 + [wspec(CONV_DIM, D_MODEL), wspec(GROUP_DIM, D_MODEL), wspec(D_MODEL, D_MODEL),
                    pl.BlockSpec((None, 1, D_MODEL), lambda i: (layer, 0, 0)),
                    wspec(D_MODEL, D_FF), wspec(D_FF, D_MODEL)],
        out_specs=row(D_MODEL),
        out_shape=jax.ShapeDtypeStruct((m_rows, D_MODEL), F32),
        compiler_params=_cparams(("arbitrary",)),
        name="merge_ffn",
    )(x2d, c2d, *outs, *lses, u2d, u2d, u2d, u2d, wc, wa, wo, g2, w1, w2)


def _t5_bucket(dist):
    max_exact = NUM_BUCKETS // 2
    nf = jnp.maximum(dist, 1).astype(jnp.float32)
    large = max_exact + (jnp.log(nf / max_exact) / math.log(MAX_REL_DISTANCE / max_exact)
                         * (NUM_BUCKETS - max_exact)).astype(jnp.int32)
    large = jnp.minimum(large, NUM_BUCKETS - 1)
    return jnp.where(dist < max_exact, dist, large)


def _bucket_tile(d):
    qi = jnp.arange(BLOCK)[:, None]
    kj = jnp.arange(2 * BLOCK)[None, :]
    off = qi + BLOCK - kj
    band = (off >= 0) & (off <= SUB_WINDOW)
    bucket = _t5_bucket(jnp.clip(off, 0, SUB_WINDOW) * d)
    return jnp.where(band, bucket, -1).astype(jnp.int32)


def kernel(x, rel_bias, norm1_g, w_in, q_norm_g, k_norm_g, conv_dw_w, conv_dw_b, conv_ln_g, conv_ln_b,
           w_conv_out, w_attn_out, w_out, norm2_g, w_ff1, w_ff2):
    batch, seq, _ = x.shape
    depth = w_in.shape[0]
    m_rows = batch * seq
    assert seq % (BLOCK * max(DILATIONS)) == 0 and x.shape[2] == D_MODEL

    w_in_b, w_conv_b, w_attn_b, w_out_b = (w.astype(BF16) for w in (w_in, w_conv_out, w_attn_out, w_out))
    w_ff1_b, w_ff2_b = w_ff1.astype(BF16), w_ff2.astype(BF16)
    head_of_col = jnp.arange(GROUP_DIM) // HEAD_DIM
    even_head = (head_of_col % 2 == 0).astype(F32)[None, :]
    qg_all = jnp.tile(q_norm_g, (1, HEADS_PER_GROUP)) * (HEAD_DIM ** -0.5)
    qge_all, qgo_all = qg_all * even_head, qg_all * (1.0 - even_head)
    kg_all = jnp.tile(k_norm_g, (1, HEADS_PER_GROUP))
    head_of_half = head_of_col[:GROUP_DIM // 2]
    bsum = ((head_of_half[:, None] == head_of_half[None, :]).astype(F32) * (1.0 / HEAD_DIM)).astype(BF16)
    bkts = [_bucket_tile(d) for d in DILATIONS]
    rbs = [rel_bias[:, g * HEADS_PER_GROUP:(g + 1) * HEADS_PER_GROUP].reshape(-1).astype(F32)
           for g in range(N_GROUPS)]

    vec3 = lambda a: a.reshape(depth, 1, a.shape[-1])
    norm1_g, norm2_g, conv_dw_b, conv_ln_g, conv_ln_b = (vec3(a) for a in (norm1_g, norm2_g, conv_dw_b, conv_ln_g, conv_ln_b))
    x2d = x.reshape(m_rows, D_MODEL)
    for layer in range(depth):
        u2d, c2d = _in_proj(x2d, norm1_g, w_in_b, conv_dw_w, conv_dw_b, conv_ln_g, conv_ln_b, layer, batch, seq)
        u3 = u2d.reshape(batch, seq, U_COLS)
        outs, lses = [], []
        for g in range(N_GROUPS):
            sl = slice(layer, layer + 1)
            o, lse = _attn_group(u3, bkts[g], rbs[g], bsum, qge_all[sl], qgo_all[sl], kg_all[sl], g, batch, seq)
            outs.append(o.reshape(m_rows, GROUP_DIM))
            lses.append(lse.reshape(m_rows, GROUP_DIM))
        x2d = _merge_ffn(x2d, c2d, outs, lses, u2d, w_conv_b, w_attn_b, w_out_b,
                         norm2_g, w_ff1_b, w_ff2_b, layer, m_rows)
    return x2d.reshape(batch, seq, D_MODEL)
```

```python
import functools
import math

import jax
import jax.numpy as jnp
from jax import lax
from jax.experimental import pallas as pl
from jax.experimental.pallas import tpu as pltpu

F32 = jnp.float32
BF16 = jnp.bfloat16

D_MODEL = 1024
CONV_DIM = 512
CONV_WIDTH = 31
N_GROUPS = 3
HEADS_PER_GROUP = 8
HEAD_DIM = 64
GROUP_DIM = HEADS_PER_GROUP * HEAD_DIM
ATTN_DIM = N_GROUPS * GROUP_DIM
DILATIONS = (1, 4, 16)
SUB_WINDOW = 128
BLOCK = 128
NUM_BUCKETS = 32
MAX_REL_DISTANCE = 2048
D_FF = 4 * D_MODEL
EPS = 1e-6
NEG_INF = -1e30
IN_COLS = 2 * CONV_DIM + 3 * ATTN_DIM + 2 * D_MODEL

LANES = 128
SUBLANES = 8
COL_TILE = 512
N_COL_TILES = IN_COLS // COL_TILE
N_CONV_COL_TILES = 2 * CONV_DIM // COL_TILE
U_COLS = IN_COLS - 2 * CONV_DIM
N_SLABS = D_MODEL // LANES
N_PAIRS = HEADS_PER_GROUP // 2
BLOCK_LOOP_UNROLL = {1: 2, 4: 2, 16: 7}
MIX_ROWS = 512
FF_CHUNK = 2048
CONV_HALO = 32
CONV_TILE = 256
CONV_CHUNK = 128
VMEM_LIMIT = 56 * 1024 * 1024


def _cparams(sem):
    return pltpu.CompilerParams(dimension_semantics=sem, vmem_limit_bytes=VMEM_LIMIT)


def _slab(c):
    return slice(c * LANES, (c + 1) * LANES)


def _in_proj_kernel(*refs, seq):
    x_refs = refs[:N_SLABS]
    (g_ref, w_ref, dw_ref, db_ref, lg_ref, lb_ref, u_ref, c_ref,
     h_ref, rn_ref, tmp_ref, uc_ref, z_ref, acc_ref) = refs[N_SLABS:]
    j = pl.program_id(1)
    rc = 128
    quarter = seq // 4

    @pl.when((pl.program_id(0) == 0) & (j == 0))
    def _():
        uc_ref[...] = jnp.zeros(uc_ref.shape, BF16)
        z_ref[...] = jnp.zeros(z_ref.shape, F32)
        acc_ref[...] = jnp.zeros(acc_ref.shape, F32)

    @pl.when(j == 0)
    def _():
        def natural(ci, carry):
            rows = pl.ds(pl.multiple_of(ci * rc, rc), rc)
            xs = [x_refs[c][rows, :] for c in range(N_SLABS)]
            ss = xs[0] * xs[0]
            for c in range(1, N_SLABS):
                ss = ss + xs[c] * xs[c]
            rn = lax.rsqrt(jnp.sum(ss, axis=-1, keepdims=True) * (1.0 / D_MODEL) + EPS)
            rn_ref[rows, :] = jnp.broadcast_to(rn, (rc, LANES))
            for c in range(N_SLABS):
                h_ref[0, rows, _slab(c)] = (xs[c] * rn * g_ref[:, _slab(c)]).astype(BF16)
            return carry

        lax.fori_loop(0, seq // rc, natural, 0, unroll=4)

        def by_four(r4, carry):
            base = pl.multiple_of(r4 * quarter, quarter)
            for a0 in range(0, quarter, rc):
                src = pl.ds(a0 * 4 + r4, rc, stride=4)
                rn = rn_ref[src, :]
                for c in range(N_SLABS):
                    y = x_refs[c][src, :] * rn * g_ref[:, _slab(c)]
                    h_ref[1, pl.ds(base + a0, rc), _slab(c)] = y.astype(BF16)
                    tmp_ref[c, a0:a0 + rc, :] = y
            for r2 in range(4):
                for c in range(N_SLABS):
                    h_ref[2, pl.ds(base + r2 * rc, rc), _slab(c)] = (
                        tmp_ref[c, pl.ds(r2, rc, stride=4), :].astype(BF16))
            return carry

        lax.fori_loop(0, 4, by_four, 0)

    jj = jnp.clip(j - 2, 0, 8)
    sel = jnp.where((j >= 2) & (j <= 10), lax.rem(jj, 3), 0)

    n_iter = CONV_TILE // CONV_CHUNK
    n_chunks = seq // CONV_CHUNK
    dot_rows = seq // n_iter

    def body(i, carry):
        rows = pl.ds(pl.multiple_of(i * dot_rows, dot_rows), dot_rows)
        u_ref[rows, :] = jnp.dot(h_ref[sel, rows, :], w_ref[...], preferred_element_type=F32).astype(BF16)
        tap_chunk = jnp.clip(j - 3, 0, seq // CONV_TILE - 1) * n_iter + i
        glu_chunk = jnp.clip((j - 3) * n_iter + i + 1, 0, n_chunks - 1)
        chunk_rows = pl.ds(pl.multiple_of(i * CONV_CHUNK, CONV_CHUNK), CONV_CHUNK)
        _conv_norm(acc_ref, (j + 1) & 1, chunk_rows, lg_ref, lb_ref, c_ref)
        _conv_taps(z_ref, pl.multiple_of(tap_chunk * CONV_CHUNK, CONV_CHUNK), dw_ref, db_ref,
                   acc_ref, j & 1, chunk_rows)
        _conv_glu(uc_ref, z_ref, pl.multiple_of(glu_chunk * CONV_CHUNK, CONV_CHUNK))
        return carry

    conv_active = (j >= 2) & (j <= seq // CONV_TILE + 3)

    @pl.when(conv_active)
    def _():
        lax.fori_loop(0, n_iter, body, 0)

    @pl.when(jnp.logical_not(conv_active))
    def _():
        u_ref[...] = jnp.dot(h_ref[sel], w_ref[...], preferred_element_type=F32).astype(BF16)

    @pl.when(j == 0)
    def _():
        uc_ref[:, 0:COL_TILE] = u_ref[...]

    @pl.when(j == 1)
    def _():
        uc_ref[:, COL_TILE:2 * COL_TILE] = u_ref[...]


def _conv_glu(uc_ref, z_ref, t0):
    a = uc_ref[pl.ds(t0, CONV_CHUNK), 0:CONV_DIM].astype(F32)
    gt = uc_ref[pl.ds(t0, CONV_CHUNK), CONV_DIM:2 * CONV_DIM].astype(F32)
    z_ref[pl.ds(t0 + CONV_HALO, CONV_CHUNK), :] = a * jax.nn.sigmoid(gt)


def _conv_norm(acc_ref, slot, chunk_rows, lg_ref, lb_ref, c_ref):
    acc = acc_ref[slot, chunk_rows, :]
    mu = jnp.mean(acc, axis=-1, keepdims=True)
    xc = acc - mu
    y = xc * lax.rsqrt(jnp.mean(xc * xc, axis=-1, keepdims=True) + EPS)
    y = y * lg_ref[...] + lb_ref[...]
    c_ref[chunk_rows, :] = (y * jax.nn.sigmoid(y)).astype(BF16)


def _conv_taps(z_ref, t0, w_ref, b_ref, acc_ref, slot, chunk_rows):
    chunk = CONV_CHUNK
    first_tap = CONV_HALO - (CONV_WIDTH - 1)
    n_win = chunk + CONV_HALO
    for lt in range(CONV_DIM // LANES):
        ls = _slab(lt)
        window = z_ref[pl.ds(t0, n_win), ls]
        acc = None
        for rho in range(SUBLANES):
            offs = [o for o in range(first_tap, first_tap + CONV_WIDTH) if o % SUBLANES == rho]
            rolled = window if rho == 0 else pltpu.roll(window, n_win - rho, axis=0)
            terms = [w_ref[o - first_tap:o - first_tap + 1, ls] * rolled[o - rho:o - rho + chunk, :]
                     for o in offs]
            while len(terms) > 1:
                terms = [terms[i] + terms[i + 1] for i in range(0, len(terms) - 1, 2)] + (
                    [terms[-1]] if len(terms) % 2 else [])
            acc = terms[0] if acc is None else acc + terms[0]
        acc_ref[slot, chunk_rows, ls] = acc + b_ref[:, ls]


def _in_proj(x2d, g, w, dw_w, dw_b, ln_g, ln_b, layer, batch, seq):
    assert seq // 16 == 128 and COL_TILE == CONV_DIM
    n_conv_tiles = seq // CONV_TILE
    x_specs = [pl.BlockSpec((seq, LANES),
                            functools.partial(lambda b, j, c: (jnp.minimum(b + jnp.minimum(j, 1), batch - 1), c), c=c))
               for c in range(N_SLABS)]
    vec = lambda: pl.BlockSpec((None, 1, CONV_DIM), lambda b, j: (layer, 0, 0))
    return pl.pallas_call(
        functools.partial(_in_proj_kernel, seq=seq),
        grid=(batch, N_COL_TILES),
        in_specs=x_specs + [
            pl.BlockSpec((None, 1, D_MODEL), lambda b, j: (layer, 0, 0)),
            pl.BlockSpec((None, D_MODEL, COL_TILE), lambda b, j: (layer, 0, j)),
            pl.BlockSpec((None, CONV_WIDTH, CONV_DIM), lambda b, j: (layer, 0, 0)),
            vec(), vec(), vec(),
        ],
        out_specs=[
            pl.BlockSpec((seq, COL_TILE), lambda b, j: (b, jnp.maximum(j - N_CONV_COL_TILES, 0))),
            pl.BlockSpec((CONV_TILE, CONV_DIM),
                         lambda b, j: (b * n_conv_tiles + jnp.clip(j - 4, 0, n_conv_tiles - 1), 0)),
        ],
        out_shape=[jax.ShapeDtypeStruct((batch * seq, U_COLS), BF16),
                   jax.ShapeDtypeStruct((batch * seq, CONV_DIM), BF16)],
        scratch_shapes=[pltpu.VMEM((N_GROUPS, seq, D_MODEL), BF16),
                        pltpu.VMEM((seq, LANES), F32),
                        pltpu.VMEM((N_SLABS, seq // 4, LANES), F32),
                        pltpu.VMEM((seq, 2 * CONV_DIM), BF16),
                        pltpu.VMEM((CONV_HALO + seq, CONV_DIM), F32),
                        pltpu.VMEM((2, CONV_TILE, CONV_DIM), F32)],
        compiler_params=_cparams(("arbitrary", "arbitrary")),
        name="in_proj",
    )(*([x2d] * N_SLABS), g, w, dw_w, dw_b, ln_g, ln_b)


def _attn_kernel(bkt_ref, rb_ref, bsum_ref, qge_ref, qgo_ref, kg_ref, q_ref, k_ref, v_ref,
                 o_ref, lse_ref,
                 bm_ref, qe_ref, qo_ref, kn_ref, va_ref, s_ref, *stage, seq, d):
    sub_len = seq // d
    nb = sub_len // BLOCK

    @pl.when(pl.program_id(0) == 0)
    def _():
        bk = bkt_ref[...]
        for h in range(HEADS_PER_GROUP):
            acc = jnp.full((BLOCK, 2 * BLOCK), NEG_INF, F32)
            for b in range(NUM_BUCKETS):
                acc = jnp.where(bk == b, rb_ref[b * HEADS_PER_GROUP + h], acc)
            bm_ref[h // 2, (h % 2) * BLOCK:(h % 2 + 1) * BLOCK, :] = acc
        va_ref[:, :, LANES:2 * LANES] = jnp.ones((N_PAIRS, seq, LANES), BF16)

    rc = 256

    def head_mean_sq(t):
        sq = (t * t).astype(BF16)
        half = GROUP_DIM // 2
        return jnp.concatenate(
            [jnp.dot(sq[:, 0:half], bsum_ref[...], preferred_element_type=F32),
             jnp.dot(sq[:, half:GROUP_DIM], bsum_ref[...], preferred_element_type=F32)], axis=1)

    def norm_body(ci, carry):
        rows = pl.ds(pl.multiple_of(ci * rc, rc), rc)
        q = q_ref[0, rows, :].astype(F32)
        qr = q * lax.rsqrt(head_mean_sq(q) + EPS)
        qe_ref[rows, :] = (qr * qge_ref[...]).astype(BF16)
        qo_ref[rows, :] = (qr * qgo_ref[...]).astype(BF16)
        k = k_ref[0, rows, :].astype(F32)
        kr = k * lax.rsqrt(head_mean_sq(k) + EPS)
        kn_ref[rows, :] = (kr * kg_ref[...]).astype(BF16)
        for p in range(N_PAIRS):
            va_ref[p, rows, 0:LANES] = v_ref[0, rows, _slab(p)]
        return carry

    lax.fori_loop(0, seq // rc, norm_body, 0, unroll=8)

    lt64 = lax.broadcasted_iota(jnp.int32, (BLOCK, LANES), 1) < HEAD_DIM

    def block_aligned(row):
        return row if isinstance(row, int) else pl.multiple_of(row, BLOCK)

    def key_rows(row0, first):
        return pl.ds(row0, BLOCK) if first else pl.ds(block_aligned(row0 - BLOCK), 2 * BLOCK)

    def score_products(row0, first):
        row0 = block_aligned(row0)
        qrows, krows = pl.ds(row0, BLOCK), key_rows(row0, first)
        out = []
        for p in range(N_PAIRS):
            q2 = jnp.concatenate([qe_ref[qrows, _slab(p)], qo_ref[qrows, _slab(p)]], axis=0)
            out.append(lax.dot_general(q2, kn_ref[krows, _slab(p)], (((1,), (1,)), ((), ())),
                                       preferred_element_type=F32))
        return out

    def width(first):
        return BLOCK if first else 2 * BLOCK

    def park(slot, scores, first):
        for p in range(N_PAIRS):
            s_ref[slot, p, :, 0:width(first)] = scores[p]

    def unpark(slot, first):
        return [s_ref[slot, p, :, 0:width(first)] for p in range(N_PAIRS)]

    def finish_block(scores, row0, t0, first):
        row0 = block_aligned(row0)
        qrows, krows = pl.ds(row0, BLOCK), key_rows(row0, first)
        probs, maxes = [], []
        for p in range(N_PAIRS):
            s = scores[p] + (bm_ref[p, :, BLOCK:2 * BLOCK] if first else bm_ref[p])
            m = jnp.max(s, axis=-1, keepdims=True)
            probs.append(jnp.exp(s - m).astype(BF16))
            maxes.append(m)
        results = [jnp.dot(probs[p], va_ref[p, krows, :], preferred_element_type=F32) for p in range(N_PAIRS)]
        for p in range(N_PAIRS):
            re, ro = results[p][:BLOCK], results[p][BLOCK:]
            me, mo = maxes[p][:BLOCK], maxes[p][BLOCK:]
            denom = jnp.where(lt64, re[:, LANES:], ro[:, LANES:])
            o_pair = jnp.where(lt64, re[:, :LANES], ro[:, :LANES]) / denom
            lse_pair = jnp.where(lt64, me, mo) + jnp.log(denom)
            if d == 1:
                o_ref[0, qrows, _slab(p)] = o_pair.astype(BF16)
                lse_ref[0, qrows, _slab(p)] = lse_pair
            else:
                nat = pl.ds(t0, BLOCK, stride=d)
                stage[0][p, nat, :] = o_pair
                stage[1][p, nat, :] = lse_pair

    def run_chain(base, head_only, t0_of, n_blocks):
        def first(n):
            return n == 0 if head_only else True

        def row(n):
            return base + n * BLOCK

        park(0, score_products(row(0), first(0)), first(0))
        park(1, score_products(row(1), first(1)), first(1))
        finish_block(unpark(0, first(0)), row(0), t0_of(0), first(0))

        def blk_body(n, c):
            slot = n & 1
            prev = unpark(1 - slot, first(1))
            park(slot, score_products(row(n), first(2)), first(2))
            finish_block(prev, row(n - 1), t0_of(n - 1), first(1))
            return c

        lax.fori_loop(2, n_blocks, blk_body, 0, unroll=BLOCK_LOOP_UNROLL[d])
        last = n_blocks - 1
        finish_block(unpark(last & 1, first(last)), row(last), t0_of(last), first(last))

    def residue_of_class(c):
        if isinstance(c, int):
            return c // 4 + 4 * (c % 4)
        return lax.shift_right_logical(c, 2) + 4 * (c & 3)

    if d == 1:
        run_chain(0, True, lambda n: n * BLOCK, nb)
    elif nb > 1:
        def class_body(r, carry):
            run_chain(pl.multiple_of(r * sub_len, BLOCK), True, lambda n: n * (BLOCK * d) + r, nb)
            return carry

        lax.fori_loop(0, d, class_body, 0, unroll=2)
    else:
        run_chain(0, False, residue_of_class, d)

    if d > 1:
        def copy_body(ci, carry):
            rows = pl.ds(pl.multiple_of(ci * rc, rc), rc)
            for p in range(N_PAIRS):
                o_ref[0, rows, _slab(p)] = stage[0][p, rows, :].astype(BF16)
                lse_ref[0, rows, _slab(p)] = stage[1][p, rows, :]
            return carry

        lax.fori_loop(0, seq // rc, copy_body, 0, unroll=4)


def _attn_group(u3, bkt, rb, bsum, qge, qgo, kg, group, batch, seq):
    d = DILATIONS[group]
    assert d in (1, 4, 16) and (seq // d) % BLOCK == 0
    const = lambda shape: pl.BlockSpec(shape, lambda b: (0,) * len(shape))
    col = lambda first_tile: pl.BlockSpec((1, seq, GROUP_DIM), lambda b: (b, 0, first_tile + group))
    scratch = [
        pltpu.VMEM((N_PAIRS, 2 * BLOCK, 2 * BLOCK), F32),
        pltpu.VMEM((seq, GROUP_DIM), BF16),
        pltpu.VMEM((seq, GROUP_DIM), BF16),
        pltpu.VMEM((seq, GROUP_DIM), BF16),
        pltpu.VMEM((N_PAIRS, seq, 2 * LANES), BF16),
        pltpu.VMEM((2, N_PAIRS, 2 * BLOCK, 2 * BLOCK), F32),
    ]
    if d > 1:
        scratch += [pltpu.VMEM((N_PAIRS, seq, LANES), F32), pltpu.VMEM((N_PAIRS, seq, LANES), F32)]
    return pl.pallas_call(
        functools.partial(_attn_kernel, seq=seq, d=d),
        grid=(batch,),
        in_specs=[
            const((BLOCK, 2 * BLOCK)),
            pl.BlockSpec(memory_space=pltpu.SMEM),
            const((GROUP_DIM // 2, GROUP_DIM // 2)),
            const((1, GROUP_DIM)), const((1, GROUP_DIM)), const((1, GROUP_DIM)),
            col(0), col(N_GROUPS), col(2 * N_GROUPS),
        ],
        out_specs=[pl.BlockSpec((1, seq, GROUP_DIM), lambda b: (b, 0, 0))] * 2,
        out_shape=[jax.ShapeDtypeStruct((batch, seq, GROUP_DIM), BF16),
                   jax.ShapeDtypeStruct((batch, seq, GROUP_DIM), F32)],
        scratch_shapes=scratch,
        compiler_params=_cparams(("arbitrary",)),
        name=f"attn_d{d}",
    )(bkt, rb, bsum, qge, qgo, kg, u3, u3, u3)


def _merge_ffn_kernel(x_ref, c_ref, o0_ref, o1_ref, o2_ref, l0_ref, l1_ref, l2_ref,
                      gc0_ref, gc1_ref, ga0_ref, ga1_ref, wc_ref, wa_ref, wo_ref,
                      g2_ref, w1_ref, w2_ref, out_ref):
    l0, l1, l2 = l0_ref[...], l1_ref[...], l2_ref[...]
    m = jnp.maximum(jnp.maximum(l0, l1), l2)
    e0, e1, e2 = jnp.exp(l0 - m), jnp.exp(l1 - m), jnp.exp(l2 - m)
    o = (e0 * o0_ref[...].astype(F32) + e1 * o1_ref[...].astype(F32)
         + e2 * o2_ref[...].astype(F32)) / (e0 + e1 + e2)
    y_attn = jnp.dot(o.astype(BF16), wa_ref[...], preferred_element_type=F32)
    y_conv = jnp.dot(c_ref[...], wc_ref[...], preferred_element_type=F32)
    half = D_MODEL // 2
    mix = []
    for hs, gc_ref, ga_ref in ((slice(0, half), gc0_ref, ga0_ref), (slice(half, D_MODEL), gc1_ref, ga1_ref)):
        gc = jax.nn.sigmoid(gc_ref[...].astype(F32))
        ga = jax.nn.sigmoid(ga_ref[...].astype(F32))
        mix.append((gc * y_conv[:, hs] + ga * y_attn[:, hs]).astype(BF16))
    y = (jnp.dot(mix[0], wo_ref[0:half, :], preferred_element_type=F32)
         + jnp.dot(mix[1], wo_ref[half:D_MODEL, :], preferred_element_type=F32))
    x = x_ref[...] + y

    rn = lax.rsqrt(jnp.mean(x * x, axis=-1, keepdims=True) + EPS)
    h = (x * rn * g2_ref[...]).astype(BF16)
    y2 = None
    for f0 in range(0, D_FF, FF_CHUNK):
        a = jnp.dot(h, w1_ref[:, f0:f0 + FF_CHUNK], preferred_element_type=F32)
        a = jnp.square(jnp.maximum(a, 0.0)).astype(BF16)
        t = jnp.dot(a, w2_ref[f0:f0 + FF_CHUNK, :], preferred_element_type=F32)
        y2 = t if y2 is None else y2 + t
    out_ref[...] = x + y2


def _merge_ffn(x2d, c2d, outs, lses, u2d, wc, wa, wo, g2, w1, w2, layer, m_rows):
    tm = MIX_ROWS
    resident = pl.Buffered(1)
    row = lambda width: pl.BlockSpec((tm, width), lambda i: (i, 0))
    gate = lambda tile: pl.BlockSpec((tm, COL_TILE), lambda i: (i, tile))
    wspec = lambda k, n: pl.BlockSpec((None, k, n), lambda i: (layer, 0, 0), pipeline_mode=resident)
    gate0 = 3 * ATTN_DIM // COL_TILE
    return pl.pallas_call(
        _merge_ffn_kernel,
        grid=(m_rows // tm,),
        in_specs=[row(D_MODEL), row(CONV_DIM)] + [row(GROUP_DIM)] * 6
                 + [gate(gate0), gate(gate0 + 1), gate(gate0 + 2), gate(gate0 + 3)]
                 + [wspec(CONV_DIM, D_MODEL), wspec(GROUP_DIM, D_MODEL), wspec(D_MODEL, D_MODEL),
                    pl.BlockSpec((None, 1, D_MODEL), lambda i: (layer, 0, 0)),
                    wspec(D_MODEL, D_FF), wspec(D_FF, D_MODEL)],
        out_specs=row(D_MODEL),
        out_shape=jax.ShapeDtypeStruct((m_rows, D_MODEL), F32),
        compiler_params=_cparams(("arbitrary",)),
        name="merge_ffn",
    )(x2d, c2d, *outs, *lses, u2d, u2d, u2d, u2d, wc, wa, wo, g2, w1, w2)


def _t5_bucket(dist):
    max_exact = NUM_BUCKETS // 2
    nf = jnp.maximum(dist, 1).astype(jnp.float32)
    large = max_exact + (jnp.log(nf / max_exact) / math.log(MAX_REL_DISTANCE / max_exact)
                         * (NUM_BUCKETS - max_exact)).astype(jnp.int32)
    large = jnp.minimum(large, NUM_BUCKETS - 1)
    return jnp.where(dist < max_exact, dist, large)


def _bucket_tile(d):
    qi = jnp.arange(BLOCK)[:, None]
    kj = jnp.arange(2 * BLOCK)[None, :]
    off = qi + BLOCK - kj
    band = (off >= 0) & (off <= SUB_WINDOW)
    bucket = _t5_bucket(jnp.clip(off, 0, SUB_WINDOW) * d)
    return jnp.where(band, bucket, -1).astype(jnp.int32)


def kernel(x, rel_bias, norm1_g, w_in, q_norm_g, k_norm_g, conv_dw_w, conv_dw_b, conv_ln_g, conv_ln_b,
           w_conv_out, w_attn_out, w_out, norm2_g, w_ff1, w_ff2):
    batch, seq, _ = x.shape
    depth = w_in.shape[0]
    m_rows = batch * seq
    assert seq % (BLOCK * max(DILATIONS)) == 0 and x.shape[2] == D_MODEL

    w_in_b, w_conv_b, w_attn_b, w_out_b = (w.astype(BF16) for w in (w_in, w_conv_out, w_attn_out, w_out))
    w_ff1_b, w_ff2_b = w_ff1.astype(BF16), w_ff2.astype(BF16)
    head_of_col = jnp.arange(GROUP_DIM) // HEAD_DIM
    even_head = (head_of_col % 2 == 0).astype(F32)[None, :]
    qg_all = jnp.tile(q_norm_g, (1, HEADS_PER_GROUP)) * (HEAD_DIM ** -0.5)
    qge_all, qgo_all = qg_all * even_head, qg_all * (1.0 - even_head)
    kg_all = jnp.tile(k_norm_g, (1, HEADS_PER_GROUP))
    head_of_half = head_of_col[:GROUP_DIM // 2]
    bsum = ((head_of_half[:, None] == head_of_half[None, :]).astype(F32) * (1.0 / HEAD_DIM)).astype(BF16)
    bkts = [_bucket_tile(d) for d in DILATIONS]
    rbs = [rel_bias[:, g * HEADS_PER_GROUP:(g + 1) * HEADS_PER_GROUP].reshape(-1).astype(F32)
           for g in range(N_GROUPS)]

    vec3 = lambda a: a.reshape(depth, 1, a.shape[-1])
    norm1_g, norm2_g, conv_dw_b, conv_ln_g, conv_ln_b = (vec3(a) for a in (norm1_g, norm2_g, conv_dw_b, conv_ln_g, conv_ln_b))
    x2d = x.reshape(m_rows, D_MODEL)
    for layer in range(depth):
        u2d, c2d = _in_proj(x2d, norm1_g, w_in_b, conv_dw_w, conv_dw_b, conv_ln_g, conv_ln_b, layer, batch, seq)
        u3 = u2d.reshape(batch, seq, U_COLS)
        outs, lses = [], []
        for g in range(N_GROUPS):
            sl = slice(layer, layer + 1)
            o, lse = _attn_group(u3, bkts[g], rbs[g], bsum, qge_all[sl], qgo_all[sl], kg_all[sl], g, batch, seq)
            outs.append(o.reshape(m_rows, GROUP_DIM))
            lses.append(lse.reshape(m_rows, GROUP_DIM))
        x2d = _merge_ffn(x2d, c2d, outs, lses, u2d, w_conv_b, w_attn_b, w_out_b,
                         norm2_g, w_ff1_b, w_ff2_b, layer, m_rows)
    return x2d.reshape(batch, seq, D_MODEL)
```

```python
import functools
import math

import jax
import jax.numpy as jnp
from jax import lax
from jax.experimental import pallas as pl
from jax.experimental.pallas import tpu as pltpu

F32 = jnp.float32
BF16 = jnp.bfloat16

D_MODEL = 1024
CONV_DIM = 512
CONV_WIDTH = 31
N_GROUPS = 3
HEADS_PER_GROUP = 8
HEAD_DIM = 64
GROUP_DIM = HEADS_PER_GROUP * HEAD_DIM
ATTN_DIM = N_GROUPS * GROUP_DIM
DILATIONS = (1, 4, 16)
SUB_WINDOW = 128
BLOCK = 128
NUM_BUCKETS = 32
MAX_REL_DISTANCE = 2048
D_FF = 4 * D_MODEL
EPS = 1e-6
NEG_INF = -1e30
IN_COLS = 2 * CONV_DIM + 3 * ATTN_DIM + 2 * D_MODEL

LANES = 128
SUBLANES = 8
COL_TILE = 512
N_COL_TILES = IN_COLS // COL_TILE
N_CONV_COL_TILES = 2 * CONV_DIM // COL_TILE
U_COLS = IN_COLS - 2 * CONV_DIM
N_SLABS = D_MODEL // LANES
N_PAIRS = HEADS_PER_GROUP // 2
BLOCK_LOOP_UNROLL = {1: 2, 4: 2, 16: 7}
D1_SEGMENTS = 4
MIX_ROWS = 512
FF_CHUNK = 2048
CONV_HALO = 32
CONV_TILE = 256
CONV_CHUNK = 128
VMEM_LIMIT = 56 * 1024 * 1024


def _cparams(sem):
    return pltpu.CompilerParams(dimension_semantics=sem, vmem_limit_bytes=VMEM_LIMIT)


def _slab(c):
    return slice(c * LANES, (c + 1) * LANES)


def _in_proj_kernel(*refs, seq):
    x_refs = refs[:N_SLABS]
    (g_ref, w_ref, dw_ref, db_ref, lg_ref, lb_ref, u_ref, c_ref,
     h_ref, rn_ref, tmp_ref, uc_ref, z_ref, acc_ref) = refs[N_SLABS:]
    j = pl.program_id(1)
    rc = 128
    quarter = seq // 4

    @pl.when((pl.program_id(0) == 0) & (j == 0))
    def _():
        uc_ref[...] = jnp.zeros(uc_ref.shape, BF16)
        z_ref[...] = jnp.zeros(z_ref.shape, F32)
        acc_ref[...] = jnp.zeros(acc_ref.shape, F32)

    @pl.when(j == 0)
    def _():
        def natural(ci, carry):
            rows = pl.ds(pl.multiple_of(ci * rc, rc), rc)
            xs = [x_refs[c][rows, :] for c in range(N_SLABS)]
            ss = xs[0] * xs[0]
            for c in range(1, N_SLABS):
                ss = ss + xs[c] * xs[c]
            rn = lax.rsqrt(jnp.sum(ss, axis=-1, keepdims=True) * (1.0 / D_MODEL) + EPS)
            rn_ref[rows, :] = jnp.broadcast_to(rn, (rc, LANES))
            for c in range(N_SLABS):
                h_ref[0, rows, _slab(c)] = (xs[c] * rn * g_ref[:, _slab(c)]).astype(BF16)
            return carry

        lax.fori_loop(0, seq // rc, natural, 0, unroll=4)

        def by_four(r4, carry):
            base = pl.multiple_of(r4 * quarter, quarter)
            for a0 in range(0, quarter, rc):
                src = pl.ds(a0 * 4 + r4, rc, stride=4)
                rn = rn_ref[src, :]
                for c in range(N_SLABS):
                    y = x_refs[c][src, :] * rn * g_ref[:, _slab(c)]
                    h_ref[1, pl.ds(base + a0, rc), _slab(c)] = y.astype(BF16)
                    tmp_ref[c, a0:a0 + rc, :] = y
            for r2 in range(4):
                for c in range(N_SLABS):
                    h_ref[2, pl.ds(base + r2 * rc, rc), _slab(c)] = (
                        tmp_ref[c, pl.ds(r2, rc, stride=4), :].astype(BF16))
            return carry

        lax.fori_loop(0, 4, by_four, 0)

    jj = jnp.clip(j - 2, 0, 8)
    sel = jnp.where((j >= 2) & (j <= 10), lax.rem(jj, 3), 0)

    n_iter = CONV_TILE // CONV_CHUNK
    n_chunks = seq // CONV_CHUNK
    dot_rows = seq // n_iter

    def body(i, carry):
        rows = pl.ds(pl.multiple_of(i * dot_rows, dot_rows), dot_rows)
        u_ref[rows, :] = jnp.dot(h_ref[sel, rows, :], w_ref[...], preferred_element_type=F32).astype(BF16)
        tap_chunk = jnp.clip(j - 3, 0, seq // CONV_TILE - 1) * n_iter + i
        glu_chunk = jnp.clip((j - 3) * n_iter + i + 1, 0, n_chunks - 1)
        chunk_rows = pl.ds(pl.multiple_of(i * CONV_CHUNK, CONV_CHUNK), CONV_CHUNK)
        _conv_norm(acc_ref, (j + 1) & 1, chunk_rows, lg_ref, lb_ref, c_ref)
        _conv_taps(z_ref, pl.multiple_of(tap_chunk * CONV_CHUNK, CONV_CHUNK), dw_ref, db_ref,
                   acc_ref, j & 1, chunk_rows)
        _conv_glu(uc_ref, z_ref, pl.multiple_of(glu_chunk * CONV_CHUNK, CONV_CHUNK))
        return carry

    conv_active = (j >= 2) & (j <= seq // CONV_TILE + 3)

    @pl.when(conv_active)
    def _():
        lax.fori_loop(0, n_iter, body, 0)

    @pl.when(jnp.logical_not(conv_active))
    def _():
        u_ref[...] = jnp.dot(h_ref[sel], w_ref[...], preferred_element_type=F32).astype(BF16)

    @pl.when(j == 0)
    def _():
        uc_ref[:, 0:COL_TILE] = u_ref[...]

    @pl.when(j == 1)
    def _():
        uc_ref[:, COL_TILE:2 * COL_TILE] = u_ref[...]


def _conv_glu(uc_ref, z_ref, t0):
    a = uc_ref[pl.ds(t0, CONV_CHUNK), 0:CONV_DIM].astype(F32)
    gt = uc_ref[pl.ds(t0, CONV_CHUNK), CONV_DIM:2 * CONV_DIM].astype(F32)
    z_ref[pl.ds(t0 + CONV_HALO, CONV_CHUNK), :] = a * jax.nn.sigmoid(gt)


def _conv_norm(acc_ref, slot, chunk_rows, lg_ref, lb_ref, c_ref):
    acc = acc_ref[slot, chunk_rows, :]
    mu = jnp.mean(acc, axis=-1, keepdims=True)
    xc = acc - mu
    y = xc * lax.rsqrt(jnp.mean(xc * xc, axis=-1, keepdims=True) + EPS)
    y = y * lg_ref[...] + lb_ref[...]
    c_ref[chunk_rows, :] = (y * jax.nn.sigmoid(y)).astype(BF16)


def _conv_taps(z_ref, t0, w_ref, b_ref, acc_ref, slot, chunk_rows):
    chunk = CONV_CHUNK
    first_tap = CONV_HALO - (CONV_WIDTH - 1)
    n_win = chunk + CONV_HALO
    for lt in range(CONV_DIM // LANES):
        ls = _slab(lt)
        window = z_ref[pl.ds(t0, n_win), ls]
        acc = None
        for rho in range(SUBLANES):
            offs = [o for o in range(first_tap, first_tap + CONV_WIDTH) if o % SUBLANES == rho]
            rolled = window if rho == 0 else pltpu.roll(window, n_win - rho, axis=0)
            terms = [w_ref[o - first_tap:o - first_tap + 1, ls] * rolled[o - rho:o - rho + chunk, :]
                     for o in offs]
            while len(terms) > 1:
                terms = [terms[i] + terms[i + 1] for i in range(0, len(terms) - 1, 2)] + (
                    [terms[-1]] if len(terms) % 2 else [])
            acc = terms[0] if acc is None else acc + terms[0]
        acc_ref[slot, chunk_rows, ls] = acc + b_ref[:, ls]


def _in_proj(x2d, g, w, dw_w, dw_b, ln_g, ln_b, layer, batch, seq):
    assert seq // 16 == 128 and COL_TILE == CONV_DIM
    n_conv_tiles = seq // CONV_TILE
    x_specs = [pl.BlockSpec((seq, LANES),
                            functools.partial(lambda b, j, c: (jnp.minimum(b + jnp.minimum(j, 1), batch - 1), c), c=c))
               for c in range(N_SLABS)]
    vec = lambda: pl.BlockSpec((None, 1, CONV_DIM), lambda b, j: (layer, 0, 0))
    return pl.pallas_call(
        functools.partial(_in_proj_kernel, seq=seq),
        grid=(batch, N_COL_TILES),
        in_specs=x_specs + [
            pl.BlockSpec((None, 1, D_MODEL), lambda b, j: (layer, 0, 0)),
            pl.BlockSpec((None, D_MODEL, COL_TILE), lambda b, j: (layer, 0, j)),
            pl.BlockSpec((None, CONV_WIDTH, CONV_DIM), lambda b, j: (layer, 0, 0)),
            vec(), vec(), vec(),
        ],
        out_specs=[
            pl.BlockSpec((seq, COL_TILE), lambda b, j: (b, jnp.maximum(j - N_CONV_COL_TILES, 0))),
            pl.BlockSpec((CONV_TILE, CONV_DIM),
                         lambda b, j: (b * n_conv_tiles + jnp.clip(j - 4, 0, n_conv_tiles - 1), 0)),
        ],
        out_shape=[jax.ShapeDtypeStruct((batch * seq, U_COLS), BF16),
                   jax.ShapeDtypeStruct((batch * seq, CONV_DIM), BF16)],
        scratch_shapes=[pltpu.VMEM((N_GROUPS, seq, D_MODEL), BF16),
                        pltpu.VMEM((seq, LANES), F32),
                        pltpu.VMEM((N_SLABS, seq // 4, LANES), F32),
                        pltpu.VMEM((seq, 2 * CONV_DIM), BF16),
                        pltpu.VMEM((CONV_HALO + seq, CONV_DIM), F32),
                        pltpu.VMEM((2, CONV_TILE, CONV_DIM), F32)],
        compiler_params=_cparams(("arbitrary", "arbitrary")),
        name="in_proj",
    )(*([x2d] * N_SLABS), g, w, dw_w, dw_b, ln_g, ln_b)


def _attn_kernel(bkt_ref, rb_ref, bsum_ref, qge_ref, qgo_ref, kg_ref, q_ref, k_ref, v_ref,
                 o_ref, lse_ref,
                 bm_ref, qe_ref, qo_ref, kn_ref, va_ref, s_ref, *stage, seq, d):
    sub_len = seq // d
    nb = sub_len // BLOCK

    @pl.when(pl.program_id(0) == 0)
    def _():
        bk = bkt_ref[...]
        for h in range(HEADS_PER_GROUP):
            acc = jnp.full((BLOCK, 2 * BLOCK), NEG_INF, F32)
            for b in range(NUM_BUCKETS):
                acc = jnp.where(bk == b, rb_ref[b * HEADS_PER_GROUP + h], acc)
            bm_ref[h // 2, (h % 2) * BLOCK:(h % 2 + 1) * BLOCK, :] = acc
        va_ref[:, :, LANES:2 * LANES] = jnp.ones((N_PAIRS, seq, LANES), BF16)

    rc = 256

    def head_mean_sq(t):
        sq = (t * t).astype(BF16)
        half = GROUP_DIM // 2
        return jnp.concatenate(
            [jnp.dot(sq[:, 0:half], bsum_ref[...], preferred_element_type=F32),
             jnp.dot(sq[:, half:GROUP_DIM], bsum_ref[...], preferred_element_type=F32)], axis=1)

    def norm_body(ci, carry):
        rows = pl.ds(pl.multiple_of(ci * rc, rc), rc)
        q = q_ref[0, rows, :].astype(F32)
        qr = q * lax.rsqrt(head_mean_sq(q) + EPS)
        qe_ref[rows, :] = (qr * qge_ref[...]).astype(BF16)
        qo_ref[rows, :] = (qr * qgo_ref[...]).astype(BF16)
        k = k_ref[0, rows, :].astype(F32)
        kr = k * lax.rsqrt(head_mean_sq(k) + EPS)
        kn_ref[rows, :] = (kr * kg_ref[...]).astype(BF16)
        for p in range(N_PAIRS):
            va_ref[p, rows, 0:LANES] = v_ref[0, rows, _slab(p)]
        return carry

    lax.fori_loop(0, seq // rc, norm_body, 0, unroll=8)

    lt64 = lax.broadcasted_iota(jnp.int32, (BLOCK, LANES), 1) < HEAD_DIM

    def block_aligned(row):
        return row if isinstance(row, int) else pl.multiple_of(row, BLOCK)

    def key_rows(row0, first):
        return pl.ds(row0, BLOCK) if first else pl.ds(block_aligned(row0 - BLOCK), 2 * BLOCK)

    def score_products(row0, first):
        row0 = block_aligned(row0)
        qrows, krows = pl.ds(row0, BLOCK), key_rows(row0, first)
        out = []
        for p in range(N_PAIRS):
            q2 = jnp.concatenate([qe_ref[qrows, _slab(p)], qo_ref[qrows, _slab(p)]], axis=0)
            out.append(lax.dot_general(q2, kn_ref[krows, _slab(p)], (((1,), (1,)), ((), ())),
                                       preferred_element_type=F32))
        return out

    def width(first):
        return BLOCK if first else 2 * BLOCK

    def park(slot, scores, first):
        for p in range(N_PAIRS):
            s_ref[slot, p, :, 0:width(first)] = scores[p]

    def unpark(slot, first):
        return [s_ref[slot, p, :, 0:width(first)] for p in range(N_PAIRS)]

    def finish_block(scores, row0, t0, first):
        row0 = block_aligned(row0)
        qrows, krows = pl.ds(row0, BLOCK), key_rows(row0, first)
        probs, maxes = [], []
        for p in range(N_PAIRS):
            s = scores[p] + (bm_ref[p, :, BLOCK:2 * BLOCK] if first else bm_ref[p])
            m = jnp.max(s, axis=-1, keepdims=True)
            probs.append(jnp.exp(s - m).astype(BF16))
            maxes.append(m)
        results = [jnp.dot(probs[p], va_ref[p, krows, :], preferred_element_type=F32) for p in range(N_PAIRS)]
        for p in range(N_PAIRS):
            re, ro = results[p][:BLOCK], results[p][BLOCK:]
            me, mo = maxes[p][:BLOCK], maxes[p][BLOCK:]
            denom = jnp.where(lt64, re[:, LANES:], ro[:, LANES:])
            o_pair = jnp.where(lt64, re[:, :LANES], ro[:, :LANES]) / denom
            lse_pair = jnp.where(lt64, me, mo) + jnp.log(denom)
            if d == 1:
                o_ref[0, qrows, _slab(p)] = o_pair.astype(BF16)
                lse_ref[0, qrows, _slab(p)] = lse_pair
            else:
                nat = pl.ds(t0, BLOCK, stride=d)
                stage[0][p, nat, :] = o_pair
                stage[1][p, nat, :] = lse_pair

    def run_chain(base, mode, t0_of, n_blocks):
        def first(n):
            return mode == "all" or (mode == "head" and n == 0)

        def row(n):
            return base + n * BLOCK

        park(0, score_products(row(0), first(0)), first(0))
        park(1, score_products(row(1), first(1)), first(1))
        finish_block(unpark(0, first(0)), row(0), t0_of(0), first(0))

        def blk_body(n, c):
            slot = n & 1
            prev = unpark(1 - slot, first(1))
            park(slot, score_products(row(n), first(2)), first(2))
            finish_block(prev, row(n - 1), t0_of(n - 1), first(1))
            return c

        lax.fori_loop(2, n_blocks, blk_body, 0, unroll=BLOCK_LOOP_UNROLL[d])
        last = n_blocks - 1
        finish_block(unpark(last & 1, first(last)), row(last), t0_of(last), first(last))

    def residue_of_class(c):
        if isinstance(c, int):
            return c // 4 + 4 * (c % 4)
        return lax.shift_right_logical(c, 2) + 4 * (c & 3)

    if d == 1:
        seg_blocks = nb // D1_SEGMENTS
        run_chain(0, "head", lambda n: n * BLOCK, seg_blocks)

        def segment_body(sg, carry):
            base = pl.multiple_of(sg * (seg_blocks * BLOCK), BLOCK)
            run_chain(base, "none", lambda n: base + n * BLOCK, seg_blocks)
            return carry

        lax.fori_loop(1, D1_SEGMENTS, segment_body, 0)
    elif nb > 1:
        def class_body(r, carry):
            run_chain(pl.multiple_of(r * sub_len, BLOCK), "head", lambda n: n * (BLOCK * d) + r, nb)
            return carry

        lax.fori_loop(0, d, class_body, 0, unroll=2)
    else:
        def class_group(cg, carry):
            run_chain(pl.multiple_of(cg * (4 * BLOCK), BLOCK), "all", lambda n: residue_of_class(cg * 4 + n), 4)
            return carry

        lax.fori_loop(0, d // 4, class_group, 0)

    if d > 1:
        def copy_body(ci, carry):
            rows = pl.ds(pl.multiple_of(ci * rc, rc), rc)
            for p in range(N_PAIRS):
                o_ref[0, rows, _slab(p)] = stage[0][p, rows, :].astype(BF16)
                lse_ref[0, rows, _slab(p)] = stage[1][p, rows, :]
            return carry

        lax.fori_loop(0, seq // rc, copy_body, 0, unroll=4)


def _attn_group(u3, bkt, rb, bsum, qge, qgo, kg, group, batch, seq):
    d = DILATIONS[group]
    assert d in (1, 4, 16) and (seq // d) % BLOCK == 0
    const = lambda shape: pl.BlockSpec(shape, lambda b: (0,) * len(shape))
    col = lambda first_tile: pl.BlockSpec((1, seq, GROUP_DIM), lambda b: (b, 0, first_tile + group))
    scratch = [
        pltpu.VMEM((N_PAIRS, 2 * BLOCK, 2 * BLOCK), F32),
        pltpu.VMEM((seq, GROUP_DIM), BF16),
        pltpu.VMEM((seq, GROUP_DIM), BF16),
        pltpu.VMEM((seq, GROUP_DIM), BF16),
        pltpu.VMEM((N_PAIRS, seq, 2 * LANES), BF16),
        pltpu.VMEM((2, N_PAIRS, 2 * BLOCK, 2 * BLOCK), F32),
    ]
    if d > 1:
        scratch += [pltpu.VMEM((N_PAIRS, seq, LANES), F32), pltpu.VMEM((N_PAIRS, seq, LANES), F32)]
    return pl.pallas_call(
        functools.partial(_attn_kernel, seq=seq, d=d),
        grid=(batch,),
        in_specs=[
            const((BLOCK, 2 * BLOCK)),
            pl.BlockSpec(memory_space=pltpu.SMEM),
            const((GROUP_DIM // 2, GROUP_DIM // 2)),
            const((1, GROUP_DIM)), const((1, GROUP_DIM)), const((1, GROUP_DIM)),
            col(0), col(N_GROUPS), col(2 * N_GROUPS),
        ],
        out_specs=[pl.BlockSpec((1, seq, GROUP_DIM), lambda b: (b, 0, 0))] * 2,
        out_shape=[jax.ShapeDtypeStruct((batch, seq, GROUP_DIM), BF16),
                   jax.ShapeDtypeStruct((batch, seq, GROUP_DIM), F32)],
        scratch_shapes=scratch,
        compiler_params=_cparams(("arbitrary",)),
        name=f"attn_d{d}",
    )(bkt, rb, bsum, qge, qgo, kg, u3, u3, u3)


def _merge_ffn_kernel(x_ref, c_ref, o0_ref, o1_ref, o2_ref, l0_ref, l1_ref, l2_ref,
                      gc0_ref, gc1_ref, ga0_ref, ga1_ref, wc_ref, wa_ref, wo_ref,
                      g2_ref, w1_ref, w2_ref, out_ref):
    l0, l1, l2 = l0_ref[...], l1_ref[...], l2_ref[...]
    m = jnp.maximum(jnp.maximum(l0, l1), l2)
    e0, e1, e2 = jnp.exp(l0 - m), jnp.exp(l1 - m), jnp.exp(l2 - m)
    o = (e0 * o0_ref[...].astype(F32) + e1 * o1_ref[...].astype(F32)
         + e2 * o2_ref[...].astype(F32)) / (e0 + e1 + e2)
    y_attn = jnp.dot(o.astype(BF16), wa_ref[...], preferred_element_type=F32)
    y_conv = jnp.dot(c_ref[...], wc_ref[...], preferred_element_type=F32)
    half = D_MODEL // 2
    mix = []
    for hs, gc_ref, ga_ref in ((slice(0, half), gc0_ref, ga0_ref), (slice(half, D_MODEL), gc1_ref, ga1_ref)):
        gc = jax.nn.sigmoid(gc_ref[...].astype(F32))
        ga = jax.nn.sigmoid(ga_ref[...].astype(F32))
        mix.append((gc * y_conv[:, hs] + ga * y_attn[:, hs]).astype(BF16))
    y = (jnp.dot(mix[0], wo_ref[0:half, :], preferred_element_type=F32)
         + jnp.dot(mix[1], wo_ref[half:D_MODEL, :], preferred_element_type=F32))
    x = x_ref[...] + y

    rn = lax.rsqrt(jnp.mean(x * x, axis=-1, keepdims=True) + EPS)
    h = (x * rn * g2_ref[...]).astype(BF16)
    y2 = None
    for f0 in range(0, D_FF, FF_CHUNK):
        a = jnp.dot(h, w1_ref[:, f0:f0 + FF_CHUNK], preferred_element_type=F32)
        a = jnp.square(jnp.maximum(a, 0.0)).astype(BF16)
        t = jnp.dot(a, w2_ref[f0:f0 + FF_CHUNK, :], preferred_element_type=F32)
        y2 = t if y2 is None else y2 + t
    out_ref[...] = x + y2


def _merge_ffn(x2d, c2d, outs, lses, u2d, wc, wa, wo, g2, w1, w2, layer, m_rows):
    tm = MIX_ROWS
    resident = pl.Buffered(1)
    row = lambda width: pl.BlockSpec((tm, width), lambda i: (i, 0))
    gate = lambda tile: pl.BlockSpec((tm, COL_TILE), lambda i: (i, tile))
    wspec = lambda k, n: pl.BlockSpec((None, k, n), lambda i: (layer, 0, 0), pipeline_mode=resident)
    gate0 = 3 * ATTN_DIM // COL_TILE
    return pl.pallas_call(
        _merge_ffn_kernel,
        grid=(m_rows // tm,),
        in_specs=[row(D_MODEL), row(CONV_DIM)] + [row(GROUP_DIM)] * 6
                 + [gate(gate0), gate(gate0 + 1), gate(gate0 + 2), gate(gate0 + 3)]
                 + [wspec(CONV_DIM, D_MODEL), wspec(GROUP_DIM, D_MODEL), wspec(D_MODEL, D_MODEL),
                    pl.BlockSpec((None, 1, D_MODEL), lambda i: (layer, 0, 0)),
                    wspec(D_MODEL, D_FF), wspec(D_FF, D_MODEL)],
        out_specs=row(D_MODEL),
        out_shape=jax.ShapeDtypeStruct((m_rows, D_MODEL), F32),
        compiler_params=_cparams(("arbitrary",)),
        name="merge_ffn",
    )(x2d, c2d, *outs, *lses, u2d, u2d, u2d, u2d, wc, wa, wo, g2, w1, w2)


def _t5_bucket(dist):
    max_exact = NUM_BUCKETS // 2
    nf = jnp.maximum(dist, 1).astype(jnp.float32)
    large = max_exact + (jnp.log(nf / max_exact) / math.log(MAX_REL_DISTANCE / max_exact)
                         * (NUM_BUCKETS - max_exact)).astype(jnp.int32)
    large = jnp.minimum(large, NUM_BUCKETS - 1)
    return jnp.where(dist < max_exact, dist, large)


def _bucket_tile(d):
    qi = jnp.arange(BLOCK)[:, None]
    kj = jnp.arange(2 * BLOCK)[None, :]
    off = qi + BLOCK - kj
    band = (off >= 0) & (off <= SUB_WINDOW)
    bucket = _t5_bucket(jnp.clip(off, 0, SUB_WINDOW) * d)
    return jnp.where(band, bucket, -1).astype(jnp.int32)


def kernel(x, rel_bias, norm1_g, w_in, q_norm_g, k_norm_g, conv_dw_w, conv_dw_b, conv_ln_g, conv_ln_b,
           w_conv_out, w_attn_out, w_out, norm2_g, w_ff1, w_ff2):
    batch, seq, _ = x.shape
    depth = w_in.shape[0]
    m_rows = batch * seq
    assert seq % (BLOCK * max(DILATIONS)) == 0 and x.shape[2] == D_MODEL

    w_in_b, w_conv_b, w_attn_b, w_out_b = (w.astype(BF16) for w in (w_in, w_conv_out, w_attn_out, w_out))
    w_ff1_b, w_ff2_b = w_ff1.astype(BF16), w_ff2.astype(BF16)
    head_of_col = jnp.arange(GROUP_DIM) // HEAD_DIM
    even_head = (head_of_col % 2 == 0).astype(F32)[None, :]
    qg_all = jnp.tile(q_norm_g, (1, HEADS_PER_GROUP)) * (HEAD_DIM ** -0.5)
    qge_all, qgo_all = qg_all * even_head, qg_all * (1.0 - even_head)
    kg_all = jnp.tile(k_norm_g, (1, HEADS_PER_GROUP))
    head_of_half = head_of_col[:GROUP_DIM // 2]
    bsum = ((head_of_half[:, None] == head_of_half[None, :]).astype(F32) * (1.0 / HEAD_DIM)).astype(BF16)
    bkts = [_bucket_tile(d) for d in DILATIONS]
    rbs = [rel_bias[:, g * HEADS_PER_GROUP:(g + 1) * HEADS_PER_GROUP].reshape(-1).astype(F32)
           for g in range(N_GROUPS)]

    vec3 = lambda a: a.reshape(depth, 1, a.shape[-1])
    norm1_g, norm2_g, conv_dw_b, conv_ln_g, conv_ln_b = (vec3(a) for a in (norm1_g, norm2_g, conv_dw_b, conv_ln_g, conv_ln_b))
    x2d = x.reshape(m_rows, D_MODEL)
    for layer in range(depth):
        u2d, c2d = _in_proj(x2d, norm1_g, w_in_b, conv_dw_w, conv_dw_b, conv_ln_g, conv_ln_b, layer, batch, seq)
        u3 = u2d.reshape(batch, seq, U_COLS)
        outs, lses = [], []
        for g in range(N_GROUPS):
            sl = slice(layer, layer + 1)
            o, lse = _attn_group(u3, bkts[g], rbs[g], bsum, qge_all[sl], qgo_all[sl], kg_all[sl], g, batch, seq)
            outs.append(o.reshape(m_rows, GROUP_DIM))
            lses.append(lse.reshape(m_rows, GROUP_DIM))
        x2d = _merge_ffn(x2d, c2d, outs, lses, u2d, w_conv_b, w_attn_b, w_out_b,
                         norm2_g, w_ff1_b, w_ff2_b, layer, m_rows)
    return x2d.reshape(batch, seq, D_MODEL)
```

```python
import functools
import math

import jax
import jax.numpy as jnp
from jax import lax
from jax.experimental import pallas as pl
from jax.experimental.pallas import tpu as pltpu

F32 = jnp.float32
BF16 = jnp.bfloat16

D_MODEL = 1024
CONV_DIM = 512
CONV_WIDTH = 31
N_GROUPS = 3
HEADS_PER_GROUP = 8
HEAD_DIM = 64
GROUP_DIM = HEADS_PER_GROUP * HEAD_DIM
ATTN_DIM = N_GROUPS * GROUP_DIM
DILATIONS = (1, 4, 16)
SUB_WINDOW = 128
BLOCK = 128
NUM_BUCKETS = 32
MAX_REL_DISTANCE = 2048
D_FF = 4 * D_MODEL
EPS = 1e-6
NEG_INF = -1e30
IN_COLS = 2 * CONV_DIM + 3 * ATTN_DIM + 2 * D_MODEL

LANES = 128
SUBLANES = 8
COL_TILE = 512
N_COL_TILES = IN_COLS // COL_TILE
N_CONV_COL_TILES = 2 * CONV_DIM // COL_TILE
U_COLS = IN_COLS - 2 * CONV_DIM
N_SLABS = D_MODEL // LANES
N_PAIRS = HEADS_PER_GROUP // 2
BLOCK_LOOP_UNROLL = {1: 2, 4: 2, 16: 7}
D1_SEGMENTS = 4
MIX_ROWS = 512
FF_CHUNK = 2048
CONV_HALO = 32
CONV_TILE = 256
CONV_CHUNK = 128
VMEM_LIMIT = 56 * 1024 * 1024


def _cparams(sem):
    return pltpu.CompilerParams(dimension_semantics=sem, vmem_limit_bytes=VMEM_LIMIT)


def _slab(c):
    return slice(c * LANES, (c + 1) * LANES)


def _in_proj_kernel(*refs, seq):
    x_refs = refs[:N_SLABS]
    (g_ref, w_ref, dw_ref, db_ref, lg_ref, lb_ref, u_ref, c_ref,
     h_ref, rn_ref, tmp_ref, uc_ref, z_ref, acc_ref) = refs[N_SLABS:]
    j = pl.program_id(1)
    rc = 128
    quarter = seq // 4

    @pl.when((pl.program_id(0) == 0) & (j == 0))
    def _():
        uc_ref[...] = jnp.zeros(uc_ref.shape, BF16)
        z_ref[...] = jnp.zeros(z_ref.shape, F32)
        acc_ref[...] = jnp.zeros(acc_ref.shape, F32)

    @pl.when(j == 0)
    def _():
        def natural(ci, carry):
            rows = pl.ds(pl.multiple_of(ci * rc, rc), rc)
            xs = [x_refs[c][rows, :] for c in range(N_SLABS)]
            ss = xs[0] * xs[0]
            for c in range(1, N_SLABS):
                ss = ss + xs[c] * xs[c]
            rn = lax.rsqrt(jnp.sum(ss, axis=-1, keepdims=True) * (1.0 / D_MODEL) + EPS)
            rn_ref[rows, :] = jnp.broadcast_to(rn, (rc, LANES))
            for c in range(N_SLABS):
                h_ref[0, rows, _slab(c)] = (xs[c] * rn * g_ref[:, _slab(c)]).astype(BF16)
            return carry

        lax.fori_loop(0, seq // rc, natural, 0, unroll=4)

        def by_four(r4, carry):
            base = pl.multiple_of(r4 * quarter, quarter)
            for a0 in range(0, quarter, rc):
                src = pl.ds(a0 * 4 + r4, rc, stride=4)
                rn = rn_ref[src, :]
                for c in range(N_SLABS):
                    y = x_refs[c][src, :] * rn * g_ref[:, _slab(c)]
                    h_ref[1, pl.ds(base + a0, rc), _slab(c)] = y.astype(BF16)
                    tmp_ref[c, a0:a0 + rc, :] = y
            for r2 in range(4):
                for c in range(N_SLABS):
                    h_ref[2, pl.ds(base + r2 * rc, rc), _slab(c)] = (
                        tmp_ref[c, pl.ds(r2, rc, stride=4), :].astype(BF16))
            return carry

        lax.fori_loop(0, 4, by_four, 0)

    jj = jnp.clip(j - 2, 0, 8)
    sel = jnp.where((j >= 2) & (j <= 10), lax.rem(jj, 3), 0)

    n_iter = CONV_TILE // CONV_CHUNK
    n_chunks = seq // CONV_CHUNK
    dot_rows = seq // n_iter

    def body(i, carry):
        rows = pl.ds(pl.multiple_of(i * dot_rows, dot_rows), dot_rows)
        u_ref[rows, :] = jnp.dot(h_ref[sel, rows, :], w_ref[...], preferred_element_type=F32).astype(BF16)
        tap_chunk = jnp.clip(j - 3, 0, seq // CONV_TILE - 1) * n_iter + i
        glu_chunk = jnp.clip((j - 3) * n_iter + i + 1, 0, n_chunks - 1)
        chunk_rows = pl.ds(pl.multiple_of(i * CONV_CHUNK, CONV_CHUNK), CONV_CHUNK)
        _conv_norm(acc_ref, (j + 1) & 1, chunk_rows, lg_ref, lb_ref, c_ref)
        _conv_taps(z_ref, pl.multiple_of(tap_chunk * CONV_CHUNK, CONV_CHUNK), dw_ref, db_ref,
                   acc_ref, j & 1, chunk_rows)
        _conv_glu(uc_ref, z_ref, pl.multiple_of(glu_chunk * CONV_CHUNK, CONV_CHUNK))
        return carry

    conv_active = (j >= 2) & (j <= seq // CONV_TILE + 3)

    @pl.when(conv_active)
    def _():
        lax.fori_loop(0, n_iter, body, 0)

    @pl.when(jnp.logical_not(conv_active))
    def _():
        u_ref[...] = jnp.dot(h_ref[sel], w_ref[...], preferred_element_type=F32).astype(BF16)

    @pl.when(j == 0)
    def _():
        uc_ref[:, 0:COL_TILE] = u_ref[...]

    @pl.when(j == 1)
    def _():
        uc_ref[:, COL_TILE:2 * COL_TILE] = u_ref[...]


def _conv_glu(uc_ref, z_ref, t0):
    a = uc_ref[pl.ds(t0, CONV_CHUNK), 0:CONV_DIM].astype(F32)
    gt = uc_ref[pl.ds(t0, CONV_CHUNK), CONV_DIM:2 * CONV_DIM].astype(F32)
    z_ref[pl.ds(t0 + CONV_HALO, CONV_CHUNK), :] = a * jax.nn.sigmoid(gt)


def _conv_norm(acc_ref, slot, chunk_rows, lg_ref, lb_ref, c_ref):
    acc = acc_ref[slot, chunk_rows, :]
    mu = jnp.mean(acc, axis=-1, keepdims=True)
    xc = acc - mu
    y = xc * lax.rsqrt(jnp.mean(xc * xc, axis=-1, keepdims=True) + EPS)
    y = y * lg_ref[...] + lb_ref[...]
    c_ref[chunk_rows, :] = (y * jax.nn.sigmoid(y)).astype(BF16)


def _conv_taps(z_ref, t0, w_ref, b_ref, acc_ref, slot, chunk_rows):
    chunk = CONV_CHUNK
    first_tap = CONV_HALO - (CONV_WIDTH - 1)
    n_win = chunk + CONV_HALO
    for lt in range(CONV_DIM // LANES):
        ls = _slab(lt)
        window = z_ref[pl.ds(t0, n_win), ls]
        acc = None
        for rho in range(SUBLANES):
            offs = [o for o in range(first_tap, first_tap + CONV_WIDTH) if o % SUBLANES == rho]
            rolled = window if rho == 0 else pltpu.roll(window, n_win - rho, axis=0)
            terms = [w_ref[o - first_tap:o - first_tap + 1, ls] * rolled[o - rho:o - rho + chunk, :]
                     for o in offs]
            while len(terms) > 1:
                terms = [terms[i] + terms[i + 1] for i in range(0, len(terms) - 1, 2)] + (
                    [terms[-1]] if len(terms) % 2 else [])
            acc = terms[0] if acc is None else acc + terms[0]
        acc_ref[slot, chunk_rows, ls] = acc + b_ref[:, ls]


def _in_proj(x2d, g, w, dw_w, dw_b, ln_g, ln_b, layer, batch, seq):
    assert seq // 16 == 128 and COL_TILE == CONV_DIM
    n_conv_tiles = seq // CONV_TILE
    x_specs = [pl.BlockSpec((seq, LANES),
                            functools.partial(lambda b, j, c: (jnp.minimum(b + jnp.minimum(j, 1), batch - 1), c), c=c))
               for c in range(N_SLABS)]
    vec = lambda: pl.BlockSpec((None, 1, CONV_DIM), lambda b, j: (layer, 0, 0))
    return pl.pallas_call(
        functools.partial(_in_proj_kernel, seq=seq),
        grid=(batch, N_COL_TILES),
        in_specs=x_specs + [
            pl.BlockSpec((None, 1, D_MODEL), lambda b, j: (layer, 0, 0)),
            pl.BlockSpec((None, D_MODEL, COL_TILE), lambda b, j: (layer, 0, j)),
            pl.BlockSpec((None, CONV_WIDTH, CONV_DIM), lambda b, j: (layer, 0, 0)),
            vec(), vec(), vec(),
        ],
        out_specs=[
            pl.BlockSpec((seq, COL_TILE), lambda b, j: (b, jnp.maximum(j - N_CONV_COL_TILES, 0))),
            pl.BlockSpec((CONV_TILE, CONV_DIM),
                         lambda b, j: (b * n_conv_tiles + jnp.clip(j - 4, 0, n_conv_tiles - 1), 0)),
        ],
        out_shape=[jax.ShapeDtypeStruct((batch * seq, U_COLS), BF16),
                   jax.ShapeDtypeStruct((batch * seq, CONV_DIM), BF16)],
        scratch_shapes=[pltpu.VMEM((N_GROUPS, seq, D_MODEL), BF16),
                        pltpu.VMEM((seq, LANES), F32),
                        pltpu.VMEM((N_SLABS, seq // 4, LANES), F32),
                        pltpu.VMEM((seq, 2 * CONV_DIM), BF16),
                        pltpu.VMEM((CONV_HALO + seq, CONV_DIM), F32),
                        pltpu.VMEM((2, CONV_TILE, CONV_DIM), F32)],
        compiler_params=_cparams(("arbitrary", "arbitrary")),
        name="in_proj",
    )(*([x2d] * N_SLABS), g, w, dw_w, dw_b, ln_g, ln_b)


def _attn_kernel(*refs, seq):
    for group, d in enumerate(DILATIONS):
        pl.when(pl.program_id(0) == group)(functools.partial(_attn_body, *refs, seq=seq, d=d, group=group))


def _attn_body(bkt_ref, rb_ref, bsum_ref, qge_ref, qgo_ref, kg_ref, q_ref, k_ref, v_ref,
               o_ref, lse_ref,
               bm_ref, qe_ref, qo_ref, kn_ref, va_ref, s_ref, *stage, seq, d, group):
    sub_len = seq // d
    nb = sub_len // BLOCK

    @pl.when(pl.program_id(1) == 0)
    def _():
        bk = bkt_ref[0]
        for h in range(HEADS_PER_GROUP):
            acc = jnp.full((BLOCK, 2 * BLOCK), NEG_INF, F32)
            for b in range(NUM_BUCKETS):
                acc = jnp.where(bk == b, rb_ref[(group * NUM_BUCKETS + b) * HEADS_PER_GROUP + h], acc)
            bm_ref[h // 2, (h % 2) * BLOCK:(h % 2 + 1) * BLOCK, :] = acc
        va_ref[:, :, LANES:2 * LANES] = jnp.ones((N_PAIRS, seq, LANES), BF16)

    rc = 256

    def head_mean_sq(t):
        sq = t * t
        half = GROUP_DIM // 2
        return jnp.concatenate(
            [jnp.dot(sq[:, 0:half], bsum_ref[...], preferred_element_type=F32),
             jnp.dot(sq[:, half:GROUP_DIM], bsum_ref[...], preferred_element_type=F32)], axis=1)

    def norm_body(ci, carry):
        rows = pl.ds(pl.multiple_of(ci * rc, rc), rc)
        q = q_ref[0, rows, :]
        qr = q.astype(F32) * lax.rsqrt(head_mean_sq(q) + EPS)
        qe_ref[rows, :] = (qr * qge_ref[...]).astype(BF16)
        qo_ref[rows, :] = (qr * qgo_ref[...]).astype(BF16)
        k = k_ref[0, rows, :]
        kr = k.astype(F32) * lax.rsqrt(head_mean_sq(k) + EPS)
        kn_ref[rows, :] = (kr * kg_ref[...]).astype(BF16)
        for p in range(N_PAIRS):
            va_ref[p, rows, 0:LANES] = v_ref[0, rows, _slab(p)]
        return carry

    lax.fori_loop(0, seq // rc, norm_body, 0, unroll=8)

    lt64 = lax.broadcasted_iota(jnp.int32, (BLOCK, LANES), 1) < HEAD_DIM

    def block_aligned(row):
        return row if isinstance(row, int) else pl.multiple_of(row, BLOCK)

    def key_rows(row0, first):
        return pl.ds(row0, BLOCK) if first else pl.ds(block_aligned(row0 - BLOCK), 2 * BLOCK)

    def score_products(row0, first):
        row0 = block_aligned(row0)
        qrows, krows = pl.ds(row0, BLOCK), key_rows(row0, first)
        out = []
        for p in range(N_PAIRS):
            q2 = jnp.concatenate([qe_ref[qrows, _slab(p)], qo_ref[qrows, _slab(p)]], axis=0)
            out.append(lax.dot_general(q2, kn_ref[krows, _slab(p)], (((1,), (1,)), ((), ())),
                                       preferred_element_type=F32))
        return out

    def width(first):
        return BLOCK if first else 2 * BLOCK

    def park(slot, scores, first):
        for p in range(N_PAIRS):
            s_ref[slot, p, :, 0:width(first)] = scores[p]

    def unpark(slot, first):
        return [s_ref[slot, p, :, 0:width(first)] for p in range(N_PAIRS)]

    def finish_block(scores, row0, t0, first):
        row0 = block_aligned(row0)
        qrows, krows = pl.ds(row0, BLOCK), key_rows(row0, first)
        probs, maxes = [], []
        for p in range(N_PAIRS):
            s = scores[p] + (bm_ref[p, :, BLOCK:2 * BLOCK] if first else bm_ref[p])
            m = jnp.max(s, axis=-1, keepdims=True)
            probs.append(jnp.exp(s - m).astype(BF16))
            maxes.append(m)
        results = [jnp.dot(probs[p], va_ref[p, krows, :], preferred_element_type=F32) for p in range(N_PAIRS)]
        for p in range(N_PAIRS):
            re, ro = results[p][:BLOCK], results[p][BLOCK:]
            me, mo = maxes[p][:BLOCK], maxes[p][BLOCK:]
            denom = jnp.where(lt64, re[:, LANES:], ro[:, LANES:])
            o_pair = jnp.where(lt64, re[:, :LANES], ro[:, :LANES]) / denom
            lse_pair = jnp.where(lt64, me, mo) + jnp.log(denom)
            if d == 1:
                o_ref[0, qrows, _slab(p)] = o_pair.astype(BF16)
                lse_ref[0, qrows, _slab(p)] = lse_pair
            else:
                nat = pl.ds(t0, BLOCK, stride=d)
                stage[0][p, nat, :] = o_pair
                stage[1][p, nat, :] = lse_pair

    def run_chain(base, mode, t0_of, n_blocks):
        def first(n):
            return mode == "all" or (mode == "head" and n == 0)

        def row(n):
            return base + n * BLOCK

        park(0, score_products(row(0), first(0)), first(0))
        park(1, score_products(row(1), first(1)), first(1))
        finish_block(unpark(0, first(0)), row(0), t0_of(0), first(0))

        def blk_body(n, c):
            slot = n & 1
            prev = unpark(1 - slot, first(1))
            park(slot, score_products(row(n), first(2)), first(2))
            finish_block(prev, row(n - 1), t0_of(n - 1), first(1))
            return c

        lax.fori_loop(2, n_blocks, blk_body, 0, unroll=BLOCK_LOOP_UNROLL[d])
        last = n_blocks - 1
        finish_block(unpark(last & 1, first(last)), row(last), t0_of(last), first(last))

    def residue_of_class(c):
        if isinstance(c, int):
            return c // 4 + 4 * (c % 4)
        return lax.shift_right_logical(c, 2) + 4 * (c & 3)

    if d == 1:
        seg_blocks = nb // D1_SEGMENTS
        run_chain(0, "head", lambda n: n * BLOCK, seg_blocks)

        def segment_body(sg, carry):
            base = pl.multiple_of(sg * (seg_blocks * BLOCK), BLOCK)
            run_chain(base, "none", lambda n: base + n * BLOCK, seg_blocks)
            return carry

        lax.fori_loop(1, D1_SEGMENTS, segment_body, 0)
    elif nb > 1:
        def class_body(r, carry):
            run_chain(pl.multiple_of(r * sub_len, BLOCK), "head", lambda n: n * (BLOCK * d) + r, nb)
            return carry

        lax.fori_loop(0, d, class_body, 0, unroll=2)
    else:
        def class_group(cg, carry):
            run_chain(pl.multiple_of(cg * (4 * BLOCK), BLOCK), "all", lambda n: residue_of_class(cg * 4 + n), 4)
            return carry

        lax.fori_loop(0, d // 4, class_group, 0)

    if d > 1:
        def copy_body(ci, carry):
            rows = pl.ds(pl.multiple_of(ci * rc, rc), rc)
            for p in range(N_PAIRS):
                o_ref[0, rows, _slab(p)] = stage[0][p, rows, :].astype(BF16)
                lse_ref[0, rows, _slab(p)] = stage[1][p, rows, :]
            return carry

        lax.fori_loop(0, seq // rc, copy_body, 0, unroll=4)


def _attn_all(u3, bkts, rbs, bsum, qge, qgo, kg, batch, seq):
    assert DILATIONS == (1, 4, 16) and seq // max(DILATIONS) == BLOCK
    const = lambda shape: pl.BlockSpec(shape, lambda g, b: (0,) * len(shape))
    col = lambda first_tile: pl.BlockSpec((1, seq, GROUP_DIM), lambda g, b: (b, 0, first_tile + g))
    scratch = [
        pltpu.VMEM((N_PAIRS, 2 * BLOCK, 2 * BLOCK), F32),
        pltpu.VMEM((seq, GROUP_DIM), BF16),
        pltpu.VMEM((seq, GROUP_DIM), BF16),
        pltpu.VMEM((seq, GROUP_DIM), BF16),
        pltpu.VMEM((N_PAIRS, seq, 2 * LANES), BF16),
        pltpu.VMEM((2, N_PAIRS, 2 * BLOCK, 2 * BLOCK), F32),
        pltpu.VMEM((N_PAIRS, seq, LANES), F32),
        pltpu.VMEM((N_PAIRS, seq, LANES), F32),
    ]
    return pl.pallas_call(
        functools.partial(_attn_kernel, seq=seq),
        grid=(N_GROUPS, batch),
        in_specs=[
            pl.BlockSpec((1, BLOCK, 2 * BLOCK), lambda g, b: (g, 0, 0)),
            pl.BlockSpec(memory_space=pltpu.SMEM),
            const((GROUP_DIM // 2, GROUP_DIM // 2)),
            const((1, GROUP_DIM)), const((1, GROUP_DIM)), const((1, GROUP_DIM)),
            col(0), col(N_GROUPS), col(2 * N_GROUPS),
        ],
        out_specs=[pl.BlockSpec((None, 1, seq, GROUP_DIM), lambda g, b: (g, b, 0, 0))] * 2,
        out_shape=[jax.ShapeDtypeStruct((N_GROUPS, batch, seq, GROUP_DIM), BF16),
                   jax.ShapeDtypeStruct((N_GROUPS, batch, seq, GROUP_DIM), F32)],
        scratch_shapes=scratch,
        compiler_params=_cparams(("arbitrary", "arbitrary")),
        name="attn",
    )(bkts, rbs, bsum, qge, qgo, kg, u3, u3, u3)


def _merge_ffn_kernel(x_ref, c_ref, o0_ref, o1_ref, o2_ref, l0_ref, l1_ref, l2_ref,
                      gc0_ref, gc1_ref, ga0_ref, ga1_ref, wc_ref, wa_ref, wo_ref,
                      g2_ref, w1_ref, w2_ref, out_ref):
    l0, l1, l2 = l0_ref[...], l1_ref[...], l2_ref[...]
    m = jnp.maximum(jnp.maximum(l0, l1), l2)
    e0, e1, e2 = jnp.exp(l0 - m), jnp.exp(l1 - m), jnp.exp(l2 - m)
    o = (e0 * o0_ref[...].astype(F32) + e1 * o1_ref[...].astype(F32)
         + e2 * o2_ref[...].astype(F32)) / (e0 + e1 + e2)
    y_attn = jnp.dot(o.astype(BF16), wa_ref[...], preferred_element_type=F32)
    y_conv = jnp.dot(c_ref[...], wc_ref[...], preferred_element_type=F32)
    half = D_MODEL // 2
    mix = []
    for hs, gc_ref, ga_ref in ((slice(0, half), gc0_ref, ga0_ref), (slice(half, D_MODEL), gc1_ref, ga1_ref)):
        gc = jax.nn.sigmoid(gc_ref[...].astype(F32))
        ga = jax.nn.sigmoid(ga_ref[...].astype(F32))
        mix.append((gc * y_conv[:, hs] + ga * y_attn[:, hs]).astype(BF16))
    y = (jnp.dot(mix[0], wo_ref[0:half, :], preferred_element_type=F32)
         + jnp.dot(mix[1], wo_ref[half:D_MODEL, :], preferred_element_type=F32))
    x = x_ref[...] + y

    rn = lax.rsqrt(jnp.mean(x * x, axis=-1, keepdims=True) + EPS)
    h = (x * rn * g2_ref[...]).astype(BF16)
    y2 = None
    for f0 in range(0, D_FF, FF_CHUNK):
        a = jnp.dot(h, w1_ref[:, f0:f0 + FF_CHUNK], preferred_element_type=F32)
        a = jnp.square(jnp.maximum(a, 0.0)).astype(BF16)
        t = jnp.dot(a, w2_ref[f0:f0 + FF_CHUNK, :], preferred_element_type=F32)
        y2 = t if y2 is None else y2 + t
    out_ref[...] = x + y2


def _merge_ffn(x2d, c2d, o_all, lse_all, u2d, wc, wa, wo, g2, w1, w2, layer, m_rows):
    tm = MIX_ROWS
    resident = pl.Buffered(1)
    row = lambda width: pl.BlockSpec((tm, width), lambda i: (i, 0))
    grp = lambda g: pl.BlockSpec((None, tm, GROUP_DIM), functools.partial(lambda i, g: (g, i, 0), g=g))
    groups = [grp(g) for g in range(N_GROUPS)]
    gate = lambda tile: pl.BlockSpec((tm, COL_TILE), lambda i: (i, tile))
    wspec = lambda k, n: pl.BlockSpec((None, k, n), lambda i: (layer, 0, 0), pipeline_mode=resident)
    gate0 = 3 * ATTN_DIM // COL_TILE
    return pl.pallas_call(
        _merge_ffn_kernel,
        grid=(m_rows // tm,),
        in_specs=[row(D_MODEL), row(CONV_DIM)] + groups + groups
                 + [gate(gate0), gate(gate0 + 1), gate(gate0 + 2), gate(gate0 + 3)]
                 + [wspec(CONV_DIM, D_MODEL), wspec(GROUP_DIM, D_MODEL), wspec(D_MODEL, D_MODEL),
                    pl.BlockSpec((None, 1, D_MODEL), lambda i: (layer, 0, 0)),
                    wspec(D_MODEL, D_FF), wspec(D_FF, D_MODEL)],
        out_specs=row(D_MODEL),
        out_shape=jax.ShapeDtypeStruct((m_rows, D_MODEL), F32),
        compiler_params=_cparams(("arbitrary",)),
        name="merge_ffn",
    )(x2d, c2d, *([o_all] * N_GROUPS), *([lse_all] * N_GROUPS), u2d, u2d, u2d, u2d, wc, wa, wo, g2, w1, w2)


def _t5_bucket(dist):
    max_exact = NUM_BUCKETS // 2
    nf = jnp.maximum(dist, 1).astype(jnp.float32)
    large = max_exact + (jnp.log(nf / max_exact) / math.log(MAX_REL_DISTANCE / max_exact)
                         * (NUM_BUCKETS - max_exact)).astype(jnp.int32)
    large = jnp.minimum(large, NUM_BUCKETS - 1)
    return jnp.where(dist < max_exact, dist, large)


def _bucket_tile(d):
    qi = jnp.arange(BLOCK)[:, None]
    kj = jnp.arange(2 * BLOCK)[None, :]
    off = qi + BLOCK - kj
    band = (off >= 0) & (off <= SUB_WINDOW)
    bucket = _t5_bucket(jnp.clip(off, 0, SUB_WINDOW) * d)
    return jnp.where(band, bucket, -1).astype(jnp.int32)


def kernel(x, rel_bias, norm1_g, w_in, q_norm_g, k_norm_g, conv_dw_w, conv_dw_b, conv_ln_g, conv_ln_b,
           w_conv_out, w_attn_out, w_out, norm2_g, w_ff1, w_ff2):
    batch, seq, _ = x.shape
    depth = w_in.shape[0]
    m_rows = batch * seq
    assert seq % (BLOCK * max(DILATIONS)) == 0 and x.shape[2] == D_MODEL

    w_in_b, w_conv_b, w_attn_b, w_out_b = (w.astype(BF16) for w in (w_in, w_conv_out, w_attn_out, w_out))
    w_ff1_b, w_ff2_b = w_ff1.astype(BF16), w_ff2.astype(BF16)
    head_of_col = jnp.arange(GROUP_DIM) // HEAD_DIM
    even_head = (head_of_col % 2 == 0).astype(F32)[None, :]
    qg_all = jnp.tile(q_norm_g, (1, HEADS_PER_GROUP)) * (HEAD_DIM ** -0.5)
    qge_all, qgo_all = qg_all * even_head, qg_all * (1.0 - even_head)
    kg_all = jnp.tile(k_norm_g, (1, HEADS_PER_GROUP))
    head_of_half = head_of_col[:GROUP_DIM // 2]
    bsum = ((head_of_half[:, None] == head_of_half[None, :]).astype(F32) * (1.0 / HEAD_DIM)).astype(BF16)
    bkts = jnp.stack([_bucket_tile(d) for d in DILATIONS])
    rbs = rel_bias.astype(F32).reshape(NUM_BUCKETS, N_GROUPS, HEADS_PER_GROUP).transpose(1, 0, 2).reshape(-1)

    vec3 = lambda a: a.reshape(depth, 1, a.shape[-1])
    norm1_g, norm2_g, conv_dw_b, conv_ln_g, conv_ln_b = (vec3(a) for a in (norm1_g, norm2_g, conv_dw_b, conv_ln_g, conv_ln_b))
    x2d = x.reshape(m_rows, D_MODEL)
    for layer in range(depth):
        u2d, c2d = _in_proj(x2d, norm1_g, w_in_b, conv_dw_w, conv_dw_b, conv_ln_g, conv_ln_b, layer, batch, seq)
        u3 = u2d.reshape(batch, seq, U_COLS)
        sl = slice(layer, layer + 1)
        o_all, lse_all = _attn_all(u3, bkts, rbs, bsum, qge_all[sl], qgo_all[sl], kg_all[sl], batch, seq)
        x2d = _merge_ffn(x2d, c2d, o_all.reshape(N_GROUPS, m_rows, GROUP_DIM),
                         lse_all.reshape(N_GROUPS, m_rows, GROUP_DIM), u2d, w_conv_b, w_attn_b, w_out_b,
                         norm2_g, w_ff1_b, w_ff2_b, layer, m_rows)
    return x2d.reshape(batch, seq, D_MODEL)
```

```python
import functools
import math

import jax
import jax.numpy as jnp
from jax import lax
from jax.experimental import pallas as pl
from jax.experimental.pallas import tpu as pltpu

F32 = jnp.float32
BF16 = jnp.bfloat16

D_MODEL = 1024
CONV_DIM = 512
CONV_WIDTH = 31
N_GROUPS = 3
HEADS_PER_GROUP = 8
HEAD_DIM = 64
GROUP_DIM = HEADS_PER_GROUP * HEAD_DIM
ATTN_DIM = N_GROUPS * GROUP_DIM
DILATIONS = (1, 4, 16)
SUB_WINDOW = 128
BLOCK = 128
NUM_BUCKETS = 32
MAX_REL_DISTANCE = 2048
D_FF = 4 * D_MODEL
EPS = 1e-6
NEG_INF = -1e30
IN_COLS = 2 * CONV_DIM + 3 * ATTN_DIM + 2 * D_MODEL

LANES = 128
SUBLANES = 8
COL_TILE = 512
N_COL_TILES = IN_COLS // COL_TILE
N_CONV_COL_TILES = 2 * CONV_DIM // COL_TILE
U_COLS = IN_COLS - 2 * CONV_DIM
N_SLABS = D_MODEL // LANES
N_PAIRS = HEADS_PER_GROUP // 2
BLOCK_LOOP_UNROLL = {1: 2, 4: 2, 16: 7}
D1_SEGMENTS = 4
MIX_ROWS = 512
FF_CHUNK = 2048
CONV_HALO = 32
CONV_TILE = 256
CONV_CHUNK = 128
VMEM_LIMIT = 56 * 1024 * 1024


def _cparams(sem):
    return pltpu.CompilerParams(dimension_semantics=sem, vmem_limit_bytes=VMEM_LIMIT)


def _slab(c):
    return slice(c * LANES, (c + 1) * LANES)


def _in_proj_kernel(*refs, seq):
    x_refs = refs[:N_SLABS]
    (g_ref, w_ref, dw_ref, db_ref, lg_ref, lb_ref, u_ref, c_ref,
     h_ref, rn_ref, tmp_ref, uc_ref, z_ref, acc_ref) = refs[N_SLABS:]
    j = pl.program_id(1)
    rc = 128
    quarter = seq // 4

    @pl.when((pl.program_id(0) == 0) & (j == 0))
    def _():
        uc_ref[...] = jnp.zeros(uc_ref.shape, BF16)
        z_ref[...] = jnp.zeros(z_ref.shape, F32)
        acc_ref[...] = jnp.zeros(acc_ref.shape, F32)

    @pl.when(j == 0)
    def _():
        def natural(ci, carry):
            rows = pl.ds(pl.multiple_of(ci * rc, rc), rc)
            xs = [x_refs[c][rows, :] for c in range(N_SLABS)]
            ss = xs[0] * xs[0]
            for c in range(1, N_SLABS):
                ss = ss + xs[c] * xs[c]
            rn = lax.rsqrt(jnp.sum(ss, axis=-1, keepdims=True) * (1.0 / D_MODEL) + EPS)
            rn_ref[rows, :] = jnp.broadcast_to(rn, (rc, LANES))
            for c in range(N_SLABS):
                h_ref[0, rows, _slab(c)] = (xs[c] * rn * g_ref[:, _slab(c)]).astype(BF16)
            return carry

        lax.fori_loop(0, seq // rc, natural, 0, unroll=4)

        def by_four(r4, carry):
            base = pl.multiple_of(r4 * quarter, quarter)
            for a0 in range(0, quarter, rc):
                src = pl.ds(a0 * 4 + r4, rc, stride=4)
                rn = rn_ref[src, :]
                for c in range(N_SLABS):
                    y = x_refs[c][src, :] * rn * g_ref[:, _slab(c)]
                    h_ref[1, pl.ds(base + a0, rc), _slab(c)] = y.astype(BF16)
                    tmp_ref[c, a0:a0 + rc, :] = y
            for r2 in range(4):
                for c in range(N_SLABS):
                    h_ref[2, pl.ds(base + r2 * rc, rc), _slab(c)] = (
                        tmp_ref[c, pl.ds(r2, rc, stride=4), :].astype(BF16))
            return carry

        lax.fori_loop(0, 4, by_four, 0)

    jj = jnp.clip(j - 2, 0, 8)
    sel = jnp.where((j >= 2) & (j <= 10), lax.rem(jj, 3), 0)

    n_iter = CONV_TILE // CONV_CHUNK
    n_chunks = seq // CONV_CHUNK
    dot_rows = seq // n_iter

    def body(i, carry):
        rows = pl.ds(pl.multiple_of(i * dot_rows, dot_rows), dot_rows)
        u_ref[rows, :] = jnp.dot(h_ref[sel, rows, :], w_ref[...].astype(BF16),
                                 preferred_element_type=F32).astype(BF16)
        tap_chunk = jnp.clip(j - 3, 0, seq // CONV_TILE - 1) * n_iter + i
        glu_chunk = jnp.clip((j - 3) * n_iter + i + 1, 0, n_chunks - 1)
        chunk_rows = pl.ds(pl.multiple_of(i * CONV_CHUNK, CONV_CHUNK), CONV_CHUNK)
        _conv_norm(acc_ref, (j + 1) & 1, chunk_rows, lg_ref, lb_ref, c_ref)
        _conv_taps(z_ref, pl.multiple_of(tap_chunk * CONV_CHUNK, CONV_CHUNK), dw_ref, db_ref,
                   acc_ref, j & 1, chunk_rows)
        _conv_glu(uc_ref, z_ref, pl.multiple_of(glu_chunk * CONV_CHUNK, CONV_CHUNK))
        return carry

    conv_active = (j >= 2) & (j <= seq // CONV_TILE + 3)

    @pl.when(conv_active)
    def _():
        lax.fori_loop(0, n_iter, body, 0)

    @pl.when(jnp.logical_not(conv_active))
    def _():
        u_ref[...] = jnp.dot(h_ref[sel], w_ref[...].astype(BF16), preferred_element_type=F32).astype(BF16)

    @pl.when(j == 0)
    def _():
        uc_ref[:, 0:COL_TILE] = u_ref[...]

    @pl.when(j == 1)
    def _():
        uc_ref[:, COL_TILE:2 * COL_TILE] = u_ref[...]


def _conv_glu(uc_ref, z_ref, t0):
    a = uc_ref[pl.ds(t0, CONV_CHUNK), 0:CONV_DIM].astype(F32)
    gt = uc_ref[pl.ds(t0, CONV_CHUNK), CONV_DIM:2 * CONV_DIM].astype(F32)
    z_ref[pl.ds(t0 + CONV_HALO, CONV_CHUNK), :] = a * jax.nn.sigmoid(gt)


def _conv_norm(acc_ref, slot, chunk_rows, lg_ref, lb_ref, c_ref):
    acc = acc_ref[slot, chunk_rows, :]
    mu = jnp.mean(acc, axis=-1, keepdims=True)
    xc = acc - mu
    y = xc * lax.rsqrt(jnp.mean(xc * xc, axis=-1, keepdims=True) + EPS)
    y = y * lg_ref[...] + lb_ref[...]
    c_ref[chunk_rows, :] = (y * jax.nn.sigmoid(y)).astype(BF16)


def _conv_taps(z_ref, t0, w_ref, b_ref, acc_ref, slot, chunk_rows):
    chunk = CONV_CHUNK
    first_tap = CONV_HALO - (CONV_WIDTH - 1)
    n_win = chunk + CONV_HALO
    for lt in range(CONV_DIM // LANES):
        ls = _slab(lt)
        window = z_ref[pl.ds(t0, n_win), ls]
        acc = None
        for rho in range(SUBLANES):
            offs = [o for o in range(first_tap, first_tap + CONV_WIDTH) if o % SUBLANES == rho]
            rolled = window if rho == 0 else pltpu.roll(window, n_win - rho, axis=0)
            terms = [w_ref[o - first_tap:o - first_tap + 1, ls] * rolled[o - rho:o - rho + chunk, :]
                     for o in offs]
            while len(terms) > 1:
                terms = [terms[i] + terms[i + 1] for i in range(0, len(terms) - 1, 2)] + (
                    [terms[-1]] if len(terms) % 2 else [])
            acc = terms[0] if acc is None else acc + terms[0]
        acc_ref[slot, chunk_rows, ls] = acc + b_ref[:, ls]


def _in_proj(x2d, g, w, dw_w, dw_b, ln_g, ln_b, layer, batch, seq):
    assert seq // 16 == 128 and COL_TILE == CONV_DIM
    n_conv_tiles = seq // CONV_TILE
    x_specs = [pl.BlockSpec((seq, LANES),
                            functools.partial(lambda b, j, c: (jnp.minimum(b + jnp.minimum(j, 1), batch - 1), c), c=c))
               for c in range(N_SLABS)]
    vec = lambda: pl.BlockSpec((None, 1, CONV_DIM), lambda b, j: (layer, 0, 0))
    return pl.pallas_call(
        functools.partial(_in_proj_kernel, seq=seq),
        grid=(batch, N_COL_TILES),
        in_specs=x_specs + [
            pl.BlockSpec((None, 1, D_MODEL), lambda b, j: (layer, 0, 0)),
            pl.BlockSpec((None, D_MODEL, COL_TILE), lambda b, j: (layer, 0, j)),
            pl.BlockSpec((None, CONV_WIDTH, CONV_DIM), lambda b, j: (layer, 0, 0)),
            vec(), vec(), vec(),
        ],
        out_specs=[
            pl.BlockSpec((seq, COL_TILE), lambda b, j: (b, jnp.maximum(j - N_CONV_COL_TILES, 0))),
            pl.BlockSpec((CONV_TILE, CONV_DIM),
                         lambda b, j: (b * n_conv_tiles + jnp.clip(j - 4, 0, n_conv_tiles - 1), 0)),
        ],
        out_shape=[jax.ShapeDtypeStruct((batch * seq, U_COLS), BF16),
                   jax.ShapeDtypeStruct((batch * seq, CONV_DIM), BF16)],
        scratch_shapes=[pltpu.VMEM((N_GROUPS, seq, D_MODEL), BF16),
                        pltpu.VMEM((seq, LANES), F32),
                        pltpu.VMEM((N_SLABS, seq // 4, LANES), F32),
                        pltpu.VMEM((seq, 2 * CONV_DIM), BF16),
                        pltpu.VMEM((CONV_HALO + seq, CONV_DIM), F32),
                        pltpu.VMEM((2, CONV_TILE, CONV_DIM), F32)],
        compiler_params=_cparams(("arbitrary", "arbitrary")),
        name="in_proj",
    )(*([x2d] * N_SLABS), g, w, dw_w, dw_b, ln_g, ln_b)


def _attn_kernel(*refs, seq):
    for group, d in enumerate(DILATIONS):
        pl.when(pl.program_id(0) == group)(functools.partial(_attn_body, *refs, seq=seq, d=d, group=group))


def _bias_kernel(bkt_ref, rb_ref, bm_ref):
    group = pl.program_id(0)
    bk = bkt_ref[0]
    for h in range(HEADS_PER_GROUP):
        acc = jnp.full((BLOCK, 2 * BLOCK), NEG_INF, F32)
        for b in range(NUM_BUCKETS):
            acc = jnp.where(bk == b, rb_ref[(group * NUM_BUCKETS + b) * HEADS_PER_GROUP + h], acc)
        bm_ref[0, h // 2, (h % 2) * BLOCK:(h % 2 + 1) * BLOCK, :] = acc


def _bias_tiles(bkts, rbs):
    return pl.pallas_call(
        _bias_kernel,
        grid=(N_GROUPS,),
        in_specs=[pl.BlockSpec((1, BLOCK, 2 * BLOCK), lambda g: (g, 0, 0)),
                  pl.BlockSpec(memory_space=pltpu.SMEM)],
        out_specs=pl.BlockSpec((1, N_PAIRS, 2 * BLOCK, 2 * BLOCK), lambda g: (g, 0, 0, 0)),
        out_shape=jax.ShapeDtypeStruct((N_GROUPS, N_PAIRS, 2 * BLOCK, 2 * BLOCK), F32),
        compiler_params=_cparams(("arbitrary",)),
        name="bias_tiles",
    )(bkts, rbs)


def _attn_body(bm_ref, bsum_ref, qge_ref, qgo_ref, kg_ref, q_ref, k_ref, v_ref,
               o_ref, lse_ref,
               qe_ref, qo_ref, kn_ref, va_ref, s_ref, *stage, seq, d, group):
    sub_len = seq // d
    nb = sub_len // BLOCK
    bm_ref = bm_ref.at[0]

    @pl.when((pl.program_id(0) == 0) & (pl.program_id(1) == 0))
    def _():
        va_ref[:, :, LANES:2 * LANES] = jnp.ones((N_PAIRS, seq, LANES), BF16)

    rc = 256

    def head_mean_sq(t):
        sq = t * t
        half = GROUP_DIM // 2
        return jnp.concatenate(
            [jnp.dot(sq[:, 0:half], bsum_ref[...], preferred_element_type=F32),
             jnp.dot(sq[:, half:GROUP_DIM], bsum_ref[...], preferred_element_type=F32)], axis=1)

    def norm_body(ci, carry):
        rows = pl.ds(pl.multiple_of(ci * rc, rc), rc)
        q = q_ref[0, rows, :]
        qr = q.astype(F32) * lax.rsqrt(head_mean_sq(q) + EPS)
        qe_ref[rows, :] = (qr * qge_ref[...]).astype(BF16)
        qo_ref[rows, :] = (qr * qgo_ref[...]).astype(BF16)
        k = k_ref[0, rows, :]
        kr = k.astype(F32) * lax.rsqrt(head_mean_sq(k) + EPS)
        kn_ref[rows, :] = (kr * kg_ref[...]).astype(BF16)
        for p in range(N_PAIRS):
            va_ref[p, rows, 0:LANES] = v_ref[0, rows, _slab(p)]
        return carry

    lax.fori_loop(0, seq // rc, norm_body, 0, unroll=8)

    lt64 = lax.broadcasted_iota(jnp.int32, (BLOCK, LANES), 1) < HEAD_DIM

    def block_aligned(row):
        return row if isinstance(row, int) else pl.multiple_of(row, BLOCK)

    def key_rows(row0, first):
        return pl.ds(row0, BLOCK) if first else pl.ds(block_aligned(row0 - BLOCK), 2 * BLOCK)

    def score_products(row0, first):
        row0 = block_aligned(row0)
        qrows, krows = pl.ds(row0, BLOCK), key_rows(row0, first)
        out = []
        for p in range(N_PAIRS):
            q2 = jnp.concatenate([qe_ref[qrows, _slab(p)], qo_ref[qrows, _slab(p)]], axis=0)
            out.append(lax.dot_general(q2, kn_ref[krows, _slab(p)], (((1,), (1,)), ((), ())),
                                       preferred_element_type=F32))
        return out

    def width(first):
        return BLOCK if first else 2 * BLOCK

    def park(slot, scores, first):
        for p in range(N_PAIRS):
            s_ref[slot, p, :, 0:width(first)] = scores[p]

    def unpark(slot, first):
        return [s_ref[slot, p, :, 0:width(first)] for p in range(N_PAIRS)]

    def finish_block(scores, row0, t0, first):
        row0 = block_aligned(row0)
        qrows, krows = pl.ds(row0, BLOCK), key_rows(row0, first)
        probs, maxes = [], []
        for p in range(N_PAIRS):
            s = scores[p] + (bm_ref[p, :, BLOCK:2 * BLOCK] if first else bm_ref[p])
            m = jnp.max(s, axis=-1, keepdims=True)
            probs.append(jnp.exp(s - m).astype(BF16))
            maxes.append(m)
        results = [jnp.dot(probs[p], va_ref[p, krows, :], preferred_element_type=F32) for p in range(N_PAIRS)]
        for p in range(N_PAIRS):
            re, ro = results[p][:BLOCK], results[p][BLOCK:]
            me, mo = maxes[p][:BLOCK], maxes[p][BLOCK:]
            denom = jnp.where(lt64, re[:, LANES:], ro[:, LANES:])
            o_pair = jnp.where(lt64, re[:, :LANES], ro[:, :LANES]) / denom
            lse_pair = jnp.where(lt64, me, mo) + jnp.log(denom)
            if d == 1:
                o_ref[0, qrows, _slab(p)] = o_pair.astype(BF16)
                lse_ref[0, qrows, _slab(p)] = lse_pair
            else:
                nat = pl.ds(t0, BLOCK, stride=d)
                stage[0][p, nat, :] = o_pair
                stage[1][p, nat, :] = lse_pair

    def run_chain(base, mode, t0_of, n_blocks):
        def first(n):
            return mode == "all" or (mode == "head" and n == 0)

        def row(n):
            return base + n * BLOCK

        park(0, score_products(row(0), first(0)), first(0))
        park(1, score_products(row(1), first(1)), first(1))
        finish_block(unpark(0, first(0)), row(0), t0_of(0), first(0))

        def blk_body(n, c):
            slot = n & 1
            prev = unpark(1 - slot, first(1))
            park(slot, score_products(row(n), first(2)), first(2))
            finish_block(prev, row(n - 1), t0_of(n - 1), first(1))
            return c

        lax.fori_loop(2, n_blocks, blk_body, 0, unroll=BLOCK_LOOP_UNROLL[d])
        last = n_blocks - 1
        finish_block(unpark(last & 1, first(last)), row(last), t0_of(last), first(last))

    def residue_of_class(c):
        if isinstance(c, int):
            return c // 4 + 4 * (c % 4)
        return lax.shift_right_logical(c, 2) + 4 * (c & 3)

    if d == 1:
        seg_blocks = nb // D1_SEGMENTS
        run_chain(0, "head", lambda n: n * BLOCK, seg_blocks)

        def segment_body(sg, carry):
            base = pl.multiple_of(sg * (seg_blocks * BLOCK), BLOCK)
            run_chain(base, "none", lambda n: base + n * BLOCK, seg_blocks)
            return carry

        lax.fori_loop(1, D1_SEGMENTS, segment_body, 0)
    elif nb > 1:
        def class_body(r, carry):
            run_chain(pl.multiple_of(r * sub_len, BLOCK), "head", lambda n: n * (BLOCK * d) + r, nb)
            return carry

        lax.fori_loop(0, d, class_body, 0, unroll=2)
    else:
        def class_group(cg, carry):
            run_chain(pl.multiple_of(cg * (4 * BLOCK), BLOCK), "all", lambda n: residue_of_class(cg * 4 + n), 4)
            return carry

        lax.fori_loop(0, d // 4, class_group, 0)

    if d > 1:
        def copy_body(ci, carry):
            rows = pl.ds(pl.multiple_of(ci * rc, rc), rc)
            for p in range(N_PAIRS):
                o_ref[0, rows, _slab(p)] = stage[0][p, rows, :].astype(BF16)
                lse_ref[0, rows, _slab(p)] = stage[1][p, rows, :]
            return carry

        lax.fori_loop(0, seq // rc, copy_body, 0, unroll=4)


def _attn_all(u3, bias, bsum, qge, qgo, kg, batch, seq):
    assert DILATIONS == (1, 4, 16) and seq // max(DILATIONS) == BLOCK
    const = lambda shape: pl.BlockSpec(shape, lambda g, b: (0,) * len(shape))
    col = lambda first_tile: pl.BlockSpec((1, seq, GROUP_DIM), lambda g, b: (b, 0, first_tile + g))
    scratch = [
        pltpu.VMEM((seq, GROUP_DIM), BF16),
        pltpu.VMEM((seq, GROUP_DIM), BF16),
        pltpu.VMEM((seq, GROUP_DIM), BF16),
        pltpu.VMEM((N_PAIRS, seq, 2 * LANES), BF16),
        pltpu.VMEM((2, N_PAIRS, 2 * BLOCK, 2 * BLOCK), F32),
        pltpu.VMEM((N_PAIRS, seq, LANES), F32),
        pltpu.VMEM((N_PAIRS, seq, LANES), F32),
    ]
    return pl.pallas_call(
        functools.partial(_attn_kernel, seq=seq),
        grid=(N_GROUPS, batch),
        in_specs=[
            pl.BlockSpec((1, N_PAIRS, 2 * BLOCK, 2 * BLOCK), lambda g, b: (g, 0, 0, 0)),
            const((GROUP_DIM // 2, GROUP_DIM // 2)),
            const((1, GROUP_DIM)), const((1, GROUP_DIM)), const((1, GROUP_DIM)),
            col(0), col(N_GROUPS), col(2 * N_GROUPS),
        ],
        out_specs=[pl.BlockSpec((None, 1, seq, GROUP_DIM), lambda g, b: (g, b, 0, 0))] * 2,
        out_shape=[jax.ShapeDtypeStruct((N_GROUPS, batch, seq, GROUP_DIM), BF16),
                   jax.ShapeDtypeStruct((N_GROUPS, batch, seq, GROUP_DIM), F32)],
        scratch_shapes=scratch,
        compiler_params=_cparams(("arbitrary", "arbitrary")),
        name="attn",
    )(bias, bsum, qge, qgo, kg, u3, u3, u3)


def _merge_ffn_kernel(x_ref, c_ref, o0_ref, o1_ref, o2_ref, l0_ref, l1_ref, l2_ref,
                      gc0_ref, gc1_ref, ga0_ref, ga1_ref, wc_ref, wa_ref, wo_ref,
                      g2_ref, w1_ref, w2_ref, out_ref):
    l0, l1, l2 = l0_ref[...], l1_ref[...], l2_ref[...]
    m = jnp.maximum(jnp.maximum(l0, l1), l2)
    e0, e1, e2 = jnp.exp(l0 - m), jnp.exp(l1 - m), jnp.exp(l2 - m)
    o = (e0 * o0_ref[...].astype(F32) + e1 * o1_ref[...].astype(F32)
         + e2 * o2_ref[...].astype(F32)) / (e0 + e1 + e2)
    y_attn = jnp.dot(o.astype(BF16), wa_ref[...], preferred_element_type=F32)
    y_conv = jnp.dot(c_ref[...], wc_ref[...], preferred_element_type=F32)
    half = D_MODEL // 2
    mix = []
    for hs, gc_ref, ga_ref in ((slice(0, half), gc0_ref, ga0_ref), (slice(half, D_MODEL), gc1_ref, ga1_ref)):
        gc = jax.nn.sigmoid(gc_ref[...].astype(F32))
        ga = jax.nn.sigmoid(ga_ref[...].astype(F32))
        mix.append((gc * y_conv[:, hs] + ga * y_attn[:, hs]).astype(BF16))
    y = (jnp.dot(mix[0], wo_ref[0:half, :], preferred_element_type=F32)
         + jnp.dot(mix[1], wo_ref[half:D_MODEL, :], preferred_element_type=F32))
    x = x_ref[...] + y

    rn = lax.rsqrt(jnp.mean(x * x, axis=-1, keepdims=True) + EPS)
    h = (x * rn * g2_ref[...]).astype(BF16)
    y2 = None
    for f0 in range(0, D_FF, FF_CHUNK):
        a = jnp.dot(h, w1_ref[:, f0:f0 + FF_CHUNK], preferred_element_type=F32)
        a = jnp.square(jnp.maximum(a, 0.0)).astype(BF16)
        t = jnp.dot(a, w2_ref[f0:f0 + FF_CHUNK, :], preferred_element_type=F32)
        y2 = t if y2 is None else y2 + t
    out_ref[...] = x + y2


def _merge_ffn(x2d, c2d, o_all, lse_all, u2d, wc, wa, wo, g2, w1, w2, layer, m_rows):
    tm = MIX_ROWS
    resident = pl.Buffered(1)
    row = lambda width: pl.BlockSpec((tm, width), lambda i: (i, 0))
    grp = lambda g: pl.BlockSpec((None, tm, GROUP_DIM), functools.partial(lambda i, g: (g, i, 0), g=g))
    groups = [grp(g) for g in range(N_GROUPS)]
    gate = lambda tile: pl.BlockSpec((tm, COL_TILE), lambda i: (i, tile))
    wspec = lambda k, n: pl.BlockSpec((None, k, n), lambda i: (layer, 0, 0), pipeline_mode=resident)
    gate0 = 3 * ATTN_DIM // COL_TILE
    return pl.pallas_call(
        _merge_ffn_kernel,
        grid=(m_rows // tm,),
        in_specs=[row(D_MODEL), row(CONV_DIM)] + groups + groups
                 + [gate(gate0), gate(gate0 + 1), gate(gate0 + 2), gate(gate0 + 3)]
                 + [wspec(CONV_DIM, D_MODEL), wspec(GROUP_DIM, D_MODEL), wspec(D_MODEL, D_MODEL),
                    pl.BlockSpec((None, 1, D_MODEL), lambda i: (layer, 0, 0)),
                    wspec(D_MODEL, D_FF), wspec(D_FF, D_MODEL)],
        out_specs=row(D_MODEL),
        out_shape=jax.ShapeDtypeStruct((m_rows, D_MODEL), F32),
        compiler_params=_cparams(("arbitrary",)),
        name="merge_ffn",
    )(x2d, c2d, *([o_all] * N_GROUPS), *([lse_all] * N_GROUPS), u2d, u2d, u2d, u2d, wc, wa, wo, g2, w1, w2)


def _t5_bucket(dist):
    max_exact = NUM_BUCKETS // 2
    nf = jnp.maximum(dist, 1).astype(jnp.float32)
    large = max_exact + (jnp.log(nf / max_exact) / math.log(MAX_REL_DISTANCE / max_exact)
                         * (NUM_BUCKETS - max_exact)).astype(jnp.int32)
    large = jnp.minimum(large, NUM_BUCKETS - 1)
    return jnp.where(dist < max_exact, dist, large)


def _bucket_tile(d):
    qi = jnp.arange(BLOCK)[:, None]
    kj = jnp.arange(2 * BLOCK)[None, :]
    off = qi + BLOCK - kj
    band = (off >= 0) & (off <= SUB_WINDOW)
    bucket = _t5_bucket(jnp.clip(off, 0, SUB_WINDOW) * d)
    return jnp.where(band, bucket, -1).astype(jnp.int32)


def kernel(x, rel_bias, norm1_g, w_in, q_norm_g, k_norm_g, conv_dw_w, conv_dw_b, conv_ln_g, conv_ln_b,
           w_conv_out, w_attn_out, w_out, norm2_g, w_ff1, w_ff2):
    batch, seq, _ = x.shape
    depth = w_in.shape[0]
    m_rows = batch * seq
    assert seq % (BLOCK * max(DILATIONS)) == 0 and x.shape[2] == D_MODEL

    w_conv_b, w_attn_b, w_out_b = (w.astype(BF16) for w in (w_conv_out, w_attn_out, w_out))
    w_ff1_b, w_ff2_b = w_ff1.astype(BF16), w_ff2.astype(BF16)
    head_of_col = jnp.arange(GROUP_DIM) // HEAD_DIM
    even_head = (head_of_col % 2 == 0).astype(F32)[None, :]
    qg_all = jnp.tile(q_norm_g, (1, HEADS_PER_GROUP)) * (HEAD_DIM ** -0.5)
    qge_all, qgo_all = qg_all * even_head, qg_all * (1.0 - even_head)
    kg_all = jnp.tile(k_norm_g, (1, HEADS_PER_GROUP))
    head_of_half = head_of_col[:GROUP_DIM // 2]
    bsum = ((head_of_half[:, None] == head_of_half[None, :]).astype(F32) * (1.0 / HEAD_DIM)).astype(BF16)
    bkts = jnp.stack([_bucket_tile(d) for d in DILATIONS])
    rbs = rel_bias.astype(F32).reshape(NUM_BUCKETS, N_GROUPS, HEADS_PER_GROUP).transpose(1, 0, 2).reshape(-1)
    bias = _bias_tiles(bkts, rbs)

    vec3 = lambda a: a.reshape(depth, 1, a.shape[-1])
    norm1_g, norm2_g, conv_dw_b, conv_ln_g, conv_ln_b = (vec3(a) for a in (norm1_g, norm2_g, conv_dw_b, conv_ln_g, conv_ln_b))
    x2d = x.reshape(m_rows, D_MODEL)
    for layer in range(depth):
        u2d, c2d = _in_proj(x2d, norm1_g, w_in, conv_dw_w, conv_dw_b, conv_ln_g, conv_ln_b, layer, batch, seq)
        u3 = u2d.reshape(batch, seq, U_COLS)
        sl = slice(layer, layer + 1)
        o_all, lse_all = _attn_all(u3, bias, bsum, qge_all[sl], qgo_all[sl], kg_all[sl], batch, seq)
        x2d = _merge_ffn(x2d, c2d, o_all.reshape(N_GROUPS, m_rows, GROUP_DIM),
                         lse_all.reshape(N_GROUPS, m_rows, GROUP_DIM), u2d, w_conv_b, w_attn_b, w_out_b,
                         norm2_g, w_ff1_b, w_ff2_b, layer, m_rows)
    return x2d.reshape(batch, seq, D_MODEL)
```

```python
import functools
import math

import jax
import jax.numpy as jnp
from jax import lax
from jax.experimental import pallas as pl
from jax.experimental.pallas import tpu as pltpu

F32 = jnp.float32
BF16 = jnp.bfloat16

D_MODEL = 1024
CONV_DIM = 512
CONV_WIDTH = 31
N_GROUPS = 3
HEADS_PER_GROUP = 8
HEAD_DIM = 64
GROUP_DIM = HEADS_PER_GROUP * HEAD_DIM
ATTN_DIM = N_GROUPS * GROUP_DIM
DILATIONS = (1, 4, 16)
SUB_WINDOW = 128
BLOCK = 128
NUM_BUCKETS = 32
MAX_REL_DISTANCE = 2048
D_FF = 4 * D_MODEL
EPS = 1e-6
NEG_INF = -1e30
IN_COLS = 2 * CONV_DIM + 3 * ATTN_DIM + 2 * D_MODEL

LANES = 128
SUBLANES = 8
COL_TILE = 512
N_COL_TILES = IN_COLS // COL_TILE
N_CONV_COL_TILES = 2 * CONV_DIM // COL_TILE
U_COLS = IN_COLS - 2 * CONV_DIM
N_SLABS = D_MODEL // LANES
N_PAIRS = HEADS_PER_GROUP // 2
BLOCK_LOOP_UNROLL = {1: 2, 4: 2, 16: 7}
D1_SEGMENTS = 4
MIX_ROWS = 512
FF_CHUNK = 2048
CONV_HALO = 32
CONV_TILE = 256
CONV_CHUNK = 128
VMEM_LIMIT = 56 * 1024 * 1024


def _cparams(sem):
    return pltpu.CompilerParams(dimension_semantics=sem, vmem_limit_bytes=VMEM_LIMIT)


def _slab(c):
    return slice(c * LANES, (c + 1) * LANES)


def _in_proj_kernel(*refs, seq):
    x_refs = refs[:N_SLABS]
    (g_ref, w_ref, dw_ref, db_ref, lg_ref, lb_ref, u_ref, c_ref,
     h_ref, rn_ref, tmp_ref, uc_ref, z_ref, acc_ref) = refs[N_SLABS:]
    j = pl.program_id(1)
    rc = 128
    quarter = seq // 4

    @pl.when((pl.program_id(0) == 0) & (j == 0))
    def _():
        uc_ref[...] = jnp.zeros(uc_ref.shape, BF16)
        z_ref[...] = jnp.zeros(z_ref.shape, F32)
        acc_ref[...] = jnp.zeros(acc_ref.shape, F32)

    @pl.when(j == 0)
    def _():
        def natural(ci, carry):
            rows = pl.ds(pl.multiple_of(ci * rc, rc), rc)
            xs = [x_refs[c][rows, :] for c in range(N_SLABS)]
            ss = xs[0] * xs[0]
            for c in range(1, N_SLABS):
                ss = ss + xs[c] * xs[c]
            rn = lax.rsqrt(jnp.sum(ss, axis=-1, keepdims=True) * (1.0 / D_MODEL) + EPS)
            rn_ref[rows, :] = jnp.broadcast_to(rn, (rc, LANES))
            for c in range(N_SLABS):
                h_ref[0, rows, _slab(c)] = (xs[c] * rn * g_ref[:, _slab(c)]).astype(BF16)
            return carry

        lax.fori_loop(0, seq // rc, natural, 0, unroll=4)

        def by_four(r4, carry):
            base = pl.multiple_of(r4 * quarter, quarter)
            for a0 in range(0, quarter, rc):
                src = pl.ds(a0 * 4 + r4, rc, stride=4)
                rn = rn_ref[src, :]
                for c in range(N_SLABS):
                    y = x_refs[c][src, :] * rn * g_ref[:, _slab(c)]
                    h_ref[1, pl.ds(base + a0, rc), _slab(c)] = y.astype(BF16)
                    tmp_ref[c, a0:a0 + rc, :] = y
            for r2 in range(4):
                for c in range(N_SLABS):
                    h_ref[2, pl.ds(base + r2 * rc, rc), _slab(c)] = (
                        tmp_ref[c, pl.ds(r2, rc, stride=4), :].astype(BF16))
            return carry

        lax.fori_loop(0, 4, by_four, 0)

    jj = jnp.clip(j - 2, 0, 8)
    sel = jnp.where((j >= 2) & (j <= 10), lax.rem(jj, 3), 0)

    n_iter = CONV_TILE // CONV_CHUNK
    n_chunks = seq // CONV_CHUNK
    dot_rows = seq // n_iter

    def body(i, carry):
        rows = pl.ds(pl.multiple_of(i * dot_rows, dot_rows), dot_rows)
        u_ref[rows, :] = jnp.dot(h_ref[sel, rows, :], w_ref[...].astype(BF16),
                                 preferred_element_type=F32).astype(BF16)
        tap_chunk = jnp.clip(j - 3, 0, seq // CONV_TILE - 1) * n_iter + i
        glu_chunk = jnp.clip((j - 3) * n_iter + i + 1, 0, n_chunks - 1)
        chunk_rows = pl.ds(pl.multiple_of(i * CONV_CHUNK, CONV_CHUNK), CONV_CHUNK)
        _conv_norm(acc_ref, (j + 1) & 1, chunk_rows, lg_ref, lb_ref, c_ref)
        _conv_taps(z_ref, pl.multiple_of(tap_chunk * CONV_CHUNK, CONV_CHUNK), dw_ref, db_ref,
                   acc_ref, j & 1, chunk_rows)
        _conv_glu(uc_ref, z_ref, pl.multiple_of(glu_chunk * CONV_CHUNK, CONV_CHUNK))
        return carry

    conv_active = (j >= 2) & (j <= seq // CONV_TILE + 3)

    @pl.when(conv_active)
    def _():
        lax.fori_loop(0, n_iter, body, 0)

    @pl.when(jnp.logical_not(conv_active))
    def _():
        u_ref[...] = jnp.dot(h_ref[sel], w_ref[...].astype(BF16), preferred_element_type=F32).astype(BF16)

    @pl.when(j == 0)
    def _():
        uc_ref[:, 0:COL_TILE] = u_ref[...]

    @pl.when(j == 1)
    def _():
        uc_ref[:, COL_TILE:2 * COL_TILE] = u_ref[...]


def _conv_glu(uc_ref, z_ref, t0):
    a = uc_ref[pl.ds(t0, CONV_CHUNK), 0:CONV_DIM].astype(F32)
    gt = uc_ref[pl.ds(t0, CONV_CHUNK), CONV_DIM:2 * CONV_DIM].astype(F32)
    z_ref[pl.ds(t0 + CONV_HALO, CONV_CHUNK), :] = a * jax.nn.sigmoid(gt)


def _conv_norm(acc_ref, slot, chunk_rows, lg_ref, lb_ref, c_ref):
    acc = acc_ref[slot, chunk_rows, :]
    mu = jnp.mean(acc, axis=-1, keepdims=True)
    xc = acc - mu
    y = xc * lax.rsqrt(jnp.mean(xc * xc, axis=-1, keepdims=True) + EPS)
    y = y * lg_ref[...] + lb_ref[...]
    c_ref[chunk_rows, :] = (y * jax.nn.sigmoid(y)).astype(BF16)


def _conv_taps(z_ref, t0, w_ref, b_ref, acc_ref, slot, chunk_rows):
    chunk = CONV_CHUNK
    first_tap = CONV_HALO - (CONV_WIDTH - 1)
    n_win = chunk + CONV_HALO
    for lt in range(CONV_DIM // LANES):
        ls = _slab(lt)
        window = z_ref[pl.ds(t0, n_win), ls]
        acc = None
        for rho in range(SUBLANES):
            offs = [o for o in range(first_tap, first_tap + CONV_WIDTH) if o % SUBLANES == rho]
            rolled = window if rho == 0 else pltpu.roll(window, n_win - rho, axis=0)
            terms = [w_ref[o - first_tap:o - first_tap + 1, ls] * rolled[o - rho:o - rho + chunk, :]
                     for o in offs]
            while len(terms) > 1:
                terms = [terms[i] + terms[i + 1] for i in range(0, len(terms) - 1, 2)] + (
                    [terms[-1]] if len(terms) % 2 else [])
            acc = terms[0] if acc is None else acc + terms[0]
        acc_ref[slot, chunk_rows, ls] = acc + b_ref[:, ls]


def _in_proj(x2d, g, w, dw_w, dw_b, ln_g, ln_b, layer, batch, seq):
    assert seq // 16 == 128 and COL_TILE == CONV_DIM
    n_conv_tiles = seq // CONV_TILE
    x_specs = [pl.BlockSpec((seq, LANES),
                            functools.partial(lambda b, j, c: (jnp.minimum(b + jnp.minimum(j, 1), batch - 1), c), c=c))
               for c in range(N_SLABS)]
    vec = lambda: pl.BlockSpec((None, 1, CONV_DIM), lambda b, j: (layer, 0, 0))
    return pl.pallas_call(
        functools.partial(_in_proj_kernel, seq=seq),
        grid=(batch, N_COL_TILES),
        in_specs=x_specs + [
            pl.BlockSpec((None, 1, D_MODEL), lambda b, j: (layer, 0, 0)),
            pl.BlockSpec((None, D_MODEL, COL_TILE), lambda b, j: (layer, 0, j)),
            pl.BlockSpec((None, CONV_WIDTH, CONV_DIM), lambda b, j: (layer, 0, 0)),
            vec(), vec(), vec(),
        ],
        out_specs=[
            pl.BlockSpec((seq, COL_TILE), lambda b, j: (b, jnp.maximum(j - N_CONV_COL_TILES, 0))),
            pl.BlockSpec((CONV_TILE, CONV_DIM),
                         lambda b, j: (b * n_conv_tiles + jnp.clip(j - 4, 0, n_conv_tiles - 1), 0)),
        ],
        out_shape=[jax.ShapeDtypeStruct((batch * seq, U_COLS), BF16),
                   jax.ShapeDtypeStruct((batch * seq, CONV_DIM), BF16)],
        scratch_shapes=[pltpu.VMEM((N_GROUPS, seq, D_MODEL), BF16),
                        pltpu.VMEM((seq, LANES), F32),
                        pltpu.VMEM((N_SLABS, seq // 4, LANES), F32),
                        pltpu.VMEM((seq, 2 * CONV_DIM), BF16),
                        pltpu.VMEM((CONV_HALO + seq, CONV_DIM), F32),
                        pltpu.VMEM((2, CONV_TILE, CONV_DIM), F32)],
        compiler_params=_cparams(("arbitrary", "arbitrary")),
        name="in_proj",
    )(*([x2d] * N_SLABS), g, w, dw_w, dw_b, ln_g, ln_b)


def _attn_kernel(*refs, seq):
    for group, d in enumerate(DILATIONS):
        pl.when(pl.program_id(0) == group)(functools.partial(_attn_body, *refs, seq=seq, d=d, group=group))


def _bias_kernel(bkt_ref, rb_ref, bm_ref):
    group = pl.program_id(0)
    bk = bkt_ref[0]
    for h in range(HEADS_PER_GROUP):
        acc = jnp.full((BLOCK, 2 * BLOCK), NEG_INF, F32)
        for b in range(NUM_BUCKETS):
            acc = jnp.where(bk == b, rb_ref[(group * NUM_BUCKETS + b) * HEADS_PER_GROUP + h], acc)
        bm_ref[0, h // 2, (h % 2) * BLOCK:(h % 2 + 1) * BLOCK, :] = acc


def _bias_tiles(bkts, rbs):
    return pl.pallas_call(
        _bias_kernel,
        grid=(N_GROUPS,),
        in_specs=[pl.BlockSpec((1, BLOCK, 2 * BLOCK), lambda g: (g, 0, 0)),
                  pl.BlockSpec(memory_space=pltpu.SMEM)],
        out_specs=pl.BlockSpec((1, N_PAIRS, 2 * BLOCK, 2 * BLOCK), lambda g: (g, 0, 0, 0)),
        out_shape=jax.ShapeDtypeStruct((N_GROUPS, N_PAIRS, 2 * BLOCK, 2 * BLOCK), F32),
        compiler_params=_cparams(("arbitrary",)),
        name="bias_tiles",
    )(bkts, rbs)


def _attn_body(bm_ref, bsum_ref, qge_ref, qgo_ref, kg_ref, q_ref, k_ref, v_ref,
               o_ref, lse_ref,
               qe_ref, qo_ref, kn_ref, va_ref, s_ref, *stage, seq, d, group):
    sub_len = seq // d
    nb = sub_len // BLOCK
    bm_ref = bm_ref.at[0]

    @pl.when((pl.program_id(0) == 0) & (pl.program_id(1) == 0))
    def _():
        va_ref[:, :, LANES:2 * LANES] = jnp.ones((N_PAIRS, seq, LANES), BF16)

    rc = 256

    def head_mean_sq(t):
        sq = t * t
        half = GROUP_DIM // 2
        return jnp.concatenate(
            [jnp.dot(sq[:, 0:half], bsum_ref[...], preferred_element_type=F32),
             jnp.dot(sq[:, half:GROUP_DIM], bsum_ref[...], preferred_element_type=F32)], axis=1)

    def norm_body(ci, carry):
        rows = pl.ds(pl.multiple_of(ci * rc, rc), rc)
        q = q_ref[0, rows, :]
        qr = q.astype(F32) * lax.rsqrt(head_mean_sq(q) + EPS)
        qe_ref[rows, :] = (qr * qge_ref[...]).astype(BF16)
        qo_ref[rows, :] = (qr * qgo_ref[...]).astype(BF16)
        k = k_ref[0, rows, :]
        kr = k.astype(F32) * lax.rsqrt(head_mean_sq(k) + EPS)
        kn_ref[rows, :] = (kr * kg_ref[...]).astype(BF16)
        for p in range(N_PAIRS):
            va_ref[p, rows, 0:LANES] = v_ref[0, rows, _slab(p)]
        return carry

    lax.fori_loop(0, seq // rc, norm_body, 0, unroll=8)

    lt64 = lax.broadcasted_iota(jnp.int32, (BLOCK, LANES), 1) < HEAD_DIM

    def block_aligned(row):
        return row if isinstance(row, int) else pl.multiple_of(row, BLOCK)

    def key_rows(row0, first):
        return pl.ds(row0, BLOCK) if first else pl.ds(block_aligned(row0 - BLOCK), 2 * BLOCK)

    def score_products(row0, first):
        row0 = block_aligned(row0)
        qrows, krows = pl.ds(row0, BLOCK), key_rows(row0, first)
        out = []
        for p in range(N_PAIRS):
            q2 = jnp.concatenate([qe_ref[qrows, _slab(p)], qo_ref[qrows, _slab(p)]], axis=0)
            out.append(lax.dot_general(q2, kn_ref[krows, _slab(p)], (((1,), (1,)), ((), ())),
                                       preferred_element_type=F32))
        return out

    def width(first):
        return BLOCK if first else 2 * BLOCK

    def park(slot, scores, first):
        for p in range(N_PAIRS):
            s_ref[slot, p, :, 0:width(first)] = scores[p]

    def unpark(slot, first):
        return [s_ref[slot, p, :, 0:width(first)] for p in range(N_PAIRS)]

    def finish_block(scores, row0, t0, first):
        row0 = block_aligned(row0)
        qrows, krows = pl.ds(row0, BLOCK), key_rows(row0, first)
        probs, maxes = [], []
        for p in range(N_PAIRS):
            s = scores[p] + (bm_ref[p, :, BLOCK:2 * BLOCK] if first else bm_ref[p])
            m = jnp.max(s, axis=-1, keepdims=True)
            probs.append(jnp.exp(s - m).astype(BF16))
            maxes.append(m)
        results = [jnp.dot(probs[p], va_ref[p, krows, :], preferred_element_type=F32) for p in range(N_PAIRS)]
        for p in range(N_PAIRS):
            re, ro = results[p][:BLOCK], results[p][BLOCK:]
            me, mo = maxes[p][:BLOCK], maxes[p][BLOCK:]
            denom = jnp.where(lt64, re[:, LANES:], ro[:, LANES:])
            o_pair = jnp.where(lt64, re[:, :LANES], ro[:, :LANES]) / denom
            lse_pair = jnp.where(lt64, me, mo) + jnp.log(denom)
            if d == 1:
                o_ref[0, qrows, _slab(p)] = o_pair.astype(BF16)
                lse_ref[0, qrows, _slab(p)] = lse_pair
            else:
                nat = pl.ds(t0, BLOCK, stride=d)
                stage[0][p, nat, :] = o_pair
                stage[1][p, nat, :] = lse_pair

    def run_chain(base, mode, t0_of, n_blocks):
        def first(n):
            return mode == "all" or (mode == "head" and n == 0)

        def row(n):
            return base + n * BLOCK

        park(0, score_products(row(0), first(0)), first(0))
        park(1, score_products(row(1), first(1)), first(1))
        finish_block(unpark(0, first(0)), row(0), t0_of(0), first(0))

        def blk_body(n, c):
            slot = n & 1
            prev = unpark(1 - slot, first(1))
            park(slot, score_products(row(n), first(2)), first(2))
            finish_block(prev, row(n - 1), t0_of(n - 1), first(1))
            return c

        lax.fori_loop(2, n_blocks, blk_body, 0, unroll=BLOCK_LOOP_UNROLL[d])
        last = n_blocks - 1
        finish_block(unpark(last & 1, first(last)), row(last), t0_of(last), first(last))

    def residue_of_class(c):
        if isinstance(c, int):
            return c // 4 + 4 * (c % 4)
        return lax.shift_right_logical(c, 2) + 4 * (c & 3)

    if d == 1:
        seg_blocks = nb // D1_SEGMENTS
        run_chain(0, "head", lambda n: n * BLOCK, seg_blocks)

        def segment_body(sg, carry):
            base = pl.multiple_of(sg * (seg_blocks * BLOCK), BLOCK)
            run_chain(base, "none", lambda n: base + n * BLOCK, seg_blocks)
            return carry

        lax.fori_loop(1, D1_SEGMENTS, segment_body, 0, unroll=D1_SEGMENTS - 1)
    elif nb > 1:
        def class_body(r, carry):
            run_chain(pl.multiple_of(r * sub_len, BLOCK), "head", lambda n: n * (BLOCK * d) + r, nb)
            return carry

        lax.fori_loop(0, d, class_body, 0, unroll=4)
    else:
        def class_group(cg, carry):
            run_chain(pl.multiple_of(cg * (4 * BLOCK), BLOCK), "all", lambda n: residue_of_class(cg * 4 + n), 4)
            return carry

        lax.fori_loop(0, d // 4, class_group, 0, unroll=2)

    if d > 1:
        def copy_body(ci, carry):
            rows = pl.ds(pl.multiple_of(ci * rc, rc), rc)
            for p in range(N_PAIRS):
                o_ref[0, rows, _slab(p)] = stage[0][p, rows, :].astype(BF16)
                lse_ref[0, rows, _slab(p)] = stage[1][p, rows, :]
            return carry

        lax.fori_loop(0, seq // rc, copy_body, 0, unroll=4)


def _attn_all(u3, bias, bsum, qge, qgo, kg, batch, seq):
    assert DILATIONS == (1, 4, 16) and seq // max(DILATIONS) == BLOCK
    const = lambda shape: pl.BlockSpec(shape, lambda g, b: (0,) * len(shape))
    col = lambda first_tile: pl.BlockSpec((1, seq, GROUP_DIM), lambda g, b: (b, 0, first_tile + g))
    scratch = [
        pltpu.VMEM((seq, GROUP_DIM), BF16),
        pltpu.VMEM((seq, GROUP_DIM), BF16),
        pltpu.VMEM((seq, GROUP_DIM), BF16),
        pltpu.VMEM((N_PAIRS, seq, 2 * LANES), BF16),
        pltpu.VMEM((2, N_PAIRS, 2 * BLOCK, 2 * BLOCK), F32),
        pltpu.VMEM((N_PAIRS, seq, LANES), F32),
        pltpu.VMEM((N_PAIRS, seq, LANES), F32),
    ]
    return pl.pallas_call(
        functools.partial(_attn_kernel, seq=seq),
        grid=(N_GROUPS, batch),
        in_specs=[
            pl.BlockSpec((1, N_PAIRS, 2 * BLOCK, 2 * BLOCK), lambda g, b: (g, 0, 0, 0)),
            const((GROUP_DIM // 2, GROUP_DIM // 2)),
            const((1, GROUP_DIM)), const((1, GROUP_DIM)), const((1, GROUP_DIM)),
            col(0), col(N_GROUPS), col(2 * N_GROUPS),
        ],
        out_specs=[pl.BlockSpec((None, 1, seq, GROUP_DIM), lambda g, b: (g, b, 0, 0))] * 2,
        out_shape=[jax.ShapeDtypeStruct((N_GROUPS, batch, seq, GROUP_DIM), BF16),
                   jax.ShapeDtypeStruct((N_GROUPS, batch, seq, GROUP_DIM), F32)],
        scratch_shapes=scratch,
        compiler_params=_cparams(("arbitrary", "arbitrary")),
        name="attn",
    )(bias, bsum, qge, qgo, kg, u3, u3, u3)


def _merge_ffn_kernel(x_ref, c_ref, o0_ref, o1_ref, o2_ref, l0_ref, l1_ref, l2_ref,
                      gc0_ref, gc1_ref, ga0_ref, ga1_ref, wc_ref, wa_ref, wo_ref,
                      g2_ref, w1_ref, w2_ref, out_ref):
    l0, l1, l2 = l0_ref[...], l1_ref[...], l2_ref[...]
    m = jnp.maximum(jnp.maximum(l0, l1), l2)
    e0, e1, e2 = jnp.exp(l0 - m), jnp.exp(l1 - m), jnp.exp(l2 - m)
    o = (e0 * o0_ref[...].astype(F32) + e1 * o1_ref[...].astype(F32)
         + e2 * o2_ref[...].astype(F32)) / (e0 + e1 + e2)
    y_attn = jnp.dot(o.astype(BF16), wa_ref[...], preferred_element_type=F32)
    y_conv = jnp.dot(c_ref[...], wc_ref[...], preferred_element_type=F32)
    half = D_MODEL // 2
    mix = []
    for hs, gc_ref, ga_ref in ((slice(0, half), gc0_ref, ga0_ref), (slice(half, D_MODEL), gc1_ref, ga1_ref)):
        gc = jax.nn.sigmoid(gc_ref[...].astype(F32))
        ga = jax.nn.sigmoid(ga_ref[...].astype(F32))
        mix.append((gc * y_conv[:, hs] + ga * y_attn[:, hs]).astype(BF16))
    y = (jnp.dot(mix[0], wo_ref[0:half, :], preferred_element_type=F32)
         + jnp.dot(mix[1], wo_ref[half:D_MODEL, :], preferred_element_type=F32))
    x = x_ref[...] + y

    rn = lax.rsqrt(jnp.mean(x * x, axis=-1, keepdims=True) + EPS)
    h = (x * rn * g2_ref[...]).astype(BF16)
    y2 = None
    for f0 in range(0, D_FF, FF_CHUNK):
        a = jnp.dot(h, w1_ref[:, f0:f0 + FF_CHUNK], preferred_element_type=F32)
        a = jnp.square(jnp.maximum(a, 0.0)).astype(BF16)
        t = jnp.dot(a, w2_ref[f0:f0 + FF_CHUNK, :], preferred_element_type=F32)
        y2 = t if y2 is None else y2 + t
    out_ref[...] = x + y2


def _merge_ffn(x2d, c2d, o_all, lse_all, u2d, wc, wa, wo, g2, w1, w2, layer, m_rows):
    tm = MIX_ROWS
    resident = pl.Buffered(1)
    row = lambda width: pl.BlockSpec((tm, width), lambda i: (i, 0))
    grp = lambda g: pl.BlockSpec((None, tm, GROUP_DIM), functools.partial(lambda i, g: (g, i, 0), g=g))
    groups = [grp(g) for g in range(N_GROUPS)]
    gate = lambda tile: pl.BlockSpec((tm, COL_TILE), lambda i: (i, tile))
    wspec = lambda k, n: pl.BlockSpec((None, k, n), lambda i: (layer, 0, 0), pipeline_mode=resident)
    gate0 = 3 * ATTN_DIM // COL_TILE
    return pl.pallas_call(
        _merge_ffn_kernel,
        grid=(m_rows // tm,),
        in_specs=[row(D_MODEL), row(CONV_DIM)] + groups + groups
                 + [gate(gate0), gate(gate0 + 1), gate(gate0 + 2), gate(gate0 + 3)]
                 + [wspec(CONV_DIM, D_MODEL), wspec(GROUP_DIM, D_MODEL), wspec(D_MODEL, D_MODEL),
                    pl.BlockSpec((None, 1, D_MODEL), lambda i: (layer, 0, 0)),
                    wspec(D_MODEL, D_FF), wspec(D_FF, D_MODEL)],
        out_specs=row(D_MODEL),
        out_shape=jax.ShapeDtypeStruct((m_rows, D_MODEL), F32),
        compiler_params=_cparams(("arbitrary",)),
        name="merge_ffn",
    )(x2d, c2d, *([o_all] * N_GROUPS), *([lse_all] * N_GROUPS), u2d, u2d, u2d, u2d, wc, wa, wo, g2, w1, w2)


def _t5_bucket(dist):
    max_exact = NUM_BUCKETS // 2
    nf = jnp.maximum(dist, 1).astype(jnp.float32)
    large = max_exact + (jnp.log(nf / max_exact) / math.log(MAX_REL_DISTANCE / max_exact)
                         * (NUM_BUCKETS - max_exact)).astype(jnp.int32)
    large = jnp.minimum(large, NUM_BUCKETS - 1)
    return jnp.where(dist < max_exact, dist, large)


def _bucket_tile(d):
    qi = jnp.arange(BLOCK)[:, None]
    kj = jnp.arange(2 * BLOCK)[None, :]
    off = qi + BLOCK - kj
    band = (off >= 0) & (off <= SUB_WINDOW)
    bucket = _t5_bucket(jnp.clip(off, 0, SUB_WINDOW) * d)
    return jnp.where(band, bucket, -1).astype(jnp.int32)


def kernel(x, rel_bias, norm1_g, w_in, q_norm_g, k_norm_g, conv_dw_w, conv_dw_b, conv_ln_g, conv_ln_b,
           w_conv_out, w_attn_out, w_out, norm2_g, w_ff1, w_ff2):
    batch, seq, _ = x.shape
    depth = w_in.shape[0]
    m_rows = batch * seq
    assert seq % (BLOCK * max(DILATIONS)) == 0 and x.shape[2] == D_MODEL

    w_conv_b, w_attn_b, w_out_b = (w.astype(BF16) for w in (w_conv_out, w_attn_out, w_out))
    w_ff1_b, w_ff2_b = w_ff1.astype(BF16), w_ff2.astype(BF16)
    head_of_col = jnp.arange(GROUP_DIM) // HEAD_DIM
    even_head = (head_of_col % 2 == 0).astype(F32)[None, :]
    qg_all = jnp.tile(q_norm_g, (1, HEADS_PER_GROUP)) * (HEAD_DIM ** -0.5)
    qge_all, qgo_all = qg_all * even_head, qg_all * (1.0 - even_head)
    kg_all = jnp.tile(k_norm_g, (1, HEADS_PER_GROUP))
    head_of_half = head_of_col[:GROUP_DIM // 2]
    bsum = ((head_of_half[:, None] == head_of_half[None, :]).astype(F32) * (1.0 / HEAD_DIM)).astype(BF16)
    bkts = jnp.stack([_bucket_tile(d) for d in DILATIONS])
    rbs = rel_bias.astype(F32).reshape(NUM_BUCKETS, N_GROUPS, HEADS_PER_GROUP).transpose(1, 0, 2).reshape(-1)
    bias = _bias_tiles(bkts, rbs)

    vec3 = lambda a: a.reshape(depth, 1, a.shape[-1])
    norm1_g, norm2_g, conv_dw_b, conv_ln_g, conv_ln_b = (vec3(a) for a in (norm1_g, norm2_g, conv_dw_b, conv_ln_g, conv_ln_b))
    x2d = x.reshape(m_rows, D_MODEL)
    for layer in range(depth):
        u2d, c2d = _in_proj(x2d, norm1_g, w_in, conv_dw_w, conv_dw_b, conv_ln_g, conv_ln_b, layer, batch, seq)
        u3 = u2d.reshape(batch, seq, U_COLS)
        sl = slice(layer, layer + 1)
        o_all, lse_all = _attn_all(u3, bias, bsum, qge_all[sl], qgo_all[sl], kg_all[sl], batch, seq)
        x2d = _merge_ffn(x2d, c2d, o_all.reshape(N_GROUPS, m_rows, GROUP_DIM),
                         lse_all.reshape(N_GROUPS, m_rows, GROUP_DIM), u2d, w_conv_b, w_attn_b, w_out_b,
                         norm2_g, w_ff1_b, w_ff2_b, layer, m_rows)
    return x2d.reshape(batch, seq, D_MODEL)
```

```python
import functools
import math

import jax
import jax.numpy as jnp
from jax import lax
from jax.experimental import pallas as pl
from jax.experimental.pallas import tpu as pltpu

F32 = jnp.float32
BF16 = jnp.bfloat16

D_MODEL = 1024
CONV_DIM = 512
CONV_WIDTH = 31
N_GROUPS = 3
HEADS_PER_GROUP = 8
HEAD_DIM = 64
GROUP_DIM = HEADS_PER_GROUP * HEAD_DIM
ATTN_DIM = N_GROUPS * GROUP_DIM
DILATIONS = (1, 4, 16)
SUB_WINDOW = 128
BLOCK = 128
NUM_BUCKETS = 32
MAX_REL_DISTANCE = 2048
D_FF = 4 * D_MODEL
EPS = 1e-6
NEG_INF = -1e30
IN_COLS = 2 * CONV_DIM + 3 * ATTN_DIM + 2 * D_MODEL

LANES = 128
SUBLANES = 8
COL_TILE = 512
N_COL_TILES = IN_COLS // COL_TILE
N_CONV_COL_TILES = 2 * CONV_DIM // COL_TILE
U_COLS = IN_COLS - 2 * CONV_DIM
N_SLABS = D_MODEL // LANES
N_PAIRS = HEADS_PER_GROUP // 2
BLOCK_LOOP_UNROLL = {1: 2, 4: 2, 16: 7}
D1_SEGMENTS = 4
MIX_ROWS = 512
FF_CHUNK = 2048
CONV_HALO = 32
CONV_TILE = 256
CONV_CHUNK = 128
VMEM_LIMIT = 56 * 1024 * 1024


def _cparams(sem):
    return pltpu.CompilerParams(dimension_semantics=sem, vmem_limit_bytes=VMEM_LIMIT)


def _slab(c):
    return slice(c * LANES, (c + 1) * LANES)


def _in_proj_kernel(*refs, seq):
    x_refs = refs[:N_SLABS]
    (g_ref, w_ref, dw_ref, db_ref, lg_ref, lb_ref, u_ref, c_ref,
     h_ref, rn_ref, tmp_ref, uc_ref, z_ref, acc_ref) = refs[N_SLABS:]
    j = pl.program_id(1)
    rc = 128
    quarter = seq // 4

    @pl.when((pl.program_id(0) == 0) & (j == 0))
    def _():
        uc_ref[...] = jnp.zeros(uc_ref.shape, BF16)
        z_ref[...] = jnp.zeros(z_ref.shape, F32)
        acc_ref[...] = jnp.zeros(acc_ref.shape, F32)

    @pl.when(j == 0)
    def _():
        def natural(ci, carry):
            rows = pl.ds(pl.multiple_of(ci * rc, rc), rc)
            xs = [x_refs[c][rows, :] for c in range(N_SLABS)]
            ss = xs[0] * xs[0]
            for c in range(1, N_SLABS):
                ss = ss + xs[c] * xs[c]
            rn = lax.rsqrt(jnp.sum(ss, axis=-1, keepdims=True) * (1.0 / D_MODEL) + EPS)
            rn_ref[rows, :] = jnp.broadcast_to(rn, (rc, LANES))
            for c in range(N_SLABS):
                h_ref[0, rows, _slab(c)] = (xs[c] * rn * g_ref[:, _slab(c)]).astype(BF16)
            return carry

        lax.fori_loop(0, seq // rc, natural, 0, unroll=8)

        def by_four(r4, carry):
            base = pl.multiple_of(r4 * quarter, quarter)
            for a0 in range(0, quarter, rc):
                src = pl.ds(a0 * 4 + r4, rc, stride=4)
                rn = rn_ref[src, :]
                for c in range(N_SLABS):
                    y = x_refs[c][src, :] * rn * g_ref[:, _slab(c)]
                    h_ref[1, pl.ds(base + a0, rc), _slab(c)] = y.astype(BF16)
                    tmp_ref[c, a0:a0 + rc, :] = y
            for r2 in range(4):
                for c in range(N_SLABS):
                    h_ref[2, pl.ds(base + r2 * rc, rc), _slab(c)] = (
                        tmp_ref[c, pl.ds(r2, rc, stride=4), :].astype(BF16))
            return carry

        lax.fori_loop(0, 4, by_four, 0, unroll=2)

    jj = jnp.clip(j - 2, 0, 8)
    sel = jnp.where((j >= 2) & (j <= 10), lax.rem(jj, 3), 0)

    n_iter = CONV_TILE // CONV_CHUNK
    n_chunks = seq // CONV_CHUNK
    dot_rows = seq // n_iter

    def body(i, carry):
        rows = pl.ds(pl.multiple_of(i * dot_rows, dot_rows), dot_rows)
        u_ref[rows, :] = jnp.dot(h_ref[sel, rows, :], w_ref[...].astype(BF16),
                                 preferred_element_type=F32).astype(BF16)
        tap_chunk = jnp.clip(j - 3, 0, seq // CONV_TILE - 1) * n_iter + i
        glu_chunk = jnp.clip((j - 3) * n_iter + i + 1, 0, n_chunks - 1)
        chunk_rows = pl.ds(pl.multiple_of(i * CONV_CHUNK, CONV_CHUNK), CONV_CHUNK)
        _conv_norm(acc_ref, (j + 1) & 1, chunk_rows, lg_ref, lb_ref, c_ref)
        _conv_taps(z_ref, pl.multiple_of(tap_chunk * CONV_CHUNK, CONV_CHUNK), dw_ref, db_ref,
                   acc_ref, j & 1, chunk_rows)
        _conv_glu(uc_ref, z_ref, pl.multiple_of(glu_chunk * CONV_CHUNK, CONV_CHUNK))
        return carry

    conv_active = (j >= 2) & (j <= seq // CONV_TILE + 3)

    @pl.when(conv_active)
    def _():
        lax.fori_loop(0, n_iter, body, 0)

    @pl.when(jnp.logical_not(conv_active))
    def _():
        u_ref[...] = jnp.dot(h_ref[sel], w_ref[...].astype(BF16), preferred_element_type=F32).astype(BF16)

    @pl.when(j == 0)
    def _():
        uc_ref[:, 0:COL_TILE] = u_ref[...]

    @pl.when(j == 1)
    def _():
        uc_ref[:, COL_TILE:2 * COL_TILE] = u_ref[...]


def _conv_glu(uc_ref, z_ref, t0):
    a = uc_ref[pl.ds(t0, CONV_CHUNK), 0:CONV_DIM].astype(F32)
    gt = uc_ref[pl.ds(t0, CONV_CHUNK), CONV_DIM:2 * CONV_DIM].astype(F32)
    z_ref[pl.ds(t0 + CONV_HALO, CONV_CHUNK), :] = a * jax.nn.sigmoid(gt)


def _conv_norm(acc_ref, slot, chunk_rows, lg_ref, lb_ref, c_ref):
    acc = acc_ref[slot, chunk_rows, :]
    mu = jnp.mean(acc, axis=-1, keepdims=True)
    xc = acc - mu
    y = xc * lax.rsqrt(jnp.mean(xc * xc, axis=-1, keepdims=True) + EPS)
    y = y * lg_ref[...] + lb_ref[...]
    c_ref[chunk_rows, :] = (y * jax.nn.sigmoid(y)).astype(BF16)


def _conv_taps(z_ref, t0, w_ref, b_ref, acc_ref, slot, chunk_rows):
    chunk = CONV_CHUNK
    first_tap = CONV_HALO - (CONV_WIDTH - 1)
    n_win = chunk + CONV_HALO
    for lt in range(CONV_DIM // LANES):
        ls = _slab(lt)
        window = z_ref[pl.ds(t0, n_win), ls]
        acc = None
        for rho in range(SUBLANES):
            offs = [o for o in range(first_tap, first_tap + CONV_WIDTH) if o % SUBLANES == rho]
            rolled = window if rho == 0 else pltpu.roll(window, n_win - rho, axis=0)
            terms = [w_ref[o - first_tap:o - first_tap + 1, ls] * rolled[o - rho:o - rho + chunk, :]
                     for o in offs]
            while len(terms) > 1:
                terms = [terms[i] + terms[i + 1] for i in range(0, len(terms) - 1, 2)] + (
                    [terms[-1]] if len(terms) % 2 else [])
            acc = terms[0] if acc is None else acc + terms[0]
        acc_ref[slot, chunk_rows, ls] = acc + b_ref[:, ls]


def _in_proj(x2d, g, w, dw_w, dw_b, ln_g, ln_b, layer, batch, seq):
    assert seq // 16 == 128 and COL_TILE == CONV_DIM
    n_conv_tiles = seq // CONV_TILE
    x_specs = [pl.BlockSpec((seq, LANES),
                            functools.partial(lambda b, j, c: (jnp.minimum(b + jnp.minimum(j, 1), batch - 1), c), c=c))
               for c in range(N_SLABS)]
    vec = lambda: pl.BlockSpec((None, 1, CONV_DIM), lambda b, j: (layer, 0, 0))
    return pl.pallas_call(
        functools.partial(_in_proj_kernel, seq=seq),
        grid=(batch, N_COL_TILES),
        in_specs=x_specs + [
            pl.BlockSpec((None, 1, D_MODEL), lambda b, j: (layer, 0, 0)),
            pl.BlockSpec((None, D_MODEL, COL_TILE), lambda b, j: (layer, 0, j)),
            pl.BlockSpec((None, CONV_WIDTH, CONV_DIM), lambda b, j: (layer, 0, 0)),
            vec(), vec(), vec(),
        ],
        out_specs=[
            pl.BlockSpec((seq, COL_TILE), lambda b, j: (b, jnp.maximum(j - N_CONV_COL_TILES, 0))),
            pl.BlockSpec((CONV_TILE, CONV_DIM),
                         lambda b, j: (b * n_conv_tiles + jnp.clip(j - 4, 0, n_conv_tiles - 1), 0)),
        ],
        out_shape=[jax.ShapeDtypeStruct((batch * seq, U_COLS), BF16),
                   jax.ShapeDtypeStruct((batch * seq, CONV_DIM), BF16)],
        scratch_shapes=[pltpu.VMEM((N_GROUPS, seq, D_MODEL), BF16),
                        pltpu.VMEM((seq, LANES), F32),
                        pltpu.VMEM((N_SLABS, seq // 4, LANES), F32),
                        pltpu.VMEM((seq, 2 * CONV_DIM), BF16),
                        pltpu.VMEM((CONV_HALO + seq, CONV_DIM), F32),
                        pltpu.VMEM((2, CONV_TILE, CONV_DIM), F32)],
        compiler_params=_cparams(("arbitrary", "arbitrary")),
        name="in_proj",
    )(*([x2d] * N_SLABS), g, w, dw_w, dw_b, ln_g, ln_b)


def _attn_kernel(*refs, seq):
    for group, d in enumerate(DILATIONS):
        pl.when(pl.program_id(0) == group)(functools.partial(_attn_body, *refs, seq=seq, d=d, group=group))


def _bias_kernel(bkt_ref, rb_ref, bm_ref):
    group = pl.program_id(0)
    bk = bkt_ref[0]
    for h in range(HEADS_PER_GROUP):
        acc = jnp.full((BLOCK, 2 * BLOCK), NEG_INF, F32)
        for b in range(NUM_BUCKETS):
            acc = jnp.where(bk == b, rb_ref[(group * NUM_BUCKETS + b) * HEADS_PER_GROUP + h], acc)
        bm_ref[0, h // 2, (h % 2) * BLOCK:(h % 2 + 1) * BLOCK, :] = acc


def _bias_tiles(bkts, rbs):
    return pl.pallas_call(
        _bias_kernel,
        grid=(N_GROUPS,),
        in_specs=[pl.BlockSpec((1, BLOCK, 2 * BLOCK), lambda g: (g, 0, 0)),
                  pl.BlockSpec(memory_space=pltpu.SMEM)],
        out_specs=pl.BlockSpec((1, N_PAIRS, 2 * BLOCK, 2 * BLOCK), lambda g: (g, 0, 0, 0)),
        out_shape=jax.ShapeDtypeStruct((N_GROUPS, N_PAIRS, 2 * BLOCK, 2 * BLOCK), F32),
        compiler_params=_cparams(("arbitrary",)),
        name="bias_tiles",
    )(bkts, rbs)


def _attn_body(bm_ref, bsum_ref, qge_ref, qgo_ref, kg_ref, q_ref, k_ref, v_ref,
               o_ref, lse_ref,
               qe_ref, qo_ref, kn_ref, va_ref, s_ref, *stage, seq, d, group):
    sub_len = seq // d
    nb = sub_len // BLOCK
    bm_ref = bm_ref.at[0]

    @pl.when((pl.program_id(0) == 0) & (pl.program_id(1) == 0))
    def _():
        va_ref[:, :, LANES:2 * LANES] = jnp.ones((N_PAIRS, seq, LANES), BF16)

    rc = 256

    def head_mean_sq(t):
        sq = t * t
        half = GROUP_DIM // 2
        return jnp.concatenate(
            [jnp.dot(sq[:, 0:half], bsum_ref[...], preferred_element_type=F32),
             jnp.dot(sq[:, half:GROUP_DIM], bsum_ref[...], preferred_element_type=F32)], axis=1)

    def norm_body(ci, carry):
        rows = pl.ds(pl.multiple_of(ci * rc, rc), rc)
        q = q_ref[0, rows, :]
        qr = q.astype(F32) * lax.rsqrt(head_mean_sq(q) + EPS)
        qe_ref[rows, :] = (qr * qge_ref[...]).astype(BF16)
        qo_ref[rows, :] = (qr * qgo_ref[...]).astype(BF16)
        k = k_ref[0, rows, :]
        kr = k.astype(F32) * lax.rsqrt(head_mean_sq(k) + EPS)
        kn_ref[rows, :] = (kr * kg_ref[...]).astype(BF16)
        for p in range(N_PAIRS):
            va_ref[p, rows, 0:LANES] = v_ref[0, rows, _slab(p)]
        return carry

    lax.fori_loop(0, seq // rc, norm_body, 0, unroll=8)

    lt64 = lax.broadcasted_iota(jnp.int32, (BLOCK, LANES), 1) < HEAD_DIM

    def block_aligned(row):
        return row if isinstance(row, int) else pl.multiple_of(row, BLOCK)

    def key_rows(row0, first):
        return pl.ds(row0, BLOCK) if first else pl.ds(block_aligned(row0 - BLOCK), 2 * BLOCK)

    def score_products(row0, first):
        row0 = block_aligned(row0)
        qrows, krows = pl.ds(row0, BLOCK), key_rows(row0, first)
        out = []
        for p in range(N_PAIRS):
            q2 = jnp.concatenate([qe_ref[qrows, _slab(p)], qo_ref[qrows, _slab(p)]], axis=0)
            out.append(lax.dot_general(q2, kn_ref[krows, _slab(p)], (((1,), (1,)), ((), ())),
                                       preferred_element_type=F32))
        return out

    def width(first):
        return BLOCK if first else 2 * BLOCK

    def park(slot, scores, first):
        for p in range(N_PAIRS):
            s_ref[slot, p, :, 0:width(first)] = scores[p]

    def unpark(slot, first):
        return [s_ref[slot, p, :, 0:width(first)] for p in range(N_PAIRS)]

    def finish_block(scores, row0, t0, first):
        row0 = block_aligned(row0)
        qrows, krows = pl.ds(row0, BLOCK), key_rows(row0, first)
        probs, maxes = [], []
        for p in range(N_PAIRS):
            s = scores[p] + (bm_ref[p, :, BLOCK:2 * BLOCK] if first else bm_ref[p])
            m = jnp.max(s, axis=-1, keepdims=True)
            probs.append(jnp.exp(s - m).astype(BF16))
            maxes.append(m)
        results = [jnp.dot(probs[p], va_ref[p, krows, :], preferred_element_type=F32) for p in range(N_PAIRS)]
        for p in range(N_PAIRS):
            re, ro = results[p][:BLOCK], results[p][BLOCK:]
            me, mo = maxes[p][:BLOCK], maxes[p][BLOCK:]
            denom = jnp.where(lt64, re[:, LANES:], ro[:, LANES:])
            o_pair = jnp.where(lt64, re[:, :LANES], ro[:, :LANES]) / denom
            lse_pair = jnp.where(lt64, me, mo) + jnp.log(denom)
            if d == 1:
                o_ref[0, qrows, _slab(p)] = o_pair.astype(BF16)
                lse_ref[0, qrows, _slab(p)] = lse_pair
            else:
                nat = pl.ds(t0, BLOCK, stride=d)
                stage[0][p, nat, :] = o_pair
                stage[1][p, nat, :] = lse_pair

    def run_chain(base, mode, t0_of, n_blocks):
        def first(n):
            return mode == "all" or (mode == "head" and n == 0)

        def row(n):
            return base + n * BLOCK

        park(0, score_products(row(0), first(0)), first(0))
        park(1, score_products(row(1), first(1)), first(1))
        finish_block(unpark(0, first(0)), row(0), t0_of(0), first(0))

        def blk_body(n, c):
            slot = n & 1
            prev = unpark(1 - slot, first(1))
            park(slot, score_products(row(n), first(2)), first(2))
            finish_block(prev, row(n - 1), t0_of(n - 1), first(1))
            return c

        lax.fori_loop(2, n_blocks, blk_body, 0, unroll=BLOCK_LOOP_UNROLL[d])
        last = n_blocks - 1
        finish_block(unpark(last & 1, first(last)), row(last), t0_of(last), first(last))

    def residue_of_class(c):
        if isinstance(c, int):
            return c // 4 + 4 * (c % 4)
        return lax.shift_right_logical(c, 2) + 4 * (c & 3)

    if d == 1:
        seg_blocks = nb // D1_SEGMENTS
        run_chain(0, "head", lambda n: n * BLOCK, seg_blocks)

        def segment_body(sg, carry):
            base = pl.multiple_of(sg * (seg_blocks * BLOCK), BLOCK)
            run_chain(base, "none", lambda n: base + n * BLOCK, seg_blocks)
            return carry

        lax.fori_loop(1, D1_SEGMENTS, segment_body, 0, unroll=D1_SEGMENTS - 1)
    elif nb > 1:
        def class_body(r, carry):
            run_chain(pl.multiple_of(r * sub_len, BLOCK), "head", lambda n: n * (BLOCK * d) + r, nb)
            return carry

        lax.fori_loop(0, d, class_body, 0, unroll=4)
    else:
        def class_group(cg, carry):
            run_chain(pl.multiple_of(cg * (4 * BLOCK), BLOCK), "all", lambda n: residue_of_class(cg * 4 + n), 4)
            return carry

        lax.fori_loop(0, d // 4, class_group, 0, unroll=4)

    if d > 1:
        def copy_body(ci, carry):
            rows = pl.ds(pl.multiple_of(ci * rc, rc), rc)
            for p in range(N_PAIRS):
                o_ref[0, rows, _slab(p)] = stage[0][p, rows, :].astype(BF16)
                lse_ref[0, rows, _slab(p)] = stage[1][p, rows, :]
            return carry

        lax.fori_loop(0, seq // rc, copy_body, 0, unroll=4)


def _attn_all(u3, bias, bsum, qge, qgo, kg, batch, seq):
    assert DILATIONS == (1, 4, 16) and seq // max(DILATIONS) == BLOCK
    const = lambda shape: pl.BlockSpec(shape, lambda g, b: (0,) * len(shape))
    col = lambda first_tile: pl.BlockSpec((1, seq, GROUP_DIM), lambda g, b: (b, 0, first_tile + g))
    scratch = [
        pltpu.VMEM((seq, GROUP_DIM), BF16),
        pltpu.VMEM((seq, GROUP_DIM), BF16),
        pltpu.VMEM((seq, GROUP_DIM), BF16),
        pltpu.VMEM((N_PAIRS, seq, 2 * LANES), BF16),
        pltpu.VMEM((2, N_PAIRS, 2 * BLOCK, 2 * BLOCK), F32),
        pltpu.VMEM((N_PAIRS, seq, LANES), F32),
        pltpu.VMEM((N_PAIRS, seq, LANES), F32),
    ]
    return pl.pallas_call(
        functools.partial(_attn_kernel, seq=seq),
        grid=(N_GROUPS, batch),
        in_specs=[
            pl.BlockSpec((1, N_PAIRS, 2 * BLOCK, 2 * BLOCK), lambda g, b: (g, 0, 0, 0)),
            const((GROUP_DIM // 2, GROUP_DIM // 2)),
            const((1, GROUP_DIM)), const((1, GROUP_DIM)), const((1, GROUP_DIM)),
            col(0), col(N_GROUPS), col(2 * N_GROUPS),
        ],
        out_specs=[pl.BlockSpec((None, 1, seq, GROUP_DIM), lambda g, b: (g, b, 0, 0))] * 2,
        out_shape=[jax.ShapeDtypeStruct((N_GROUPS, batch, seq, GROUP_DIM), BF16),
                   jax.ShapeDtypeStruct((N_GROUPS, batch, seq, GROUP_DIM), F32)],
        scratch_shapes=scratch,
        compiler_params=_cparams(("arbitrary", "arbitrary")),
        name="attn",
    )(bias, bsum, qge, qgo, kg, u3, u3, u3)


def _merge_ffn_kernel(x_ref, c_ref, o0_ref, o1_ref, o2_ref, l0_ref, l1_ref, l2_ref,
                      gc0_ref, gc1_ref, ga0_ref, ga1_ref, wc_ref, wa_ref, wo_ref,
                      g2_ref, w1_ref, w2_ref, out_ref):
    l0, l1, l2 = l0_ref[...], l1_ref[...], l2_ref[...]
    m = jnp.maximum(jnp.maximum(l0, l1), l2)
    e0, e1, e2 = jnp.exp(l0 - m), jnp.exp(l1 - m), jnp.exp(l2 - m)
    o = (e0 * o0_ref[...].astype(F32) + e1 * o1_ref[...].astype(F32)
         + e2 * o2_ref[...].astype(F32)) / (e0 + e1 + e2)
    y_attn = jnp.dot(o.astype(BF16), wa_ref[...], preferred_element_type=F32)
    y_conv = jnp.dot(c_ref[...], wc_ref[...], preferred_element_type=F32)
    half = D_MODEL // 2
    mix = []
    for hs, gc_ref, ga_ref in ((slice(0, half), gc0_ref, ga0_ref), (slice(half, D_MODEL), gc1_ref, ga1_ref)):
        gc = jax.nn.sigmoid(gc_ref[...].astype(F32))
        ga = jax.nn.sigmoid(ga_ref[...].astype(F32))
        mix.append((gc * y_conv[:, hs] + ga * y_attn[:, hs]).astype(BF16))
    y = (jnp.dot(mix[0], wo_ref[0:half, :], preferred_element_type=F32)
         + jnp.dot(mix[1], wo_ref[half:D_MODEL, :], preferred_element_type=F32))
    x = x_ref[...] + y

    rn = lax.rsqrt(jnp.mean(x * x, axis=-1, keepdims=True) + EPS)
    h = (x * rn * g2_ref[...]).astype(BF16)
    y2 = None
    for f0 in range(0, D_FF, FF_CHUNK):
        a = jnp.dot(h, w1_ref[:, f0:f0 + FF_CHUNK], preferred_element_type=F32)
        a = jnp.square(jnp.maximum(a, 0.0)).astype(BF16)
        t = jnp.dot(a, w2_ref[f0:f0 + FF_CHUNK, :], preferred_element_type=F32)
        y2 = t if y2 is None else y2 + t
    out_ref[...] = x + y2


def _merge_ffn(x2d, c2d, o_all, lse_all, u2d, wc, wa, wo, g2, w1, w2, layer, m_rows):
    tm = MIX_ROWS
    resident = pl.Buffered(1)
    row = lambda width: pl.BlockSpec((tm, width), lambda i: (i, 0))
    grp = lambda g: pl.BlockSpec((None, tm, GROUP_DIM), functools.partial(lambda i, g: (g, i, 0), g=g))
    groups = [grp(g) for g in range(N_GROUPS)]
    gate = lambda tile: pl.BlockSpec((tm, COL_TILE), lambda i: (i, tile))
    wspec = lambda k, n: pl.BlockSpec((None, k, n), lambda i: (layer, 0, 0), pipeline_mode=resident)
    gate0 = 3 * ATTN_DIM // COL_TILE
    return pl.pallas_call(
        _merge_ffn_kernel,
        grid=(m_rows // tm,),
        in_specs=[row(D_MODEL), row(CONV_DIM)] + groups + groups
                 + [gate(gate0), gate(gate0 + 1), gate(gate0 + 2), gate(gate0 + 3)]
                 + [wspec(CONV_DIM, D_MODEL), wspec(GROUP_DIM, D_MODEL), wspec(D_MODEL, D_MODEL),
                    pl.BlockSpec((None, 1, D_MODEL), lambda i: (layer, 0, 0)),
                    wspec(D_MODEL, D_FF), wspec(D_FF, D_MODEL)],
        out_specs=row(D_MODEL),
        out_shape=jax.ShapeDtypeStruct((m_rows, D_MODEL), F32),
        compiler_params=_cparams(("arbitrary",)),
        name="merge_ffn",
    )(x2d, c2d, *([o_all] * N_GROUPS), *([lse_all] * N_GROUPS), u2d, u2d, u2d, u2d, wc, wa, wo, g2, w1, w2)


def _t5_bucket(dist):
    max_exact = NUM_BUCKETS // 2
    nf = jnp.maximum(dist, 1).astype(jnp.float32)
    large = max_exact + (jnp.log(nf / max_exact) / math.log(MAX_REL_DISTANCE / max_exact)
                         * (NUM_BUCKETS - max_exact)).astype(jnp.int32)
    large = jnp.minimum(large, NUM_BUCKETS - 1)
    return jnp.where(dist < max_exact, dist, large)


def _bucket_tile(d):
    qi = jnp.arange(BLOCK)[:, None]
    kj = jnp.arange(2 * BLOCK)[None, :]
    off = qi + BLOCK - kj
    band = (off >= 0) & (off <= SUB_WINDOW)
    bucket = _t5_bucket(jnp.clip(off, 0, SUB_WINDOW) * d)
    return jnp.where(band, bucket, -1).astype(jnp.int32)


def kernel(x, rel_bias, norm1_g, w_in, q_norm_g, k_norm_g, conv_dw_w, conv_dw_b, conv_ln_g, conv_ln_b,
           w_conv_out, w_attn_out, w_out, norm2_g, w_ff1, w_ff2):
    batch, seq, _ = x.shape
    depth = w_in.shape[0]
    m_rows = batch * seq
    assert seq % (BLOCK * max(DILATIONS)) == 0 and x.shape[2] == D_MODEL

    w_conv_b, w_attn_b, w_out_b = (w.astype(BF16) for w in (w_conv_out, w_attn_out, w_out))
    w_ff1_b, w_ff2_b = w_ff1.astype(BF16), w_ff2.astype(BF16)
    head_of_col = jnp.arange(GROUP_DIM) // HEAD_DIM
    even_head = (head_of_col % 2 == 0).astype(F32)[None, :]
    qg_all = jnp.tile(q_norm_g, (1, HEADS_PER_GROUP)) * (HEAD_DIM ** -0.5)
    qge_all, qgo_all = qg_all * even_head, qg_all * (1.0 - even_head)
    kg_all = jnp.tile(k_norm_g, (1, HEADS_PER_GROUP))
    head_of_half = head_of_col[:GROUP_DIM // 2]
    bsum = ((head_of_half[:, None] == head_of_half[None, :]).astype(F32) * (1.0 / HEAD_DIM)).astype(BF16)
    bkts = jnp.stack([_bucket_tile(d) for d in DILATIONS])
    rbs = rel_bias.astype(F32).reshape(NUM_BUCKETS, N_GROUPS, HEADS_PER_GROUP).transpose(1, 0, 2).reshape(-1)
    bias = _bias_tiles(bkts, rbs)

    vec3 = lambda a: a.reshape(depth, 1, a.shape[-1])
    norm1_g, norm2_g, conv_dw_b, conv_ln_g, conv_ln_b = (vec3(a) for a in (norm1_g, norm2_g, conv_dw_b, conv_ln_g, conv_ln_b))
    x2d = x.reshape(m_rows, D_MODEL)
    for layer in range(depth):
        u2d, c2d = _in_proj(x2d, norm1_g, w_in, conv_dw_w, conv_dw_b, conv_ln_g, conv_ln_b, layer, batch, seq)
        u3 = u2d.reshape(batch, seq, U_COLS)
        sl = slice(layer, layer + 1)
        o_all, lse_all = _attn_all(u3, bias, bsum, qge_all[sl], qgo_all[sl], kg_all[sl], batch, seq)
        x2d = _merge_ffn(x2d, c2d, o_all.reshape(N_GROUPS, m_rows, GROUP_DIM),
                         lse_all.reshape(N_GROUPS, m_rows, GROUP_DIM), u2d, w_conv_b, w_attn_b, w_out_b,
                         norm2_g, w_ff1_b, w_ff2_b, layer, m_rows)
    return x2d.reshape(batch, seq, D_MODEL)
```

```python
import functools
import math

import jax
import jax.numpy as jnp
from jax import lax
from jax.experimental import pallas as pl
from jax.experimental.pallas import tpu as pltpu

F32 = jnp.float32
BF16 = jnp.bfloat16

D_MODEL = 1024
CONV_DIM = 512
CONV_WIDTH = 31
N_GROUPS = 3
HEADS_PER_GROUP = 8
HEAD_DIM = 64
GROUP_DIM = HEADS_PER_GROUP * HEAD_DIM
ATTN_DIM = N_GROUPS * GROUP_DIM
DILATIONS = (1, 4, 16)
SUB_WINDOW = 128
BLOCK = 128
NUM_BUCKETS = 32
MAX_REL_DISTANCE = 2048
D_FF = 4 * D_MODEL
EPS = 1e-6
NEG_INF = -1e30
IN_COLS = 2 * CONV_DIM + 3 * ATTN_DIM + 2 * D_MODEL

LANES = 128
SUBLANES = 8
COL_TILE = 512
N_COL_TILES = IN_COLS // COL_TILE
N_CONV_COL_TILES = 2 * CONV_DIM // COL_TILE
U_COLS = IN_COLS - 2 * CONV_DIM
N_SLABS = D_MODEL // LANES
N_PAIRS = HEADS_PER_GROUP // 2
CHAIN_BLOCKS = 4
MIX_ROWS = 512
FF_CHUNK = 2048
CONV_HALO = 32
CONV_TILE = 256
CONV_CHUNK = 128
VMEM_LIMIT = 56 * 1024 * 1024


def _cparams(sem):
    return pltpu.CompilerParams(dimension_semantics=sem, vmem_limit_bytes=VMEM_LIMIT)


def _slab(c):
    return slice(c * LANES, (c + 1) * LANES)


def _in_proj_kernel(*refs, seq):
    x_refs = refs[:N_SLABS]
    (g_ref, w_ref, dw_ref, db_ref, lg_ref, lb_ref, u_ref, c_ref,
     h_ref, rn_ref, tmp_ref, uc_ref, z_ref, acc_ref) = refs[N_SLABS:]
    j = pl.program_id(1)
    rc = 128
    quarter = seq // 4

    @pl.when((pl.program_id(0) == 0) & (j == 0))
    def _():
        uc_ref[...] = jnp.zeros(uc_ref.shape, BF16)
        z_ref[...] = jnp.zeros(z_ref.shape, F32)
        acc_ref[...] = jnp.zeros(acc_ref.shape, F32)

    @pl.when(j == 0)
    def _():
        def natural(ci, carry):
            rows = pl.ds(pl.multiple_of(ci * rc, rc), rc)
            xs = [x_refs[c][rows, :] for c in range(N_SLABS)]
            ss = xs[0] * xs[0]
            for c in range(1, N_SLABS):
                ss = ss + xs[c] * xs[c]
            rn = lax.rsqrt(jnp.sum(ss, axis=-1, keepdims=True) * (1.0 / D_MODEL) + EPS)
            rn_ref[rows, :] = jnp.broadcast_to(rn, (rc, LANES))
            for c in range(N_SLABS):
                h_ref[0, rows, _slab(c)] = (xs[c] * rn * g_ref[:, _slab(c)]).astype(BF16)
            return carry

        lax.fori_loop(0, seq // rc, natural, 0, unroll=8)

        def by_four(r4, carry):
            base = pl.multiple_of(r4 * quarter, quarter)
            for a0 in range(0, quarter, rc):
                src = pl.ds(a0 * 4 + r4, rc, stride=4)
                rn = rn_ref[src, :]
                for c in range(N_SLABS):
                    y = x_refs[c][src, :] * rn * g_ref[:, _slab(c)]
                    h_ref[1, pl.ds(base + a0, rc), _slab(c)] = y.astype(BF16)
                    tmp_ref[c, a0:a0 + rc, :] = y
            for r2 in range(4):
                for c in range(N_SLABS):
                    h_ref[2, pl.ds(base + r2 * rc, rc), _slab(c)] = (
                        tmp_ref[c, pl.ds(r2, rc, stride=4), :].astype(BF16))
            return carry

        lax.fori_loop(0, 4, by_four, 0, unroll=2)

    jj = jnp.clip(j - 2, 0, 8)
    sel = jnp.where((j >= 2) & (j <= 10), lax.rem(jj, 3), 0)

    n_iter = CONV_TILE // CONV_CHUNK
    n_chunks = seq // CONV_CHUNK
    dot_rows = seq // n_iter

    def body(i, carry):
        rows = pl.ds(pl.multiple_of(i * dot_rows, dot_rows), dot_rows)
        u_ref[rows, :] = jnp.dot(h_ref[sel, rows, :], w_ref[...].astype(BF16),
                                 preferred_element_type=F32).astype(BF16)
        tap_chunk = jnp.clip(j - 3, 0, seq // CONV_TILE - 1) * n_iter + i
        glu_chunk = jnp.clip((j - 3) * n_iter + i + 1, 0, n_chunks - 1)
        chunk_rows = pl.ds(pl.multiple_of(i * CONV_CHUNK, CONV_CHUNK), CONV_CHUNK)
        _conv_norm(acc_ref, (j + 1) & 1, chunk_rows, lg_ref, lb_ref, c_ref)
        _conv_taps(z_ref, pl.multiple_of(tap_chunk * CONV_CHUNK, CONV_CHUNK), dw_ref, db_ref,
                   acc_ref, j & 1, chunk_rows)
        _conv_glu(uc_ref, z_ref, pl.multiple_of(glu_chunk * CONV_CHUNK, CONV_CHUNK))
        return carry

    conv_active = (j >= 2) & (j <= seq // CONV_TILE + 3)

    @pl.when(conv_active)
    def _():
        lax.fori_loop(0, n_iter, body, 0)

    @pl.when(jnp.logical_not(conv_active))
    def _():
        u_ref[...] = jnp.dot(h_ref[sel], w_ref[...].astype(BF16), preferred_element_type=F32).astype(BF16)

    @pl.when(j == 0)
    def _():
        uc_ref[:, 0:COL_TILE] = u_ref[...]

    @pl.when(j == 1)
    def _():
        uc_ref[:, COL_TILE:2 * COL_TILE] = u_ref[...]


def _conv_glu(uc_ref, z_ref, t0):
    a = uc_ref[pl.ds(t0, CONV_CHUNK), 0:CONV_DIM].astype(F32)
    gt = uc_ref[pl.ds(t0, CONV_CHUNK), CONV_DIM:2 * CONV_DIM].astype(F32)
    z_ref[pl.ds(t0 + CONV_HALO, CONV_CHUNK), :] = a * jax.nn.sigmoid(gt)


def _conv_norm(acc_ref, slot, chunk_rows, lg_ref, lb_ref, c_ref):
    acc = acc_ref[slot, chunk_rows, :]
    mu = jnp.mean(acc, axis=-1, keepdims=True)
    xc = acc - mu
    y = xc * lax.rsqrt(jnp.mean(xc * xc, axis=-1, keepdims=True) + EPS)
    y = y * lg_ref[...] + lb_ref[...]
    c_ref[chunk_rows, :] = (y * jax.nn.sigmoid(y)).astype(BF16)


def _conv_taps(z_ref, t0, w_ref, b_ref, acc_ref, slot, chunk_rows):
    chunk = CONV_CHUNK
    first_tap = CONV_HALO - (CONV_WIDTH - 1)
    n_win = chunk + CONV_HALO
    for lt in range(CONV_DIM // LANES):
        ls = _slab(lt)
        window = z_ref[pl.ds(t0, n_win), ls]
        acc = None
        for rho in range(SUBLANES):
            offs = [o for o in range(first_tap, first_tap + CONV_WIDTH) if o % SUBLANES == rho]
            rolled = window if rho == 0 else pltpu.roll(window, n_win - rho, axis=0)
            terms = [w_ref[o - first_tap:o - first_tap + 1, ls] * rolled[o - rho:o - rho + chunk, :]
                     for o in offs]
            while len(terms) > 1:
                terms = [terms[i] + terms[i + 1] for i in range(0, len(terms) - 1, 2)] + (
                    [terms[-1]] if len(terms) % 2 else [])
            acc = terms[0] if acc is None else acc + terms[0]
        acc_ref[slot, chunk_rows, ls] = acc + b_ref[:, ls]


def _in_proj(x2d, g, w, dw_w, dw_b, ln_g, ln_b, layer, batch, seq):
    assert seq // 16 == 128 and COL_TILE == CONV_DIM
    n_conv_tiles = seq // CONV_TILE
    x_specs = [pl.BlockSpec((seq, LANES),
                            functools.partial(lambda b, j, c: (jnp.minimum(b + jnp.minimum(j, 1), batch - 1), c), c=c))
               for c in range(N_SLABS)]
    vec = lambda: pl.BlockSpec((None, 1, CONV_DIM), lambda b, j: (layer, 0, 0))
    return pl.pallas_call(
        functools.partial(_in_proj_kernel, seq=seq),
        grid=(batch, N_COL_TILES),
        in_specs=x_specs + [
            pl.BlockSpec((None, 1, D_MODEL), lambda b, j: (layer, 0, 0)),
            pl.BlockSpec((None, D_MODEL, COL_TILE), lambda b, j: (layer, 0, j)),
            pl.BlockSpec((None, CONV_WIDTH, CONV_DIM), lambda b, j: (layer, 0, 0)),
            vec(), vec(), vec(),
        ],
        out_specs=[
            pl.BlockSpec((seq, COL_TILE), lambda b, j: (b, jnp.maximum(j - N_CONV_COL_TILES, 0))),
            pl.BlockSpec((CONV_TILE, CONV_DIM),
                         lambda b, j: (b * n_conv_tiles + jnp.clip(j - 4, 0, n_conv_tiles - 1), 0)),
        ],
        out_shape=[jax.ShapeDtypeStruct((batch * seq, U_COLS), BF16),
                   jax.ShapeDtypeStruct((batch * seq, CONV_DIM), BF16)],
        scratch_shapes=[pltpu.VMEM((N_GROUPS, seq, D_MODEL), BF16),
                        pltpu.VMEM((seq, LANES), F32),
                        pltpu.VMEM((N_SLABS, seq // 4, LANES), F32),
                        pltpu.VMEM((seq, 2 * CONV_DIM), BF16),
                        pltpu.VMEM((CONV_HALO + seq, CONV_DIM), F32),
                        pltpu.VMEM((2, CONV_TILE, CONV_DIM), F32)],
        compiler_params=_cparams(("arbitrary", "arbitrary")),
        name="in_proj",
    )(*([x2d] * N_SLABS), g, w, dw_w, dw_b, ln_g, ln_b)


def _attn_kernel(*refs, seq):
    for group, d in enumerate(DILATIONS):
        pl.when(pl.program_id(0) == group)(functools.partial(_attn_body, *refs, seq=seq, d=d, group=group))


def _bias_kernel(bkt_ref, rb_ref, bm_ref):
    group = pl.program_id(0)
    bk = bkt_ref[0]
    for h in range(HEADS_PER_GROUP):
        acc = jnp.full((BLOCK, 2 * BLOCK), NEG_INF, F32)
        for b in range(NUM_BUCKETS):
            acc = jnp.where(bk == b, rb_ref[(group * NUM_BUCKETS + b) * HEADS_PER_GROUP + h], acc)
        bm_ref[0, h // 2, (h % 2) * BLOCK:(h % 2 + 1) * BLOCK, :] = acc


def _bias_tiles(bkts, rbs):
    return pl.pallas_call(
        _bias_kernel,
        grid=(N_GROUPS,),
        in_specs=[pl.BlockSpec((1, BLOCK, 2 * BLOCK), lambda g: (g, 0, 0)),
                  pl.BlockSpec(memory_space=pltpu.SMEM)],
        out_specs=pl.BlockSpec((1, N_PAIRS, 2 * BLOCK, 2 * BLOCK), lambda g: (g, 0, 0, 0)),
        out_shape=jax.ShapeDtypeStruct((N_GROUPS, N_PAIRS, 2 * BLOCK, 2 * BLOCK), F32),
        compiler_params=_cparams(("arbitrary",)),
        name="bias_tiles",
    )(bkts, rbs)


def _attn_body(bm_ref, bsum_ref, qge_ref, qgo_ref, kg_ref, q_ref, k_ref, v_ref,
               o_ref, lse_ref,
               qe_ref, qo_ref, kn_ref, va_ref, s_ref, *stage, seq, d, group):
    sub_len = seq // d
    nb = sub_len // BLOCK
    bm_ref = bm_ref.at[0]

    @pl.when((pl.program_id(0) == 0) & (pl.program_id(1) == 0))
    def _():
        va_ref[:, :, LANES:2 * LANES] = jnp.ones((N_PAIRS, seq, LANES), BF16)

    rc = 256

    def head_mean_sq(t):
        sq = t * t
        half = GROUP_DIM // 2
        return jnp.concatenate(
            [jnp.dot(sq[:, 0:half], bsum_ref[...], preferred_element_type=F32),
             jnp.dot(sq[:, half:GROUP_DIM], bsum_ref[...], preferred_element_type=F32)], axis=1)

    def norm_body(ci, carry):
        rows = pl.ds(pl.multiple_of(ci * rc, rc), rc)
        q = q_ref[0, rows, :]
        qr = q.astype(F32) * lax.rsqrt(head_mean_sq(q) + EPS)
        qe_ref[rows, :] = (qr * qge_ref[...]).astype(BF16)
        qo_ref[rows, :] = (qr * qgo_ref[...]).astype(BF16)
        k = k_ref[0, rows, :]
        kr = k.astype(F32) * lax.rsqrt(head_mean_sq(k) + EPS)
        kn_ref[rows, :] = (kr * kg_ref[...]).astype(BF16)
        for p in range(N_PAIRS):
            va_ref[p, rows, 0:LANES] = v_ref[0, rows, _slab(p)]
        return carry

    lax.fori_loop(0, seq // rc, norm_body, 0, unroll=8)

    lt64 = lax.broadcasted_iota(jnp.int32, (BLOCK, LANES), 1) < HEAD_DIM

    def block_aligned(row):
        return row if isinstance(row, int) else pl.multiple_of(row, BLOCK)

    def key_rows(row0, first):
        return pl.ds(row0, BLOCK) if first else pl.ds(block_aligned(row0 - BLOCK), 2 * BLOCK)

    def score_products(row0, first):
        row0 = block_aligned(row0)
        qrows, krows = pl.ds(row0, BLOCK), key_rows(row0, first)
        out = []
        for p in range(N_PAIRS):
            q2 = jnp.concatenate([qe_ref[qrows, _slab(p)], qo_ref[qrows, _slab(p)]], axis=0)
            out.append(lax.dot_general(q2, kn_ref[krows, _slab(p)], (((1,), (1,)), ((), ())),
                                       preferred_element_type=F32))
        return out

    def width(first):
        return BLOCK if first else 2 * BLOCK

    def park(slot, scores, first):
        for p in range(N_PAIRS):
            s_ref[slot, p, :, 0:width(first)] = scores[p]

    def unpark(slot, first):
        return [s_ref[slot, p, :, 0:width(first)] for p in range(N_PAIRS)]

    def finish_block(scores, row0, t0, first):
        row0 = block_aligned(row0)
        qrows, krows = pl.ds(row0, BLOCK), key_rows(row0, first)
        probs, maxes = [], []
        for p in range(N_PAIRS):
            s = scores[p] + (bm_ref[p, :, BLOCK:2 * BLOCK] if first else bm_ref[p])
            m = jnp.max(s, axis=-1, keepdims=True)
            probs.append(jnp.exp(s - m).astype(BF16))
            maxes.append(m)
        results = [jnp.dot(probs[p], va_ref[p, krows, :], preferred_element_type=F32) for p in range(N_PAIRS)]
        for p in range(N_PAIRS):
            re, ro = results[p][:BLOCK], results[p][BLOCK:]
            me, mo = maxes[p][:BLOCK], maxes[p][BLOCK:]
            denom = jnp.where(lt64, re[:, LANES:], ro[:, LANES:])
            o_pair = jnp.where(lt64, re[:, :LANES], ro[:, :LANES]) / denom
            lse_pair = jnp.where(lt64, me, mo) + jnp.log(denom)
            if d == 1:
                o_ref[0, qrows, _slab(p)] = o_pair.astype(BF16)
                lse_ref[0, qrows, _slab(p)] = lse_pair
            else:
                nat = pl.ds(t0, BLOCK, stride=d)
                stage[0][p, nat, :] = o_pair
                stage[1][p, nat, :] = lse_pair

    def run_chain(base, mode, t0_of, n_blocks):
        def first(n):
            return mode == "all" or (mode == "head" and n == 0)

        def row(n):
            return base + n * BLOCK

        park(0, score_products(row(0), first(0)), first(0))
        park(1, score_products(row(1), first(1)), first(1))
        finish_block(unpark(0, first(0)), row(0), t0_of(0), first(0))

        def blk_body(n, c):
            slot = n & 1
            prev = unpark(1 - slot, first(1))
            park(slot, score_products(row(n), first(2)), first(2))
            finish_block(prev, row(n - 1), t0_of(n - 1), first(1))
            return c

        lax.fori_loop(2, n_blocks, blk_body, 0, unroll=n_blocks - 2)
        last = n_blocks - 1
        finish_block(unpark(last & 1, first(last)), row(last), t0_of(last), first(last))

    def residue_of_class(c):
        if isinstance(c, int):
            return c // 4 + 4 * (c % 4)
        return lax.shift_right_logical(c, 2) + 4 * (c & 3)

    if d == 1:
        n_seg = nb // CHAIN_BLOCKS
        run_chain(0, "head", lambda n: n * BLOCK, CHAIN_BLOCKS)

        def segment_body(sg, carry):
            base = pl.multiple_of(sg * (CHAIN_BLOCKS * BLOCK), BLOCK)
            run_chain(base, "none", lambda n: base + n * BLOCK, CHAIN_BLOCKS)
            return carry

        lax.fori_loop(1, n_seg, segment_body, 0, unroll=n_seg - 1)
    elif nb > 1:
        assert nb == CHAIN_BLOCKS

        def class_body(r, carry):
            run_chain(pl.multiple_of(r * sub_len, BLOCK), "head", lambda n: n * (BLOCK * d) + r, nb)
            return carry

        lax.fori_loop(0, d, class_body, 0, unroll=d)
    else:
        def class_group(cg, carry):
            run_chain(pl.multiple_of(cg * (CHAIN_BLOCKS * BLOCK), BLOCK), "all",
                      lambda n: residue_of_class(cg * CHAIN_BLOCKS + n), CHAIN_BLOCKS)
            return carry

        lax.fori_loop(0, d // CHAIN_BLOCKS, class_group, 0, unroll=d // CHAIN_BLOCKS)

    if d > 1:
        def copy_body(ci, carry):
            rows = pl.ds(pl.multiple_of(ci * rc, rc), rc)
            for p in range(N_PAIRS):
                o_ref[0, rows, _slab(p)] = stage[0][p, rows, :].astype(BF16)
                lse_ref[0, rows, _slab(p)] = stage[1][p, rows, :]
            return carry

        lax.fori_loop(0, seq // rc, copy_body, 0, unroll=4)


def _attn_all(u3, bias, bsum, qge, qgo, kg, batch, seq):
    assert DILATIONS == (1, 4, 16) and seq // max(DILATIONS) == BLOCK
    const = lambda shape: pl.BlockSpec(shape, lambda g, b: (0,) * len(shape))
    col = lambda first_tile: pl.BlockSpec((1, seq, GROUP_DIM), lambda g, b: (b, 0, first_tile + g))
    scratch = [
        pltpu.VMEM((seq, GROUP_DIM), BF16),
        pltpu.VMEM((seq, GROUP_DIM), BF16),
        pltpu.VMEM((seq, GROUP_DIM), BF16),
        pltpu.VMEM((N_PAIRS, seq, 2 * LANES), BF16),
        pltpu.VMEM((2, N_PAIRS, 2 * BLOCK, 2 * BLOCK), F32),
        pltpu.VMEM((N_PAIRS, seq, LANES), F32),
        pltpu.VMEM((N_PAIRS, seq, LANES), F32),
    ]
    return pl.pallas_call(
        functools.partial(_attn_kernel, seq=seq),
        grid=(N_GROUPS, batch),
        in_specs=[
            pl.BlockSpec((1, N_PAIRS, 2 * BLOCK, 2 * BLOCK), lambda g, b: (g, 0, 0, 0)),
            const((GROUP_DIM // 2, GROUP_DIM // 2)),
            const((1, GROUP_DIM)), const((1, GROUP_DIM)), const((1, GROUP_DIM)),
            col(0), col(N_GROUPS), col(2 * N_GROUPS),
        ],
        out_specs=[pl.BlockSpec((None, 1, seq, GROUP_DIM), lambda g, b: (g, b, 0, 0))] * 2,
        out_shape=[jax.ShapeDtypeStruct((N_GROUPS, batch, seq, GROUP_DIM), BF16),
                   jax.ShapeDtypeStruct((N_GROUPS, batch, seq, GROUP_DIM), F32)],
        scratch_shapes=scratch,
        compiler_params=_cparams(("arbitrary", "arbitrary")),
        name="attn",
    )(bias, bsum, qge, qgo, kg, u3, u3, u3)


def _merge_ffn_kernel(x_ref, c_ref, o0_ref, o1_ref, o2_ref, l0_ref, l1_ref, l2_ref,
                      gc0_ref, gc1_ref, ga0_ref, ga1_ref, wc_ref, wa_ref, wo_ref,
                      g2_ref, w1_ref, w2_ref, out_ref):
    l0, l1, l2 = l0_ref[...], l1_ref[...], l2_ref[...]
    m = jnp.maximum(jnp.maximum(l0, l1), l2)
    e0, e1, e2 = jnp.exp(l0 - m), jnp.exp(l1 - m), jnp.exp(l2 - m)
    o = (e0 * o0_ref[...].astype(F32) + e1 * o1_ref[...].astype(F32)
         + e2 * o2_ref[...].astype(F32)) / (e0 + e1 + e2)
    y_attn = jnp.dot(o.astype(BF16), wa_ref[...], preferred_element_type=F32)
    y_conv = jnp.dot(c_ref[...], wc_ref[...], preferred_element_type=F32)
    half = D_MODEL // 2
    mix = []
    for hs, gc_ref, ga_ref in ((slice(0, half), gc0_ref, ga0_ref), (slice(half, D_MODEL), gc1_ref, ga1_ref)):
        gc = jax.nn.sigmoid(gc_ref[...].astype(F32))
        ga = jax.nn.sigmoid(ga_ref[...].astype(F32))
        mix.append((gc * y_conv[:, hs] + ga * y_attn[:, hs]).astype(BF16))
    y = (jnp.dot(mix[0], wo_ref[0:half, :], preferred_element_type=F32)
         + jnp.dot(mix[1], wo_ref[half:D_MODEL, :], preferred_element_type=F32))
    x = x_ref[...] + y

    rn = lax.rsqrt(jnp.mean(x * x, axis=-1, keepdims=True) + EPS)
    h = (x * rn * g2_ref[...]).astype(BF16)
    y2 = None
    for f0 in range(0, D_FF, FF_CHUNK):
        a = jnp.dot(h, w1_ref[:, f0:f0 + FF_CHUNK], preferred_element_type=F32)
        a = jnp.square(jnp.maximum(a, 0.0)).astype(BF16)
        t = jnp.dot(a, w2_ref[f0:f0 + FF_CHUNK, :], preferred_element_type=F32)
        y2 = t if y2 is None else y2 + t
    out_ref[...] = x + y2


def _merge_ffn(x2d, c2d, o_all, lse_all, u2d, wc, wa, wo, g2, w1, w2, layer, m_rows):
    tm = MIX_ROWS
    resident = pl.Buffered(1)
    row = lambda width: pl.BlockSpec((tm, width), lambda i: (i, 0))
    grp = lambda g: pl.BlockSpec((None, tm, GROUP_DIM), functools.partial(lambda i, g: (g, i, 0), g=g))
    groups = [grp(g) for g in range(N_GROUPS)]
    gate = lambda tile: pl.BlockSpec((tm, COL_TILE), lambda i: (i, tile))
    wspec = lambda k, n: pl.BlockSpec((None, k, n), lambda i: (layer, 0, 0), pipeline_mode=resident)
    gate0 = 3 * ATTN_DIM // COL_TILE
    return pl.pallas_call(
        _merge_ffn_kernel,
        grid=(m_rows // tm,),
        in_specs=[row(D_MODEL), row(CONV_DIM)] + groups + groups
                 + [gate(gate0), gate(gate0 + 1), gate(gate0 + 2), gate(gate0 + 3)]
                 + [wspec(CONV_DIM, D_MODEL), wspec(GROUP_DIM, D_MODEL), wspec(D_MODEL, D_MODEL),
                    pl.BlockSpec((None, 1, D_MODEL), lambda i: (layer, 0, 0)),
                    wspec(D_MODEL, D_FF), wspec(D_FF, D_MODEL)],
        out_specs=row(D_MODEL),
        out_shape=jax.ShapeDtypeStruct((m_rows, D_MODEL), F32),
        compiler_params=_cparams(("arbitrary",)),
        name="merge_ffn",
    )(x2d, c2d, *([o_all] * N_GROUPS), *([lse_all] * N_GROUPS), u2d, u2d, u2d, u2d, wc, wa, wo, g2, w1, w2)


def _t5_bucket(dist):
    max_exact = NUM_BUCKETS // 2
    nf = jnp.maximum(dist, 1).astype(jnp.float32)
    large = max_exact + (jnp.log(nf / max_exact) / math.log(MAX_REL_DISTANCE / max_exact)
                         * (NUM_BUCKETS - max_exact)).astype(jnp.int32)
    large = jnp.minimum(large, NUM_BUCKETS - 1)
    return jnp.where(dist < max_exact, dist, large)


def _bucket_tile(d):
    qi = jnp.arange(BLOCK)[:, None]
    kj = jnp.arange(2 * BLOCK)[None, :]
    off = qi + BLOCK - kj
    band = (off >= 0) & (off <= SUB_WINDOW)
    bucket = _t5_bucket(jnp.clip(off, 0, SUB_WINDOW) * d)
    return jnp.where(band, bucket, -1).astype(jnp.int32)


def kernel(x, rel_bias, norm1_g, w_in, q_norm_g, k_norm_g, conv_dw_w, conv_dw_b, conv_ln_g, conv_ln_b,
           w_conv_out, w_attn_out, w_out, norm2_g, w_ff1, w_ff2):
    batch, seq, _ = x.shape
    depth = w_in.shape[0]
    m_rows = batch * seq
    assert seq % (BLOCK * max(DILATIONS)) == 0 and x.shape[2] == D_MODEL

    w_conv_b, w_attn_b, w_out_b = (w.astype(BF16) for w in (w_conv_out, w_attn_out, w_out))
    w_ff1_b, w_ff2_b = w_ff1.astype(BF16), w_ff2.astype(BF16)
    head_of_col = jnp.arange(GROUP_DIM) // HEAD_DIM
    even_head = (head_of_col % 2 == 0).astype(F32)[None, :]
    qg_all = jnp.tile(q_norm_g, (1, HEADS_PER_GROUP)) * (HEAD_DIM ** -0.5)
    qge_all, qgo_all = qg_all * even_head, qg_all * (1.0 - even_head)
    kg_all = jnp.tile(k_norm_g, (1, HEADS_PER_GROUP))
    head_of_half = head_of_col[:GROUP_DIM // 2]
    bsum = ((head_of_half[:, None] == head_of_half[None, :]).astype(F32) * (1.0 / HEAD_DIM)).astype(BF16)
    bkts = jnp.stack([_bucket_tile(d) for d in DILATIONS])
    rbs = rel_bias.astype(F32).reshape(NUM_BUCKETS, N_GROUPS, HEADS_PER_GROUP).transpose(1, 0, 2).reshape(-1)
    bias = _bias_tiles(bkts, rbs)

    vec3 = lambda a: a.reshape(depth, 1, a.shape[-1])
    norm1_g, norm2_g, conv_dw_b, conv_ln_g, conv_ln_b = (vec3(a) for a in (norm1_g, norm2_g, conv_dw_b, conv_ln_g, conv_ln_b))
    x2d = x.reshape(m_rows, D_MODEL)
    for layer in range(depth):
        u2d, c2d = _in_proj(x2d, norm1_g, w_in, conv_dw_w, conv_dw_b, conv_ln_g, conv_ln_b, layer, batch, seq)
        u3 = u2d.reshape(batch, seq, U_COLS)
        sl = slice(layer, layer + 1)
        o_all, lse_all = _attn_all(u3, bias, bsum, qge_all[sl], qgo_all[sl], kg_all[sl], batch, seq)
        x2d = _merge_ffn(x2d, c2d, o_all.reshape(N_GROUPS, m_rows, GROUP_DIM),
                         lse_all.reshape(N_GROUPS, m_rows, GROUP_DIM), u2d, w_conv_b, w_attn_b, w_out_b,
                         norm2_g, w_ff1_b, w_ff2_b, layer, m_rows)
    return x2d.reshape(batch, seq, D_MODEL)
```
